```python
import math
import jax, jax.numpy as jnp
from jax import lax
import numpy as np

D_MODEL = 1024
BATCH = 16
SEQ = 2048
DEPTH = 1

HEAD_DIM = 64
N_DIFF_HEADS = 4
DIFF_V_DIM = 2 * HEAD_DIM
N_DIL_HEADS = 8
DIL_PATTERNS = ((128, 1), (512, 4), (2048, 16))
Q_BLOCK = 128
D_FF = 2816
RMS_EPS = 1e-6
LAMBDA_STD = 0.1

DIFF_QK_WIDTH = N_DIFF_HEADS * 2 * HEAD_DIM
DIFF_V_WIDTH = N_DIFF_HEADS * DIFF_V_DIM
DIL_WIDTH = N_DIL_HEADS * HEAD_DIM
IN_PROJ_WIDTH = 2 * DIFF_QK_WIDTH + DIFF_V_WIDTH + 3 * DIL_WIDTH
MIX_WIDTH = DIFF_V_WIDTH + DIL_WIDTH
N_ALIBI_HEADS = N_DIFF_HEADS + N_DIL_HEADS

kernel_name = "hybrid_diffattn_dilated_macaron_layer"


def rms_norm(x, g):
    xf = x.astype(jnp.float32)
    y = xf * lax.rsqrt(jnp.mean(xf * xf, axis=-1, keepdims=True) + RMS_EPS)
    return (y * g.astype(jnp.float32)).astype(x.dtype)


def swiglu(x, w_gate, w_up, w_down):
    return (jax.nn.silu(x @ w_gate) * (x @ w_up)) @ w_down


def alibi_slopes():
    all_s = 2.0 ** (-8.0 * jnp.arange(1, N_ALIBI_HEADS + 1, dtype=jnp.float32) / N_ALIBI_HEADS)
    diff_idx = np.arange(0, N_ALIBI_HEADS, N_ALIBI_HEADS // N_DIFF_HEADS)
    dil_idx = np.setdiff1d(np.arange(N_ALIBI_HEADS), diff_idx)
    return all_s[diff_idx], all_s[dil_idx]


def lambda_init_fn(layer):
    return 0.8 - 0.6 * math.exp(-0.3 * layer)


def diff_attention(q, k, v, slopes, lam, lam_init, subln_g):
    B, S, H = q.shape[0], q.shape[1], q.shape[2]
    nb = S // Q_BLOCK
    scale = HEAD_DIM ** -0.5
    qb = q.reshape(B, nb, Q_BLOCK, H, 2, HEAD_DIM).transpose(1, 0, 2, 3, 4, 5)
    pos_k = jnp.arange(S)

    def block(args):
        qi, bi = args
        s = jnp.einsum('bqhcd,bkhcd->bhcqk', qi, k, preferred_element_type=jnp.float32) * scale
        pos_q = bi * Q_BLOCK + jnp.arange(Q_BLOCK)
        dist = pos_q[:, None] - pos_k[None, :]
        bias = -slopes[:, None, None] * dist.astype(jnp.float32)[None]
        s = s + bias[None, :, None]
        s = jnp.where((dist >= 0)[None, None, None], s, -jnp.inf)
        p = jax.nn.softmax(s, axis=-1)
        a = p[:, :, 0] - lam * p[:, :, 1]
        return jnp.einsum('bhqk,bkhd->bqhd', a.astype(v.dtype), v)

    o = lax.map(block, (qb, jnp.arange(nb)))
    o = o.transpose(1, 0, 2, 3, 4).reshape(B, S, H, DIFF_V_DIM)
    o = rms_norm(o, subln_g) * (1.0 - lam_init)
    return o.reshape(B, S, H * DIFF_V_DIM)


def dilated_pattern(q, k, v, slopes, window, dil):
    B, S, H, D = q.shape
    L = S // dil
    n_win = window // dil
    Lp = -(-L // Q_BLOCK) * Q_BLOCK
    nb = Lp // Q_BLOCK
    pad = Lp - L

    def to_blocks(a):
        a = a.reshape(B, L, dil, H, D).transpose(0, 2, 1, 3, 4)
        a = jnp.pad(a, ((0, 0), (0, 0), (0, pad), (0, 0), (0, 0)))
        return a.reshape(B, dil, nb, Q_BLOCK, H, D)

    def with_prev(a):
        prev = jnp.pad(a, ((0, 0), (0, 0), (1, 0), (0, 0), (0, 0), (0, 0)))[:, :, :-1]
        return jnp.concatenate([prev, a], axis=3)

    qb = to_blocks(q)
    kc = with_prev(to_blocks(k))
    vc = with_prev(to_blocks(v))
    s = jnp.einsum('brnqhd,brnkhd->brnhqk', qb, kc, preferred_element_type=jnp.float32) * (D ** -0.5)
    j = jnp.arange(Q_BLOCK)
    c = jnp.arange(2 * Q_BLOCK)
    dsub = (Q_BLOCK + j)[:, None] - c[None, :]
    key_idx = jnp.arange(nb)[:, None, None] * Q_BLOCK - Q_BLOCK + c[None, None, :]
    valid = (dsub >= 0)[None] & (dsub <= n_win)[None] & (key_idx >= 0)
    bias = -slopes[:, None, None] * (dsub * dil).astype(jnp.float32)[None]
    s = s + bias[None, None, None]
    s = jnp.where(valid[None, None, :, None], s, -jnp.inf)
    m = jnp.max(s, axis=-1, keepdims=True)
    e = jnp.exp(s - m)
    l = jnp.sum(e, axis=-1, keepdims=True)
    o = jnp.einsum('brnhqk,brnkhd->brnqhd', (e / l).astype(v.dtype), vc)
    lse = (m + jnp.log(l))[..., 0]
    o = o.reshape(B, dil, Lp, H, D)[:, :, :L].transpose(0, 2, 1, 3, 4).reshape(B, S, H, D)
    lse = lse.transpose(0, 1, 2, 4, 3).reshape(B, dil, Lp, H)[:, :, :L].transpose(0, 2, 1, 3).reshape(B, S, H)
    return o, lse


def dilated_attention(q, k, v, slopes):
    B, S, H, D = q.shape
    outs, lses = [], []
    for window, dil in DIL_PATTERNS:
        o, lse = dilated_pattern(q, k, v, slopes, window, dil)
        outs.append(o)
        lses.append(lse)
    w = jax.nn.softmax(jnp.stack(lses, axis=0), axis=0)
    o = sum(w[i][..., None] * outs[i].astype(jnp.float32) for i in range(len(DIL_PATTERNS)))
    return o.astype(q.dtype).reshape(B, S, H * D)


def hybrid_mixer(h, w_in, lam_q1, lam_k1, lam_q2, lam_k2, subln_g, w_out, layer):
    B, S, _ = h.shape
    proj = h @ w_in
    splits = np.cumsum([DIFF_QK_WIDTH, DIFF_QK_WIDTH, DIFF_V_WIDTH, DIL_WIDTH, DIL_WIDTH])
    q_d, k_d, v_d, q_l, k_l, v_l = jnp.split(proj, splits, axis=-1)
    slopes_d, slopes_l = alibi_slopes()
    lam_init = lambda_init_fn(layer)
    lam = (jnp.exp(jnp.sum(lam_q1.astype(jnp.float32) * lam_k1.astype(jnp.float32)))
           - jnp.exp(jnp.sum(lam_q2.astype(jnp.float32) * lam_k2.astype(jnp.float32))) + lam_init)
    o_d = diff_attention(q_d.reshape(B, S, N_DIFF_HEADS, 2, HEAD_DIM),
                         k_d.reshape(B, S, N_DIFF_HEADS, 2, HEAD_DIM),
                         v_d.reshape(B, S, N_DIFF_HEADS, DIFF_V_DIM),
                         slopes_d, lam, lam_init, subln_g)
    o_l = dilated_attention(q_l.reshape(B, S, N_DIL_HEADS, HEAD_DIM),
                            k_l.reshape(B, S, N_DIL_HEADS, HEAD_DIM),
                            v_l.reshape(B, S, N_DIL_HEADS, HEAD_DIM), slopes_l)
    return jnp.concatenate([o_d, o_l], axis=-1) @ w_out


def setup_inputs(seed: int = 0) -> dict:
    key = jax.random.key(seed)
    ks = jax.random.split(key, 24)
    f32 = jnp.float32

    def w(k, fan_in, fan_out):
        return jax.random.normal(k, (DEPTH, fan_in, fan_out), f32) * fan_in ** -0.5

    def gain(k, n):
        return 1.0 + 0.05 * jax.random.normal(k, (DEPTH, n), f32)

    return {
        "x": jax.random.normal(ks[0], (BATCH, SEQ, D_MODEL), f32),
        "ffn1_pre_g": gain(ks[1], D_MODEL),
        "ffn1_w_gate": w(ks[2], D_MODEL, D_FF),
        "ffn1_w_up": w(ks[3], D_MODEL, D_FF),
        "ffn1_w_down": w(ks[4], D_FF, D_MODEL),
        "ffn1_post_g": gain(ks[5], D_MODEL),
        "mix_pre_g": gain(ks[6], D_MODEL),
        "w_in": w(ks[7], D_MODEL, IN_PROJ_WIDTH),
        "lambda_q1": LAMBDA_STD * jax.random.normal(ks[8], (DEPTH, HEAD_DIM), f32),
        "lambda_k1": LAMBDA_STD * jax.random.normal(ks[9], (DEPTH, HEAD_DIM), f32),
        "lambda_q2": LAMBDA_STD * jax.random.normal(ks[10], (DEPTH, HEAD_DIM), f32),
        "lambda_k2": LAMBDA_STD * jax.random.normal(ks[11], (DEPTH, HEAD_DIM), f32),
        "diff_subln_g": gain(ks[12], DIFF_V_DIM),
        "w_out": w(ks[13], MIX_WIDTH, D_MODEL),
        "mix_post_g": gain(ks[14], D_MODEL),
        "ffn2_pre_g": gain(ks[15], D_MODEL),
        "ffn2_w_gate": w(ks[16], D_MODEL, D_FF),
        "ffn2_w_up": w(ks[17], D_MODEL, D_FF),
        "ffn2_w_down": w(ks[18], D_FF, D_MODEL),
        "ffn2_post_g": gain(ks[19], D_MODEL),
    }


def reference(x, ffn1_pre_g, ffn1_w_gate, ffn1_w_up, ffn1_w_down, ffn1_post_g,
              mix_pre_g, w_in, lambda_q1, lambda_k1, lambda_q2, lambda_k2, diff_subln_g,
              w_out, mix_post_g, ffn2_pre_g, ffn2_w_gate, ffn2_w_up, ffn2_w_down, ffn2_post_g):
    for l in range(DEPTH):
        h = swiglu(rms_norm(x, ffn1_pre_g[l]), ffn1_w_gate[l], ffn1_w_up[l], ffn1_w_down[l])
        x = x + 0.5 * rms_norm(h, ffn1_post_g[l])
        h = hybrid_mixer(rms_norm(x, mix_pre_g[l]), w_in[l], lambda_q1[l], lambda_k1[l],
                         lambda_q2[l], lambda_k2[l], diff_subln_g[l], w_out[l], l)
        x = x + rms_norm(h, mix_post_g[l])
        h = swiglu(rms_norm(x, ffn2_pre_g[l]), ffn2_w_gate[l], ffn2_w_up[l], ffn2_w_down[l])
        x = x + 0.5 * rms_norm(h, ffn2_post_g[l])
    return x
```

```python
import functools
import math

import numpy as np
import jax
import jax.numpy as jnp
from jax import lax
from jax.experimental import pallas as pl
from jax.experimental.pallas import tpu as pltpu

F32 = jnp.float32
BF16 = jnp.bfloat16

D_MODEL = 1024
SEQ = 2048
HEAD_DIM = 64
N_DIFF_HEADS = 4
DIFF_V_DIM = 2 * HEAD_DIM
N_DIL_HEADS = 8
DIL_PATTERNS = ((128, 1), (512, 4), (2048, 16))
Q_BLOCK = 128
D_FF = 2816
RMS_EPS = 1e-6

DIFF_QK_WIDTH = N_DIFF_HEADS * 2 * HEAD_DIM
DIFF_V_WIDTH = N_DIFF_HEADS * DIFF_V_DIM
DIL_WIDTH = N_DIL_HEADS * HEAD_DIM
IN_PROJ_WIDTH = 2 * DIFF_QK_WIDTH + DIFF_V_WIDTH + 3 * DIL_WIDTH
N_ALIBI_HEADS = N_DIFF_HEADS + N_DIL_HEADS
LAM_INIT = 0.8 - 0.6 * math.exp(-0.3 * 0)

LANES = 128
NEG_BIG = -1e30
VMEM_LIMIT_FFN = 56 * 1024 * 1024
VMEM_LIMIT_ATTN = 40 * 1024 * 1024

TM = 256
TQ = 256
TK = 256


def _rms(x, g):
    ms = jnp.mean(x * x, axis=-1, keepdims=True)
    return x * lax.rsqrt(ms + RMS_EPS) * g


def _swiglu_half_step(x, pre_g, wg_ref, wu_ref, wd_ref, post_g):
    xb = _rms(x, pre_g).astype(BF16)
    g = jnp.dot(xb, wg_ref[...], preferred_element_type=F32)
    u = jnp.dot(xb, wu_ref[...], preferred_element_type=F32)
    h = (g * (1.0 / (1.0 + jnp.exp(-g))) * u).astype(BF16)
    y = jnp.dot(h, wd_ref[...], preferred_element_type=F32)
    return x + 0.5 * _rms(y, post_g)


def _ffn1_inproj_kernel(x_ref, pre_g, wg, wu, wd, post_g, mix_g, w_in, x1_ref, proj_ref, vt_ref):
    x1 = _swiglu_half_step(x_ref[...], pre_g[...], wg, wu, wd, post_g[...])
    x1_ref[...] = x1
    hb = _rms(x1, mix_g[...]).astype(BF16)
    proj = jnp.dot(hb, w_in[...], preferred_element_type=F32)
    proj_ref[...] = proj.astype(BF16)
    v_d = proj[:, 2 * DIFF_QK_WIDTH:2 * DIFF_QK_WIDTH + DIFF_V_WIDTH]
    vt_ref[...] = v_d.T.astype(BF16)


def _const_spec(shape):
    nd = len(shape)
    return pl.BlockSpec(shape, lambda *_: (0,) * nd, pipeline_mode=pl.Buffered(1))


def _ffn1_inproj(x2d, pre_g, wg, wu, wd, post_g, mix_g, w_in, batch):
    t = x2d.shape[0]
    tiles_per_seq = SEQ // TM
    row = lambda i: (i, 0)
    return pl.pallas_call(
        _ffn1_inproj_kernel,
        grid=(t // TM,),
        in_specs=[
            pl.BlockSpec((TM, D_MODEL), row),
            _const_spec((1, D_MODEL)),
            _const_spec((D_MODEL, D_FF)),
            _const_spec((D_MODEL, D_FF)),
            _const_spec((D_FF, D_MODEL)),
            _const_spec((1, D_MODEL)),
            _const_spec((1, D_MODEL)),
            _const_spec((D_MODEL, IN_PROJ_WIDTH)),
        ],
        out_specs=[
            pl.BlockSpec((TM, D_MODEL), row),
            pl.BlockSpec((TM, IN_PROJ_WIDTH), row),
            pl.BlockSpec((None, DIFF_V_WIDTH, TM), lambda i: (i // tiles_per_seq, 0, i % tiles_per_seq)),
        ],
        out_shape=[
            jax.ShapeDtypeStruct((t, D_MODEL), F32),
            jax.ShapeDtypeStruct((t, IN_PROJ_WIDTH), BF16),
            jax.ShapeDtypeStruct((batch, DIFF_V_WIDTH, SEQ), BF16),
        ],
        compiler_params=pltpu.CompilerParams(
            dimension_semantics=("arbitrary",), vmem_limit_bytes=VMEM_LIMIT_FFN),
        name="ffn1_inproj",
    )(x2d, pre_g, wg, wu, wd, post_g, mix_g, w_in)


def _outproj_ffn2_kernel(od_ref, ol_ref, x1_ref, wo_d, wo_l, mix_post_g, pre_g, wg, wu, wd, post_g, out_ref):
    h = (jnp.dot(od_ref[...], wo_d[...], preferred_element_type=F32)
         + jnp.dot(ol_ref[...], wo_l[...], preferred_element_type=F32))
    x2 = x1_ref[...] + _rms(h, mix_post_g[...])
    out_ref[...] = _swiglu_half_step(x2, pre_g[...], wg, wu, wd, post_g[...])


def _outproj_ffn2(od, ol, x1, wo_d, wo_l, mix_post_g, pre_g, wg, wu, wd, post_g):
    t = x1.shape[0]
    row = lambda i: (i, 0)
    return pl.pallas_call(
        _outproj_ffn2_kernel,
        grid=(t // TM,),
        in_specs=[
            pl.BlockSpec((TM, DIFF_V_WIDTH), row),
            pl.BlockSpec((TM, DIL_WIDTH), row),
            pl.BlockSpec((TM, D_MODEL), row),
            _const_spec((DIFF_V_WIDTH, D_MODEL)),
            _const_spec((DIL_WIDTH, D_MODEL)),
            _const_spec((1, D_MODEL)),
            _const_spec((1, D_MODEL)),
            _const_spec((D_MODEL, D_FF)),
            _const_spec((D_MODEL, D_FF)),
            _const_spec((D_FF, D_MODEL)),
            _const_spec((1, D_MODEL)),
        ],
        out_specs=pl.BlockSpec((TM, D_MODEL), row),
        out_shape=jax.ShapeDtypeStruct((t, D_MODEL), F32),
        compiler_params=pltpu.CompilerParams(
            dimension_semantics=("arbitrary",), vmem_limit_bytes=VMEM_LIMIT_FFN),
        name="outproj_ffn2",
    )(od, ol, x1, wo_d, wo_l, mix_post_g, pre_g, wg, wu, wd, post_g)


_NT = (((1,), (1,)), ((), ()))


def _diff_attn_kernel(slopes_ref, q_ref, k_ref, vt_ref, lq1, lk1, lq2, lk2, g_ref, o_ref, acc1, acc2):
    slope = slopes_ref[pl.program_id(1)]
    lam = (jnp.exp(jnp.sum(lq1[...] * lk1[...], axis=-1, keepdims=True))
           - jnp.exp(jnp.sum(lq2[...] * lk2[...], axis=-1, keepdims=True)) + LAM_INIT)
    lane = lax.broadcasted_iota(jnp.int32, (1, LANES), 1)
    first_map = lane < HEAD_DIM
    d0 = (lax.broadcasted_iota(jnp.int32, (TK, TQ), 1)
          - lax.broadcasted_iota(jnp.int32, (TK, TQ), 0)).astype(F32)
    bias0 = -slope * d0
    bias_diag = jnp.where(d0 >= 0, bias0, NEG_BIG)
    zero = jnp.zeros((), BF16)

    def q_tile(qi, carry):
        qs = pl.multiple_of(qi * TQ, TQ)
        q = q_ref[pl.ds(qs, TQ), :]
        q1 = jnp.where(first_map, q, zero)
        q2 = jnp.where(first_map, zero, q)

        def scores(j):
            ks = pl.multiple_of(j * TK, TK)
            kt = k_ref[pl.ds(ks, TK), :]
            s1 = lax.dot_general(kt, q1, _NT, preferred_element_type=F32)
            s2 = lax.dot_general(kt, q2, _NT, preferred_element_type=F32)
            return s1, s2, vt_ref[:, pl.ds(ks, TK)]

        s1, s2, vt = scores(qi)
        s1 = s1 + bias_diag
        s2 = s2 + bias_diag
        m1 = jnp.max(s1, axis=0, keepdims=True)
        m2 = jnp.max(s2, axis=0, keepdims=True)
        p1 = jnp.exp(s1 - m1)
        p2 = jnp.exp(s2 - m2)
        l1 = jnp.sum(p1, axis=0, keepdims=True)
        l2 = jnp.sum(p2, axis=0, keepdims=True)
        acc1[...] = jnp.dot(vt, p1.astype(BF16), preferred_element_type=F32)
        acc2[...] = jnp.dot(vt, p2.astype(BF16), preferred_element_type=F32)

        def k_tile(j, stats):
            m1, l1, m2, l2 = stats
            s1, s2, vt = scores(j)
            coff = slope * ((qi - j) * TK).astype(F32)
            s1 = s1 + bias0
            s2 = s2 + bias0
            n1 = jnp.maximum(m1, jnp.max(s1, axis=0, keepdims=True) - coff)
            n2 = jnp.maximum(m2, jnp.max(s2, axis=0, keepdims=True) - coff)
            a1 = jnp.exp(m1 - n1)
            a2 = jnp.exp(m2 - n2)
            p1 = jnp.exp(s1 - (n1 + coff))
            p2 = jnp.exp(s2 - (n2 + coff))
            l1 = a1 * l1 + jnp.sum(p1, axis=0, keepdims=True)
            l2 = a2 * l2 + jnp.sum(p2, axis=0, keepdims=True)
            acc1[...] = a1 * acc1[...] + jnp.dot(vt, p1.astype(BF16), preferred_element_type=F32)
            acc2[...] = a2 * acc2[...] + jnp.dot(vt, p2.astype(BF16), preferred_element_type=F32)
            return n1, l1, n2, l2

        m1, l1, m2, l2 = lax.fori_loop(0, qi, k_tile, (m1, l1, m2, l2))
        o = acc1[...] * (1.0 / l1) - lam * (acc2[...] * (1.0 / l2))
        ms = jnp.mean(o * o, axis=0, keepdims=True)
        y = o * lax.rsqrt(ms + RMS_EPS) * g_ref[...] * (1.0 - LAM_INIT)
        o_ref[pl.ds(qs, TQ), :] = y.T.astype(BF16)
        return carry

    lax.fori_loop(0, SEQ // TQ, q_tile, 0)


def _diff_attention(slopes, proj3, vt, lq1, lk1, lq2, lk2, g_col):
    batch = proj3.shape[0]
    k_off = DIFF_QK_WIDTH // LANES
    vec = pl.BlockSpec((1, HEAD_DIM), lambda b, h: (0, 0))
    return pl.pallas_call(
        _diff_attn_kernel,
        grid=(batch, N_DIFF_HEADS),
        in_specs=[
            pl.BlockSpec(memory_space=pltpu.SMEM),
            pl.BlockSpec((None, SEQ, LANES), lambda b, h: (b, 0, h)),
            pl.BlockSpec((None, SEQ, LANES), lambda b, h: (b, 0, k_off + h)),
            pl.BlockSpec((None, DIFF_V_DIM, SEQ), lambda b, h: (b, h, 0)),
            vec, vec, vec, vec,
            pl.BlockSpec((DIFF_V_DIM, 1), lambda b, h: (0, 0)),
        ],
        out_specs=pl.BlockSpec((None, SEQ, LANES), lambda b, h: (b, 0, h)),
        out_shape=jax.ShapeDtypeStruct((batch, SEQ, DIFF_V_WIDTH), BF16),
        scratch_shapes=[pltpu.VMEM((DIFF_V_DIM, TQ), F32), pltpu.VMEM((DIFF_V_DIM, TQ), F32)],
        compiler_params=pltpu.CompilerParams(
            dimension_semantics=("arbitrary", "arbitrary"), vmem_limit_bytes=VMEM_LIMIT_ATTN),
        name="diff_attention",
    )(slopes, proj3, proj3, vt, lq1, lk1, lq2, lk2, g_col)


N_PATTERNS = len(DIL_PATTERNS)
N_WIN = Q_BLOCK


def _dil_attn_kernel(slopes_ref, q_ref, k_ref, v_ref, o_ref, qf, kf, vf, bias, o_acc, lse_acc):
    hp = pl.program_id(1)
    lane = lax.broadcasted_iota(jnp.int32, (1, LANES), 1)
    first_head = lane < HEAD_DIM

    qf[...] = q_ref[...].astype(F32)
    kf[...] = k_ref[...].astype(F32)
    vf[...] = v_ref[...].astype(F32)

    dsub = (Q_BLOCK + lax.broadcasted_iota(jnp.int32, (Q_BLOCK, 2 * Q_BLOCK), 0)
            - lax.broadcasted_iota(jnp.int32, (Q_BLOCK, 2 * Q_BLOCK), 1))
    valid = (dsub >= 0) & (dsub <= N_WIN)
    dsub_f = dsub.astype(F32)
    for p, (_, dil) in enumerate(DIL_PATTERNS):
        for hh in range(2):
            slope = slopes_ref[2 * hp + hh]
            bias[2 * p + hh] = jnp.where(valid, (-slope * float(dil)) * dsub_f, NEG_BIG)

    def block(p, dil, start, prev_start):
        def rows(ref, s):
            if dil == 1:
                return ref[pl.ds(s, Q_BLOCK), :]
            return ref[pl.ds(s, Q_BLOCK, stride=dil), :]

        q = rows(qf, start).astype(BF16)
        k = rows(kf, start)
        v = rows(vf, start)
        if prev_start is not None:
            k = jnp.concatenate([rows(kf, prev_start), k], axis=0)
            v = jnp.concatenate([rows(vf, prev_start), v], axis=0)
        o = None
        lse = None
        for hh in range(2):
            keep = first_head if hh == 0 else jnp.logical_not(first_head)
            kh = jnp.where(keep, k, 0.0).astype(BF16)
            vh = jnp.where(keep, v, 0.0).astype(BF16)
            s = lax.dot_general(q, kh, _NT, preferred_element_type=F32)
            if prev_start is not None:
                s = s + bias[2 * p + hh]
            else:
                s = s + bias[2 * p + hh, :, Q_BLOCK:]
            m = jnp.max(s, axis=1, keepdims=True)
            e = jnp.exp(s - m)
            l = jnp.sum(e, axis=1, keepdims=True)
            oh = jnp.dot(e.astype(BF16), vh, preferred_element_type=F32) * (1.0 / l)
            lh = m + jnp.log(l)
            o = oh if o is None else o + oh
            lse = lh if lse is None else jnp.where(first_head, lse, lh)
        lse = jnp.broadcast_to(lse, (Q_BLOCK, LANES))
        if dil == 1:
            o_acc[p, pl.ds(start, Q_BLOCK), :] = o
            lse_acc[p, pl.ds(start, Q_BLOCK), :] = lse
        else:
            o_acc[p, pl.ds(start, Q_BLOCK, stride=dil), :] = o
            lse_acc[p, pl.ds(start, Q_BLOCK, stride=dil), :] = lse

    for p, (window, dil) in enumerate(DIL_PATTERNS):
        n_blocks = SEQ // dil // Q_BLOCK
        span = Q_BLOCK * dil
        log2_dil = dil.bit_length() - 1

        def first_block(r, c, p=p, dil=dil):
            block(p, dil, r, None)
            return c

        lax.fori_loop(0, dil, first_block, 0)
        if n_blocks > 1:
            def later_block(i, c, p=p, dil=dil, span=span, log2_dil=log2_dil):
                r = i & (dil - 1)
                n = (i >> log2_dil) + 1
                start = n * span + r
                block(p, dil, start, start - span)
                return c

            lax.fori_loop(0, dil * (n_blocks - 1), later_block, 0)

    rows_per_step = 256

    def blend(i, c):
        rs = pl.multiple_of(i * rows_per_step, rows_per_step)
        sl = pl.ds(rs, rows_per_step)
        lses = [lse_acc[p, sl, :] for p in range(N_PATTERNS)]
        mx = functools.reduce(jnp.maximum, lses)
        ws = [jnp.exp(l - mx) for l in lses]
        den = functools.reduce(lambda a, b: a + b, ws)
        num = functools.reduce(lambda a, b: a + b, [ws[p] * o_acc[p, sl, :] for p in range(N_PATTERNS)])
        o_ref[sl, :] = (num * (1.0 / den)).astype(BF16)
        return c

    lax.fori_loop(0, SEQ // rows_per_step, blend, 0)


def _dilated_attention(slopes, proj3):
    batch = proj3.shape[0]
    q_off = (2 * DIFF_QK_WIDTH + DIFF_V_WIDTH) // LANES
    k_off = q_off + DIL_WIDTH // LANES
    v_off = k_off + DIL_WIDTH // LANES
    n_pairs = DIL_WIDTH // LANES
    return pl.pallas_call(
        _dil_attn_kernel,
        grid=(batch, n_pairs),
        in_specs=[
            pl.BlockSpec(memory_space=pltpu.SMEM),
            pl.BlockSpec((None, SEQ, LANES), lambda b, h: (b, 0, q_off + h)),
            pl.BlockSpec((None, SEQ, LANES), lambda b, h: (b, 0, k_off + h)),
            pl.BlockSpec((None, SEQ, LANES), lambda b, h: (b, 0, v_off + h)),
        ],
        out_specs=pl.BlockSpec((None, SEQ, LANES), lambda b, h: (b, 0, h)),
        out_shape=jax.ShapeDtypeStruct((batch, SEQ, DIL_WIDTH), BF16),
        scratch_shapes=[
            pltpu.VMEM((SEQ, LANES), F32),
            pltpu.VMEM((SEQ, LANES), F32),
            pltpu.VMEM((SEQ, LANES), F32),
            pltpu.VMEM((2 * N_PATTERNS, Q_BLOCK, 2 * Q_BLOCK), F32),
            pltpu.VMEM((N_PATTERNS, SEQ, LANES), F32),
            pltpu.VMEM((N_PATTERNS, SEQ, LANES), F32),
        ],
        compiler_params=pltpu.CompilerParams(
            dimension_semantics=("arbitrary", "arbitrary"), vmem_limit_bytes=VMEM_LIMIT_ATTN),
        name="dilated_attention",
    )(slopes, proj3, proj3, proj3)


def _alibi_slopes():
    all_s = 2.0 ** (-8.0 * np.arange(1, N_ALIBI_HEADS + 1, dtype=np.float32) / N_ALIBI_HEADS)
    diff_idx = np.arange(0, N_ALIBI_HEADS, N_ALIBI_HEADS // N_DIFF_HEADS)
    dil_idx = np.setdiff1d(np.arange(N_ALIBI_HEADS), diff_idx)
    return jnp.asarray(all_s[diff_idx], F32), jnp.asarray(all_s[dil_idx], F32)


def kernel(x, ffn1_pre_g, ffn1_w_gate, ffn1_w_up, ffn1_w_down, ffn1_post_g, mix_pre_g, w_in, lambda_q1, lambda_k1, lambda_q2, lambda_k2, diff_subln_g, w_out, mix_post_g, ffn2_pre_g, ffn2_w_gate, ffn2_w_up, ffn2_w_down, ffn2_post_g):
    batch, seq, d_model = x.shape
    assert (seq, d_model) == (SEQ, D_MODEL) and ffn1_pre_g.shape[0] == 1
    slopes_d, slopes_l = _alibi_slopes()

    scale = HEAD_DIM ** -0.5
    col_scale = np.ones((IN_PROJ_WIDTH,), np.float32)
    col_scale[:DIFF_QK_WIDTH] = scale
    q_l0 = 2 * DIFF_QK_WIDTH + DIFF_V_WIDTH
    col_scale[q_l0:q_l0 + DIL_WIDTH] = scale

    l = 0
    row = lambda g: g[l].reshape(1, -1)
    x2d = x.reshape(batch * seq, d_model)
    x1, proj, vt = _ffn1_inproj(
        x2d, row(ffn1_pre_g), ffn1_w_gate[l].astype(BF16), ffn1_w_up[l].astype(BF16),
        ffn1_w_down[l].astype(BF16), row(ffn1_post_g), row(mix_pre_g),
        (w_in[l] * col_scale).astype(BF16), batch)
    proj3 = proj.reshape(batch, seq, IN_PROJ_WIDTH)
    o_d = _diff_attention(slopes_d, proj3, vt, row(lambda_q1), row(lambda_k1), row(lambda_q2),
                          row(lambda_k2), diff_subln_g[l].reshape(DIFF_V_DIM, 1))
    o_l = _dilated_attention(slopes_l, proj3)
    w_o = w_out[l].astype(BF16)
    out = _outproj_ffn2(
        o_d.reshape(batch * seq, DIFF_V_WIDTH), o_l.reshape(batch * seq, DIL_WIDTH), x1,
        w_o[:DIFF_V_WIDTH], w_o[DIFF_V_WIDTH:], row(mix_post_g), row(ffn2_pre_g),
        ffn2_w_gate[l].astype(BF16), ffn2_w_up[l].astype(BF16), ffn2_w_down[l].astype(BF16),
        row(ffn2_post_g))
    return out.reshape(batch, seq, d_model)
```

```python
import functools
import math

import numpy as np
import jax
import jax.numpy as jnp
from jax import lax
from jax.experimental import pallas as pl
from jax.experimental.pallas import tpu as pltpu

F32 = jnp.float32
BF16 = jnp.bfloat16

D_MODEL = 1024
SEQ = 2048
HEAD_DIM = 64
N_DIFF_HEADS = 4
DIFF_V_DIM = 2 * HEAD_DIM
N_DIL_HEADS = 8
DIL_PATTERNS = ((128, 1), (512, 4), (2048, 16))
Q_BLOCK = 128
D_FF = 2816
RMS_EPS = 1e-6

DIFF_QK_WIDTH = N_DIFF_HEADS * 2 * HEAD_DIM
DIFF_V_WIDTH = N_DIFF_HEADS * DIFF_V_DIM
DIL_WIDTH = N_DIL_HEADS * HEAD_DIM
IN_PROJ_WIDTH = 2 * DIFF_QK_WIDTH + DIFF_V_WIDTH + 3 * DIL_WIDTH
N_ALIBI_HEADS = N_DIFF_HEADS + N_DIL_HEADS
LAM_INIT = 0.8 - 0.6 * math.exp(-0.3 * 0)

LANES = 128
NEG_BIG = -1e30
VMEM_LIMIT_FFN = 56 * 1024 * 1024
VMEM_LIMIT_ATTN = 40 * 1024 * 1024

TM = 256
TQ = 256
TK = 256


def _rms(x, g):
    ms = jnp.mean(x * x, axis=-1, keepdims=True)
    return x * lax.rsqrt(ms + RMS_EPS) * g


def _swiglu_half_step(x, pre_g, wg_ref, wu_ref, wd_ref, post_g):
    xb = _rms(x, pre_g).astype(BF16)
    g = jnp.dot(xb, wg_ref[...], preferred_element_type=F32)
    u = jnp.dot(xb, wu_ref[...], preferred_element_type=F32)
    h = (g * (1.0 / (1.0 + jnp.exp(-g))) * u).astype(BF16)
    y = jnp.dot(h, wd_ref[...], preferred_element_type=F32)
    return x + 0.5 * _rms(y, post_g)


def _ffn1_inproj_kernel(x_ref, pre_g, wg, wu, wd, post_g, mix_g, w_in, x1_ref, proj_ref, vt_ref):
    x1 = _swiglu_half_step(x_ref[...], pre_g[...], wg, wu, wd, post_g[...])
    x1_ref[...] = x1
    hb = _rms(x1, mix_g[...]).astype(BF16)
    proj = jnp.dot(hb, w_in[...], preferred_element_type=F32)
    proj_ref[...] = proj.astype(BF16)
    v_d = proj[:, 2 * DIFF_QK_WIDTH:2 * DIFF_QK_WIDTH + DIFF_V_WIDTH]
    vt_ref[...] = v_d.T.astype(BF16)


def _const_spec(shape):
    nd = len(shape)
    return pl.BlockSpec(shape, lambda *_: (0,) * nd, pipeline_mode=pl.Buffered(1))


def _ffn1_inproj(x2d, pre_g, wg, wu, wd, post_g, mix_g, w_in, batch):
    t = x2d.shape[0]
    tiles_per_seq = SEQ // TM
    row = lambda i: (i, 0)
    return pl.pallas_call(
        _ffn1_inproj_kernel,
        grid=(t // TM,),
        in_specs=[
            pl.BlockSpec((TM, D_MODEL), row),
            _const_spec((1, D_MODEL)),
            _const_spec((D_MODEL, D_FF)),
            _const_spec((D_MODEL, D_FF)),
            _const_spec((D_FF, D_MODEL)),
            _const_spec((1, D_MODEL)),
            _const_spec((1, D_MODEL)),
            _const_spec((D_MODEL, IN_PROJ_WIDTH)),
        ],
        out_specs=[
            pl.BlockSpec((TM, D_MODEL), row),
            pl.BlockSpec((TM, IN_PROJ_WIDTH), row),
            pl.BlockSpec((None, DIFF_V_WIDTH, TM), lambda i: (i // tiles_per_seq, 0, i % tiles_per_seq)),
        ],
        out_shape=[
            jax.ShapeDtypeStruct((t, D_MODEL), F32),
            jax.ShapeDtypeStruct((t, IN_PROJ_WIDTH), BF16),
            jax.ShapeDtypeStruct((batch, DIFF_V_WIDTH, SEQ), BF16),
        ],
        compiler_params=pltpu.CompilerParams(
            dimension_semantics=("arbitrary",), vmem_limit_bytes=VMEM_LIMIT_FFN),
        name="ffn1_inproj",
    )(x2d, pre_g, wg, wu, wd, post_g, mix_g, w_in)


def _outproj_ffn2_kernel(od_ref, ol_ref, x1_ref, wo_d, wo_l, mix_post_g, pre_g, wg, wu, wd, post_g, out_ref):
    h = (jnp.dot(od_ref[...], wo_d[...], preferred_element_type=F32)
         + jnp.dot(ol_ref[...], wo_l[...], preferred_element_type=F32))
    x2 = x1_ref[...] + _rms(h, mix_post_g[...])
    out_ref[...] = _swiglu_half_step(x2, pre_g[...], wg, wu, wd, post_g[...])


def _outproj_ffn2(od, ol, x1, wo_d, wo_l, mix_post_g, pre_g, wg, wu, wd, post_g):
    t = x1.shape[0]
    row = lambda i: (i, 0)
    return pl.pallas_call(
        _outproj_ffn2_kernel,
        grid=(t // TM,),
        in_specs=[
            pl.BlockSpec((TM, DIFF_V_WIDTH), row),
            pl.BlockSpec((TM, DIL_WIDTH), row),
            pl.BlockSpec((TM, D_MODEL), row),
            _const_spec((DIFF_V_WIDTH, D_MODEL)),
            _const_spec((DIL_WIDTH, D_MODEL)),
            _const_spec((1, D_MODEL)),
            _const_spec((1, D_MODEL)),
            _const_spec((D_MODEL, D_FF)),
            _const_spec((D_MODEL, D_FF)),
            _const_spec((D_FF, D_MODEL)),
            _const_spec((1, D_MODEL)),
        ],
        out_specs=pl.BlockSpec((TM, D_MODEL), row),
        out_shape=jax.ShapeDtypeStruct((t, D_MODEL), F32),
        compiler_params=pltpu.CompilerParams(
            dimension_semantics=("arbitrary",), vmem_limit_bytes=VMEM_LIMIT_FFN),
        name="outproj_ffn2",
    )(od, ol, x1, wo_d, wo_l, mix_post_g, pre_g, wg, wu, wd, post_g)


_NT = (((1,), (1,)), ((), ()))
N_MAPS = 2 * N_DIFF_HEADS


def _diff_attn_kernel(q_ref, k_ref, vt_ref, lq1, lk1, lq2, lk2, g_ref, o_ref, bias, acc, *, slopes):
    lam = (jnp.exp(jnp.sum(lq1[...] * lk1[...], axis=-1, keepdims=True))
           - jnp.exp(jnp.sum(lq2[...] * lk2[...], axis=-1, keepdims=True)) + LAM_INIT)
    lane = lax.broadcasted_iota(jnp.int32, (1, LANES), 1)
    first_map = lane < HEAD_DIM
    zero = jnp.zeros((), BF16)
    d0 = (lax.broadcasted_iota(jnp.int32, (TK, TQ), 1)
          - lax.broadcasted_iota(jnp.int32, (TK, TQ), 0)).astype(F32)
    for h in range(N_DIFF_HEADS):
        bias[h] = -slopes[h] * d0
        bias[N_DIFF_HEADS + h] = jnp.where(d0 >= 0, -slopes[h] * d0, NEG_BIG)

    def q_tile(qi, carry):
        qs = pl.multiple_of(qi * TQ, TQ)
        qm = []
        for h in range(N_DIFF_HEADS):
            q = q_ref[pl.ds(qs, TQ), h * LANES:(h + 1) * LANES]
            qm.append(jnp.where(first_map, q, zero))
            qm.append(jnp.where(first_map, zero, q))

        def scores(j, c):
            h = c // 2
            ks = pl.multiple_of(j * TK, TK)
            kt = k_ref[pl.ds(ks, TK), h * LANES:(h + 1) * LANES]
            return lax.dot_general(kt, qm[c], _NT, preferred_element_type=F32)

        def values_t(j, c):
            h = c // 2
            ks = pl.multiple_of(j * TK, TK)
            return vt_ref[h * DIFF_V_DIM:(h + 1) * DIFF_V_DIM, pl.ds(ks, TK)]

        def accumulate(j, c, a, p):
            pv = jnp.dot(values_t(j, c), p, preferred_element_type=F32)
            acc[c] = a * acc[c] + pv

        ms, ls, ps = [], [], []
        for c in range(N_MAPS):
            s = scores(qi, c) + bias[N_DIFF_HEADS + c // 2]
            m = jnp.max(s, axis=0, keepdims=True)
            p = jnp.exp(s - m)
            ms.append(m)
            ls.append(jnp.sum(p, axis=0, keepdims=True))
            ps.append(p.astype(BF16))

        def k_tile(t, carry):
            ms, ls, alphas, ps = carry
            j = qi - t
            dist = (t * TK).astype(F32)
            new_m, new_l, new_a, new_p = [], [], [], []
            for c in range(N_MAPS):
                accumulate(j + 1, c, alphas[c], ps[c])
                coff = slopes[c // 2] * dist
                s = scores(j, c) + bias[c // 2]
                n = jnp.maximum(ms[c], jnp.max(s, axis=0, keepdims=True) - coff)
                a = jnp.exp(ms[c] - n)
                p = jnp.exp(s - (n + coff))
                new_m.append(n)
                new_l.append(a * ls[c] + jnp.sum(p, axis=0, keepdims=True))
                new_a.append(a)
                new_p.append(p.astype(BF16))
            return tuple(new_m), tuple(new_l), tuple(new_a), tuple(new_p)

        for c in range(N_MAPS):
            acc[c] = jnp.zeros((DIFF_V_DIM, TQ), F32)
        ones = tuple(jnp.ones((1, TQ), F32) for _ in range(N_MAPS))
        ms, ls, alphas, ps = lax.fori_loop(1, qi + 1, k_tile, (tuple(ms), tuple(ls), ones, tuple(ps)))
        for c in range(N_MAPS):
            accumulate(0, c, alphas[c], ps[c])
        for h in range(N_DIFF_HEADS):
            l1, l2 = ls[2 * h], ls[2 * h + 1]
            o = acc[2 * h] * (1.0 / l1) - lam * (acc[2 * h + 1] * (1.0 / l2))
            ms = jnp.mean(o * o, axis=0, keepdims=True)
            y = o * lax.rsqrt(ms + RMS_EPS) * g_ref[...] * (1.0 - LAM_INIT)
            o_ref[pl.ds(qs, TQ), h * LANES:(h + 1) * LANES] = y.T.astype(BF16)
        return carry

    lax.fori_loop(0, SEQ // TQ, q_tile, 0)


def _diff_attention(slopes, proj3, vt, lq1, lk1, lq2, lk2, g_col):
    batch = proj3.shape[0]
    vec = pl.BlockSpec((1, HEAD_DIM), lambda b: (0, 0))
    return pl.pallas_call(
        functools.partial(_diff_attn_kernel, slopes=slopes),
        grid=(batch,),
        in_specs=[
            pl.BlockSpec((None, SEQ, DIFF_QK_WIDTH), lambda b: (b, 0, 0)),
            pl.BlockSpec((None, SEQ, DIFF_QK_WIDTH), lambda b: (b, 0, 1)),
            pl.BlockSpec((None, DIFF_V_WIDTH, SEQ), lambda b: (b, 0, 0)),
            vec, vec, vec, vec,
            pl.BlockSpec((DIFF_V_DIM, 1), lambda b: (0, 0)),
        ],
        out_specs=pl.BlockSpec((None, SEQ, DIFF_V_WIDTH), lambda b: (b, 0, 0)),
        out_shape=jax.ShapeDtypeStruct((batch, SEQ, DIFF_V_WIDTH), BF16),
        scratch_shapes=[
            pltpu.VMEM((2 * N_DIFF_HEADS, TK, TQ), F32),
            pltpu.VMEM((N_MAPS, DIFF_V_DIM, TQ), F32),
        ],
        compiler_params=pltpu.CompilerParams(
            dimension_semantics=("arbitrary",), vmem_limit_bytes=VMEM_LIMIT_ATTN),
        name="diff_attention",
    )(proj3, proj3, vt, lq1, lk1, lq2, lk2, g_col)


N_PATTERNS = len(DIL_PATTERNS)
N_WIN = Q_BLOCK
N_BLOCKS = SEQ // Q_BLOCK
KPAD = Q_BLOCK
UNROLL = 4


def _dil_attn_kernel(slopes_ref, q_ref, k_ref, v_ref, o_ref,
                     qf, kf, vf, qa, qb, kt, vv, bias, o_acc, lse_acc):
    hp = pl.program_id(1)
    lane = lax.broadcasted_iota(jnp.int32, (1, LANES), 1)
    first_head = lane < HEAD_DIM

    qf[...] = q_ref[...].astype(F32)
    kf[...] = k_ref[...].astype(F32)
    vf[...] = v_ref[...].astype(F32)
    kt[0, :, 0:KPAD] = jnp.zeros((LANES, KPAD), BF16)
    vv[0, 0:KPAD, :] = jnp.zeros((KPAD, LANES), BF16)

    col = lax.broadcasted_iota(jnp.int32, (Q_BLOCK, 2 * Q_BLOCK), 1)
    dsub = Q_BLOCK + lax.broadcasted_iota(jnp.int32, (Q_BLOCK, 2 * Q_BLOCK), 0) - col
    valid = (dsub >= 0) & (dsub <= N_WIN)
    dsub_f = dsub.astype(F32)
    for p, (_, dil) in enumerate(DIL_PATTERNS):
        for hh in range(2):
            slope = slopes_ref[2 * hp + hh]
            b = jnp.where(valid, (-slope * float(dil)) * dsub_f, NEG_BIG)
            bias[2 * p + hh] = b
            if p == 0:
                bias[2 * N_PATTERNS + hh] = jnp.where(col >= Q_BLOCK, b, NEG_BIG)

    def strided_rows(ref, start, dil):
        if dil == 1:
            return ref[pl.ds(start, Q_BLOCK), :]
        return ref[pl.ds(start, Q_BLOCK, stride=dil), :]

    def prepare(c, carry):
        dst = pl.multiple_of(c * Q_BLOCK, Q_BLOCK)
        for p, (_, dil) in enumerate(DIL_PATTERNS):
            blocks_per_class = N_BLOCKS // dil
            r = c >> (blocks_per_class.bit_length() - 1)
            n = c & (blocks_per_class - 1)
            start = n * (Q_BLOCK * dil) + r
            if dil == 1:
                start = pl.multiple_of(start, Q_BLOCK)
            q = strided_rows(qf, start, dil)
            qa[p, pl.ds(dst, Q_BLOCK), :] = jnp.where(first_head, q, 0.0).astype(BF16)
            qb[p, pl.ds(dst, Q_BLOCK), :] = jnp.where(first_head, 0.0, q).astype(BF16)
            kt[p, :, pl.ds(KPAD + dst, Q_BLOCK)] = strided_rows(kf, start, dil).T.astype(BF16)
            vv[p, pl.ds(KPAD + dst, Q_BLOCK), :] = strided_rows(vf, start, dil).astype(BF16)
        return carry

    lax.fori_loop(0, N_BLOCKS, prepare, 0)

    def attend(p, dil, q_row, with_prev, nat_start, bias_idx):
        n_keys = 2 * Q_BLOCK if with_prev else Q_BLOCK
        key_row = q_row + (KPAD - Q_BLOCK if with_prev else KPAD)
        if not isinstance(q_row, int):
            q_row = pl.multiple_of(q_row, Q_BLOCK)
            key_row = pl.multiple_of(key_row, Q_BLOCK)
        k_t = kt[p, :, pl.ds(key_row, n_keys)]
        v = vv[p, pl.ds(key_row, n_keys), :]
        outs = []
        for hh, q_src in enumerate((qa, qb)):
            s = jnp.dot(q_src[p, pl.ds(q_row, Q_BLOCK), :], k_t, preferred_element_type=F32)
            if with_prev:
                s = s + bias[bias_idx[hh]]
            else:
                s = s + bias[bias_idx[hh], :, Q_BLOCK:]
            m = jnp.max(s, axis=1, keepdims=True)
            e = jnp.exp(s - m)
            l = jnp.sum(e, axis=1, keepdims=True)
            oh = jnp.dot(e.astype(BF16), v, preferred_element_type=F32) * (1.0 / l)
            outs.append((oh, m + jnp.log(l)))
        o = jnp.where(first_head, outs[0][0], outs[1][0])
        lse = jnp.where(first_head, outs[0][1], outs[1][1])
        if dil == 1:
            nat = pl.ds(pl.multiple_of(nat_start, Q_BLOCK), Q_BLOCK)
        else:
            nat = pl.ds(nat_start, Q_BLOCK, stride=dil)
        o_acc[p, nat, :] = o
        lse_acc[p, nat, :] = lse

    def dense_blocks(i, carry):
        for u in range(UNROLL):
            n = i * UNROLL + u
            if u == 0:
                idx = tuple(jnp.where(i == 0, 2 * N_PATTERNS + hh, hh) for hh in range(2))
            else:
                idx = (0, 1)
            attend(0, 1, n * Q_BLOCK, True, n * Q_BLOCK, idx)
        return carry

    lax.fori_loop(0, N_BLOCKS // UNROLL, dense_blocks, 0)

    dil4 = DIL_PATTERNS[1][1]
    run = SEQ // dil4
    for r in range(dil4):
        attend(1, dil4, r * run, False, r, (2, 3))

    def dil4_blocks(n, carry):
        for r in range(dil4):
            attend(1, dil4, r * run + n * Q_BLOCK, True, n * (Q_BLOCK * dil4) + r, (2, 3))
        return carry

    lax.fori_loop(1, run // Q_BLOCK, dil4_blocks, 0)

    dil16 = DIL_PATTERNS[2][1]

    def dil16_blocks(i, carry):
        for u in range(UNROLL):
            r = i * UNROLL + u
            attend(2, dil16, r * Q_BLOCK, False, r, (4, 5))
        return carry

    lax.fori_loop(0, dil16 // UNROLL, dil16_blocks, 0)

    rows_per_step = 256

    def blend(i, carry):
        sl = pl.ds(pl.multiple_of(i * rows_per_step, rows_per_step), rows_per_step)
        lses = [lse_acc[p, sl, :] for p in range(N_PATTERNS)]
        mx = functools.reduce(jnp.maximum, lses)
        ws = [jnp.exp(l - mx) for l in lses]
        den = functools.reduce(lambda a, b: a + b, ws)
        num = functools.reduce(lambda a, b: a + b, [ws[p] * o_acc[p, sl, :] for p in range(N_PATTERNS)])
        o_ref[sl, :] = (num * (1.0 / den)).astype(BF16)
        return carry

    lax.fori_loop(0, SEQ // rows_per_step, blend, 0)


def _dilated_attention(slopes, proj3):
    batch = proj3.shape[0]
    q_off = (2 * DIFF_QK_WIDTH + DIFF_V_WIDTH) // LANES
    k_off = q_off + DIL_WIDTH // LANES
    v_off = k_off + DIL_WIDTH // LANES
    n_pairs = DIL_WIDTH // LANES
    return pl.pallas_call(
        _dil_attn_kernel,
        grid=(batch, n_pairs),
        in_specs=[
            pl.BlockSpec(memory_space=pltpu.SMEM),
            pl.BlockSpec((None, SEQ, LANES), lambda b, h: (b, 0, q_off + h)),
            pl.BlockSpec((None, SEQ, LANES), lambda b, h: (b, 0, k_off + h)),
            pl.BlockSpec((None, SEQ, LANES), lambda b, h: (b, 0, v_off + h)),
        ],
        out_specs=pl.BlockSpec((None, SEQ, LANES), lambda b, h: (b, 0, h)),
        out_shape=jax.ShapeDtypeStruct((batch, SEQ, DIL_WIDTH), BF16),
        scratch_shapes=[
            pltpu.VMEM((SEQ, LANES), F32),
            pltpu.VMEM((SEQ, LANES), F32),
            pltpu.VMEM((SEQ, LANES), F32),
            pltpu.VMEM((N_PATTERNS, SEQ, LANES), BF16),
            pltpu.VMEM((N_PATTERNS, SEQ, LANES), BF16),
            pltpu.VMEM((N_PATTERNS, LANES, KPAD + SEQ), BF16),
            pltpu.VMEM((N_PATTERNS, KPAD + SEQ, LANES), BF16),
            pltpu.VMEM((2 * N_PATTERNS + 2, Q_BLOCK, 2 * Q_BLOCK), F32),
            pltpu.VMEM((N_PATTERNS, SEQ, LANES), F32),
            pltpu.VMEM((N_PATTERNS, SEQ, LANES), F32),
        ],
        compiler_params=pltpu.CompilerParams(
            dimension_semantics=("arbitrary", "arbitrary"), vmem_limit_bytes=VMEM_LIMIT_ATTN),
        name="dilated_attention",
    )(slopes, proj3, proj3, proj3)


def _alibi_slopes():
    all_s = 2.0 ** (-8.0 * np.arange(1, N_ALIBI_HEADS + 1, dtype=np.float32) / N_ALIBI_HEADS)
    diff_idx = np.arange(0, N_ALIBI_HEADS, N_ALIBI_HEADS // N_DIFF_HEADS)
    dil_idx = np.setdiff1d(np.arange(N_ALIBI_HEADS), diff_idx)
    return tuple(float(s) for s in all_s[diff_idx]), jnp.asarray(all_s[dil_idx], F32)


def kernel(x, ffn1_pre_g, ffn1_w_gate, ffn1_w_up, ffn1_w_down, ffn1_post_g, mix_pre_g, w_in, lambda_q1, lambda_k1, lambda_q2, lambda_k2, diff_subln_g, w_out, mix_post_g, ffn2_pre_g, ffn2_w_gate, ffn2_w_up, ffn2_w_down, ffn2_post_g):
    batch, seq, d_model = x.shape
    assert (seq, d_model) == (SEQ, D_MODEL) and ffn1_pre_g.shape[0] == 1
    slopes_d, slopes_l = _alibi_slopes()

    scale = HEAD_DIM ** -0.5
    col_scale = np.ones((IN_PROJ_WIDTH,), np.float32)
    col_scale[:DIFF_QK_WIDTH] = scale
    q_l0 = 2 * DIFF_QK_WIDTH + DIFF_V_WIDTH
    col_scale[q_l0:q_l0 + DIL_WIDTH] = scale

    l = 0
    row = lambda g: g[l].reshape(1, -1)
    x2d = x.reshape(batch * seq, d_model)
    x1, proj, vt = _ffn1_inproj(
        x2d, row(ffn1_pre_g), ffn1_w_gate[l].astype(BF16), ffn1_w_up[l].astype(BF16),
        ffn1_w_down[l].astype(BF16), row(ffn1_post_g), row(mix_pre_g),
        (w_in[l] * col_scale).astype(BF16), batch)
    proj3 = proj.reshape(batch, seq, IN_PROJ_WIDTH)
    o_d = _diff_attention(slopes_d, proj3, vt, row(lambda_q1), row(lambda_k1), row(lambda_q2),
                          row(lambda_k2), diff_subln_g[l].reshape(DIFF_V_DIM, 1))
    o_l = _dilated_attention(slopes_l, proj3)
    w_o = w_out[l].astype(BF16)
    out = _outproj_ffn2(
        o_d.reshape(batch * seq, DIFF_V_WIDTH), o_l.reshape(batch * seq, DIL_WIDTH), x1,
        w_o[:DIFF_V_WIDTH], w_o[DIFF_V_WIDTH:], row(mix_post_g), row(ffn2_pre_g),
        ffn2_w_gate[l].astype(BF16), ffn2_w_up[l].astype(BF16), ffn2_w_down[l].astype(BF16),
        row(ffn2_post_g))
    return out.reshape(batch, seq, d_model)
```

```python
import functools
import math

import numpy as np
import jax
import jax.numpy as jnp
from jax import lax
from jax.experimental import pallas as pl
from jax.experimental.pallas import tpu as pltpu

F32 = jnp.float32
BF16 = jnp.bfloat16

D_MODEL = 1024
SEQ = 2048
HEAD_DIM = 64
N_DIFF_HEADS = 4
DIFF_V_DIM = 2 * HEAD_DIM
N_DIL_HEADS = 8
DIL_PATTERNS = ((128, 1), (512, 4), (2048, 16))
Q_BLOCK = 128
D_FF = 2816
RMS_EPS = 1e-6

DIFF_QK_WIDTH = N_DIFF_HEADS * 2 * HEAD_DIM
DIFF_V_WIDTH = N_DIFF_HEADS * DIFF_V_DIM
DIL_WIDTH = N_DIL_HEADS * HEAD_DIM
IN_PROJ_WIDTH = 2 * DIFF_QK_WIDTH + DIFF_V_WIDTH + 3 * DIL_WIDTH
N_ALIBI_HEADS = N_DIFF_HEADS + N_DIL_HEADS
LAM_INIT = 0.8 - 0.6 * math.exp(-0.3 * 0)

LANES = 128
NEG_BIG = -1e30
VMEM_LIMIT_FFN = 56 * 1024 * 1024
VMEM_LIMIT_ATTN = 40 * 1024 * 1024

TM = 256
TQ = 256
TK = 256


def _rms(x, g):
    ms = jnp.mean(x * x, axis=-1, keepdims=True)
    return x * lax.rsqrt(ms + RMS_EPS) * g


def _swiglu_half_step(x, pre_g, wg_ref, wu_ref, wd_ref, post_g):
    xb = _rms(x, pre_g).astype(BF16)
    g = jnp.dot(xb, wg_ref[...], preferred_element_type=F32)
    u = jnp.dot(xb, wu_ref[...], preferred_element_type=F32)
    h = (g * (1.0 / (1.0 + jnp.exp(-g))) * u).astype(BF16)
    y = jnp.dot(h, wd_ref[...], preferred_element_type=F32)
    return x + 0.5 * _rms(y, post_g)


def _ffn1_inproj_kernel(x_ref, pre_g, wg, wu, wd, post_g, mix_g, w_in, x1_ref, proj_ref, vt_ref):
    x1 = _swiglu_half_step(x_ref[...], pre_g[...], wg, wu, wd, post_g[...])
    x1_ref[...] = x1
    hb = _rms(x1, mix_g[...]).astype(BF16)
    proj = jnp.dot(hb, w_in[...], preferred_element_type=F32)
    proj_ref[...] = proj.astype(BF16)
    v_d = proj[:, 2 * DIFF_QK_WIDTH:2 * DIFF_QK_WIDTH + DIFF_V_WIDTH]
    vt_ref[...] = v_d.T.astype(BF16)


def _const_spec(shape):
    nd = len(shape)
    return pl.BlockSpec(shape, lambda *_: (0,) * nd, pipeline_mode=pl.Buffered(1))


def _ffn1_inproj(x2d, pre_g, wg, wu, wd, post_g, mix_g, w_in, batch):
    t = x2d.shape[0]
    tiles_per_seq = SEQ // TM
    row = lambda i: (i, 0)
    return pl.pallas_call(
        _ffn1_inproj_kernel,
        grid=(t // TM,),
        in_specs=[
            pl.BlockSpec((TM, D_MODEL), row),
            _const_spec((1, D_MODEL)),
            _const_spec((D_MODEL, D_FF)),
            _const_spec((D_MODEL, D_FF)),
            _const_spec((D_FF, D_MODEL)),
            _const_spec((1, D_MODEL)),
            _const_spec((1, D_MODEL)),
            _const_spec((D_MODEL, IN_PROJ_WIDTH)),
        ],
        out_specs=[
            pl.BlockSpec((TM, D_MODEL), row),
            pl.BlockSpec((TM, IN_PROJ_WIDTH), row),
            pl.BlockSpec((None, DIFF_V_WIDTH, TM), lambda i: (i // tiles_per_seq, 0, i % tiles_per_seq)),
        ],
        out_shape=[
            jax.ShapeDtypeStruct((t, D_MODEL), F32),
            jax.ShapeDtypeStruct((t, IN_PROJ_WIDTH), BF16),
            jax.ShapeDtypeStruct((batch, DIFF_V_WIDTH, SEQ), BF16),
        ],
        compiler_params=pltpu.CompilerParams(
            dimension_semantics=("arbitrary",), vmem_limit_bytes=VMEM_LIMIT_FFN),
        name="ffn1_inproj",
    )(x2d, pre_g, wg, wu, wd, post_g, mix_g, w_in)


def _outproj_ffn2_kernel(od_ref, ol_ref, x1_ref, wo_d, wo_l, mix_post_g, pre_g, wg, wu, wd, post_g, out_ref):
    h = (jnp.dot(od_ref[...], wo_d[...], preferred_element_type=F32)
         + jnp.dot(ol_ref[...], wo_l[...], preferred_element_type=F32))
    x2 = x1_ref[...] + _rms(h, mix_post_g[...])
    out_ref[...] = _swiglu_half_step(x2, pre_g[...], wg, wu, wd, post_g[...])


def _outproj_ffn2(od, ol, x1, wo_d, wo_l, mix_post_g, pre_g, wg, wu, wd, post_g):
    t = x1.shape[0]
    row = lambda i: (i, 0)
    return pl.pallas_call(
        _outproj_ffn2_kernel,
        grid=(t // TM,),
        in_specs=[
            pl.BlockSpec((TM, DIFF_V_WIDTH), row),
            pl.BlockSpec((TM, DIL_WIDTH), row),
            pl.BlockSpec((TM, D_MODEL), row),
            _const_spec((DIFF_V_WIDTH, D_MODEL)),
            _const_spec((DIL_WIDTH, D_MODEL)),
            _const_spec((1, D_MODEL)),
            _const_spec((1, D_MODEL)),
            _const_spec((D_MODEL, D_FF)),
            _const_spec((D_MODEL, D_FF)),
            _const_spec((D_FF, D_MODEL)),
            _const_spec((1, D_MODEL)),
        ],
        out_specs=pl.BlockSpec((TM, D_MODEL), row),
        out_shape=jax.ShapeDtypeStruct((t, D_MODEL), F32),
        compiler_params=pltpu.CompilerParams(
            dimension_semantics=("arbitrary",), vmem_limit_bytes=VMEM_LIMIT_FFN),
        name="outproj_ffn2",
    )(od, ol, x1, wo_d, wo_l, mix_post_g, pre_g, wg, wu, wd, post_g)


_NT = (((1,), (1,)), ((), ()))
N_MAPS = 2 * N_DIFF_HEADS
ONES_ROWS = 16
SLOPE_TERMS = 3
LOG2E = math.log2(math.e)


def _bf16_terms(x, n_terms):
    terms, rest = [], np.float32(x)
    for _ in range(n_terms):
        t = np.float32(np.asarray(rest, dtype=BF16))
        terms.append(float(t))
        rest = np.float32(rest - t)
    return terms


def _diff_attn_kernel(q_ref, k_ref, vt_ref, lq1, lk1, lq2, lk2, g_ref, o_ref, kp, vta, mask, acc, *, slopes):
    lam = (jnp.exp(jnp.sum(lq1[...] * lk1[...], axis=-1, keepdims=True))
           - jnp.exp(jnp.sum(lq2[...] * lk2[...], axis=-1, keepdims=True)) + LAM_INIT)
    lane = lax.broadcasted_iota(jnp.int32, (1, LANES), 1)
    first_map = lane < HEAD_DIM

    def in_lanes(base, values):
        row = jnp.zeros((1, LANES), F32)
        for i, v in enumerate(values):
            row = jnp.where(lane == base + i, v, row)
        return row.astype(BF16)

    slope_rows = [(in_lanes(HEAD_DIM, _bf16_terms(s * LOG2E, SLOPE_TERMS)),
                   in_lanes(0, _bf16_terms(s * LOG2E, SLOPE_TERMS))) for s in slopes]
    key_idx = lax.broadcasted_iota(jnp.int32, (TK, LANES), 0).astype(F32)
    idx_first = jnp.where((lane >= HEAD_DIM) & (lane < HEAD_DIM + SLOPE_TERMS), key_idx, 0.0).astype(BF16)
    idx_second = jnp.where(lane < SLOPE_TERMS, key_idx, 0.0).astype(BF16)

    def augment_keys(i, carry):
        rows = pl.ds(pl.multiple_of(i * TK, TK), TK)
        for h in range(N_DIFF_HEADS):
            k = k_ref[rows, h * LANES:(h + 1) * LANES]
            kp[2 * h, rows, :] = jnp.where(first_map, k, idx_first)
            kp[2 * h + 1, rows, :] = jnp.where(first_map, idx_second, k)
        return carry

    lax.fori_loop(0, SEQ // TK, augment_keys, 0)
    for h in range(N_DIFF_HEADS):
        vta[h, 0:DIFF_V_DIM, :] = vt_ref[h * DIFF_V_DIM:(h + 1) * DIFF_V_DIM, :]
        vta[h, DIFF_V_DIM:, :] = jnp.ones((ONES_ROWS, SEQ), BF16)
    causal = (lax.broadcasted_iota(jnp.int32, (TK, TQ), 1)
              >= lax.broadcasted_iota(jnp.int32, (TK, TQ), 0))
    mask[...] = jnp.where(causal, 0.0, NEG_BIG)

    def q_tile(qi, carry):
        qs = pl.multiple_of(qi * TQ, TQ)
        qm = []
        for h in range(N_DIFF_HEADS):
            q = q_ref[pl.ds(qs, TQ), h * LANES:(h + 1) * LANES]
            qm.append(jnp.where(first_map, q, slope_rows[h][0]))
            qm.append(jnp.where(first_map, slope_rows[h][1], q))

        def scores(j, c):
            ks = pl.multiple_of(j * TK, TK)
            return lax.dot_general(kp[c, pl.ds(ks, TK), :], qm[c], _NT, preferred_element_type=F32)

        def accumulate(j, c, a, p):
            ks = pl.multiple_of(j * TK, TK)
            pv = jnp.dot(vta[c // 2, :, pl.ds(ks, TK)], p, preferred_element_type=F32)
            acc[c] = a * acc[c] + pv

        ms, ps = [], []
        for c in range(N_MAPS):
            s = scores(qi, c) + mask[...]
            m = jnp.max(s, axis=0, keepdims=True)
            ms.append(m)
            ps.append(jnp.exp2(s - m).astype(BF16))

        def k_tile(t, carry):
            ms, alphas, ps = carry
            j = qi - t
            dist = jnp.asarray(t * TK, F32)
            new_m, new_a, new_p = [], [], []
            for c in range(N_MAPS):
                accumulate(j + 1, c, alphas[c], ps[c])
                coff = (slopes[c // 2] * LOG2E) * dist
                s = scores(j, c)
                n = jnp.maximum(ms[c], jnp.max(s, axis=0, keepdims=True) - coff)
                new_m.append(n)
                new_a.append(jnp.exp2(ms[c] - n))
                new_p.append(jnp.exp2(s - (n + coff)).astype(BF16))
            return tuple(new_m), tuple(new_a), tuple(new_p)

        for c in range(N_MAPS):
            acc[c] = jnp.zeros((DIFF_V_DIM + ONES_ROWS, TQ), F32)
        ones = tuple(jnp.ones((1, TQ), F32) for _ in range(N_MAPS))
        ms, alphas, ps = lax.fori_loop(1, qi + 1, k_tile, (tuple(ms), ones, tuple(ps)))
        for c in range(N_MAPS):
            accumulate(0, c, alphas[c], ps[c])
        for h in range(N_DIFF_HEADS):
            a1, a2 = acc[2 * h], acc[2 * h + 1]
            l1, l2 = a1[DIFF_V_DIM:DIFF_V_DIM + 1, :], a2[DIFF_V_DIM:DIFF_V_DIM + 1, :]
            o = a1[:DIFF_V_DIM, :] * (1.0 / l1) - lam * (a2[:DIFF_V_DIM, :] * (1.0 / l2))
            ms = jnp.mean(o * o, axis=0, keepdims=True)
            y = o * lax.rsqrt(ms + RMS_EPS) * g_ref[...] * (1.0 - LAM_INIT)
            o_ref[pl.ds(qs, TQ), h * LANES:(h + 1) * LANES] = y.T.astype(BF16)
        return carry

    lax.fori_loop(0, SEQ // TQ, q_tile, 0)


def _diff_attention(slopes, proj3, vt, lq1, lk1, lq2, lk2, g_col):
    batch = proj3.shape[0]
    vec = pl.BlockSpec((1, HEAD_DIM), lambda b: (0, 0))
    return pl.pallas_call(
        functools.partial(_diff_attn_kernel, slopes=slopes),
        grid=(batch,),
        in_specs=[
            pl.BlockSpec((None, SEQ, DIFF_QK_WIDTH), lambda b: (b, 0, 0)),
            pl.BlockSpec((None, SEQ, DIFF_QK_WIDTH), lambda b: (b, 0, 1)),
            pl.BlockSpec((None, DIFF_V_WIDTH, SEQ), lambda b: (b, 0, 0)),
            vec, vec, vec, vec,
            pl.BlockSpec((DIFF_V_DIM, 1), lambda b: (0, 0)),
        ],
        out_specs=pl.BlockSpec((None, SEQ, DIFF_V_WIDTH), lambda b: (b, 0, 0)),
        out_shape=jax.ShapeDtypeStruct((batch, SEQ, DIFF_V_WIDTH), BF16),
        scratch_shapes=[
            pltpu.VMEM((N_MAPS, SEQ, LANES), BF16),
            pltpu.VMEM((N_DIFF_HEADS, DIFF_V_DIM + ONES_ROWS, SEQ), BF16),
            pltpu.VMEM((TK, TQ), F32),
            pltpu.VMEM((N_MAPS, DIFF_V_DIM + ONES_ROWS, TQ), F32),
        ],
        compiler_params=pltpu.CompilerParams(
            dimension_semantics=("arbitrary",), vmem_limit_bytes=VMEM_LIMIT_ATTN),
        name="diff_attention",
    )(proj3, proj3, vt, lq1, lk1, lq2, lk2, g_col)


N_PATTERNS = len(DIL_PATTERNS)
N_WIN = Q_BLOCK
N_BLOCKS = SEQ // Q_BLOCK
KPAD = Q_BLOCK
UNROLL = 4


def _dil_attn_kernel(slopes_ref, q_ref, k_ref, v_ref, o_ref,
                     qf, kf, vf, qa, qb, kt, vv, bias, o_acc, lse_acc):
    hp = pl.program_id(1)
    lane = lax.broadcasted_iota(jnp.int32, (1, LANES), 1)
    first_head = lane < HEAD_DIM

    qf[...] = q_ref[...].astype(F32)
    kf[...] = k_ref[...].astype(F32)
    vf[...] = v_ref[...].astype(F32)
    kt[0, :, 0:KPAD] = jnp.zeros((LANES, KPAD), BF16)
    vv[0, 0:KPAD, :] = jnp.zeros((KPAD, LANES), BF16)

    col = lax.broadcasted_iota(jnp.int32, (Q_BLOCK, 2 * Q_BLOCK), 1)
    dsub = Q_BLOCK + lax.broadcasted_iota(jnp.int32, (Q_BLOCK, 2 * Q_BLOCK), 0) - col
    valid = (dsub >= 0) & (dsub <= N_WIN)
    dsub_f = dsub.astype(F32)
    for p, (_, dil) in enumerate(DIL_PATTERNS):
        for hh in range(2):
            slope = slopes_ref[2 * hp + hh]
            b = jnp.where(valid, (-slope * (LOG2E * dil)) * dsub_f, NEG_BIG)
            bias[2 * p + hh] = b
            if p == 0:
                bias[2 * N_PATTERNS + hh] = jnp.where(col >= Q_BLOCK, b, NEG_BIG)

    def strided_rows(ref, start, dil):
        if dil == 1:
            return ref[pl.ds(start, Q_BLOCK), :]
        return ref[pl.ds(start, Q_BLOCK, stride=dil), :]

    def prepare(c, carry):
        dst = pl.multiple_of(c * Q_BLOCK, Q_BLOCK)
        for p, (_, dil) in enumerate(DIL_PATTERNS):
            blocks_per_class = N_BLOCKS // dil
            r = c >> (blocks_per_class.bit_length() - 1)
            n = c & (blocks_per_class - 1)
            start = n * (Q_BLOCK * dil) + r
            if dil == 1:
                start = pl.multiple_of(start, Q_BLOCK)
            q = strided_rows(qf, start, dil)
            qa[p, pl.ds(dst, Q_BLOCK), :] = jnp.where(first_head, q, 0.0).astype(BF16)
            qb[p, pl.ds(dst, Q_BLOCK), :] = jnp.where(first_head, 0.0, q).astype(BF16)
            kt[p, :, pl.ds(KPAD + dst, Q_BLOCK)] = strided_rows(kf, start, dil).T.astype(BF16)
            vv[p, pl.ds(KPAD + dst, Q_BLOCK), :] = strided_rows(vf, start, dil).astype(BF16)
        return carry

    lax.fori_loop(0, N_BLOCKS, prepare, 0)

    def attend(p, dil, q_row, with_prev, nat_start, bias_idx):
        n_keys = 2 * Q_BLOCK if with_prev else Q_BLOCK
        key_row = q_row + (KPAD - Q_BLOCK if with_prev else KPAD)
        if not isinstance(q_row, int):
            q_row = pl.multiple_of(q_row, Q_BLOCK)
            key_row = pl.multiple_of(key_row, Q_BLOCK)
        k_t = kt[p, :, pl.ds(key_row, n_keys)]
        v = vv[p, pl.ds(key_row, n_keys), :]
        outs = []
        for hh, q_src in enumerate((qa, qb)):
            s = jnp.dot(q_src[p, pl.ds(q_row, Q_BLOCK), :], k_t, preferred_element_type=F32)
            if with_prev:
                s = s + bias[bias_idx[hh]]
            else:
                s = s + bias[bias_idx[hh], :, Q_BLOCK:]
            m = jnp.max(s, axis=1, keepdims=True)
            e = jnp.exp2(s - m)
            l = jnp.sum(e, axis=1, keepdims=True)
            oh = jnp.dot(e.astype(BF16), v, preferred_element_type=F32) * (1.0 / l)
            outs.append((oh, m + jnp.log2(l)))
        o = jnp.where(first_head, outs[0][0], outs[1][0])
        lse = jnp.where(first_head, outs[0][1], outs[1][1])
        if dil == 1:
            nat = pl.ds(pl.multiple_of(nat_start, Q_BLOCK), Q_BLOCK)
        else:
            nat = pl.ds(nat_start, Q_BLOCK, stride=dil)
        o_acc[p, nat, :] = o
        lse_acc[p, nat, :] = lse

    def dense_blocks(i, carry):
        for u in range(UNROLL):
            n = i * UNROLL + u
            if u == 0:
                idx = tuple(jnp.where(i == 0, 2 * N_PATTERNS + hh, hh) for hh in range(2))
            else:
                idx = (0, 1)
            attend(0, 1, n * Q_BLOCK, True, n * Q_BLOCK, idx)
        return carry

    lax.fori_loop(0, N_BLOCKS // UNROLL, dense_blocks, 0)

    dil4 = DIL_PATTERNS[1][1]
    run = SEQ // dil4
    for r in range(dil4):
        attend(1, dil4, r * run, False, r, (2, 3))

    def dil4_blocks(n, carry):
        for r in range(dil4):
            attend(1, dil4, r * run + n * Q_BLOCK, True, n * (Q_BLOCK * dil4) + r, (2, 3))
        return carry

    lax.fori_loop(1, run // Q_BLOCK, dil4_blocks, 0)

    dil16 = DIL_PATTERNS[2][1]

    def dil16_blocks(i, carry):
        for u in range(UNROLL):
            r = i * UNROLL + u
            attend(2, dil16, r * Q_BLOCK, False, r, (4, 5))
        return carry

    lax.fori_loop(0, dil16 // UNROLL, dil16_blocks, 0)

    rows_per_step = 256

    def blend(i, carry):
        sl = pl.ds(pl.multiple_of(i * rows_per_step, rows_per_step), rows_per_step)
        lses = [lse_acc[p, sl, :] for p in range(N_PATTERNS)]
        mx = functools.reduce(jnp.maximum, lses)
        ws = [jnp.exp2(l - mx) for l in lses]
        den = functools.reduce(lambda a, b: a + b, ws)
        num = functools.reduce(lambda a, b: a + b, [ws[p] * o_acc[p, sl, :] for p in range(N_PATTERNS)])
        o_ref[sl, :] = (num * (1.0 / den)).astype(BF16)
        return carry

    lax.fori_loop(0, SEQ // rows_per_step, blend, 0)


def _dilated_attention(slopes, proj3):
    batch = proj3.shape[0]
    q_off = (2 * DIFF_QK_WIDTH + DIFF_V_WIDTH) // LANES
    k_off = q_off + DIL_WIDTH // LANES
    v_off = k_off + DIL_WIDTH // LANES
    n_pairs = DIL_WIDTH // LANES
    return pl.pallas_call(
        _dil_attn_kernel,
        grid=(batch, n_pairs),
        in_specs=[
            pl.BlockSpec(memory_space=pltpu.SMEM),
            pl.BlockSpec((None, SEQ, LANES), lambda b, h: (b, 0, q_off + h)),
            pl.BlockSpec((None, SEQ, LANES), lambda b, h: (b, 0, k_off + h)),
            pl.BlockSpec((None, SEQ, LANES), lambda b, h: (b, 0, v_off + h)),
        ],
        out_specs=pl.BlockSpec((None, SEQ, LANES), lambda b, h: (b, 0, h)),
        out_shape=jax.ShapeDtypeStruct((batch, SEQ, DIL_WIDTH), BF16),
        scratch_shapes=[
            pltpu.VMEM((SEQ, LANES), F32),
            pltpu.VMEM((SEQ, LANES), F32),
            pltpu.VMEM((SEQ, LANES), F32),
            pltpu.VMEM((N_PATTERNS, SEQ, LANES), BF16),
            pltpu.VMEM((N_PATTERNS, SEQ, LANES), BF16),
            pltpu.VMEM((N_PATTERNS, LANES, KPAD + SEQ), BF16),
            pltpu.VMEM((N_PATTERNS, KPAD + SEQ, LANES), BF16),
            pltpu.VMEM((2 * N_PATTERNS + 2, Q_BLOCK, 2 * Q_BLOCK), F32),
            pltpu.VMEM((N_PATTERNS, SEQ, LANES), F32),
            pltpu.VMEM((N_PATTERNS, SEQ, LANES), F32),
        ],
        compiler_params=pltpu.CompilerParams(
            dimension_semantics=("arbitrary", "arbitrary"), vmem_limit_bytes=VMEM_LIMIT_ATTN),
        name="dilated_attention",
    )(slopes, proj3, proj3, proj3)


def _alibi_slopes():
    all_s = 2.0 ** (-8.0 * np.arange(1, N_ALIBI_HEADS + 1, dtype=np.float32) / N_ALIBI_HEADS)
    diff_idx = np.arange(0, N_ALIBI_HEADS, N_ALIBI_HEADS // N_DIFF_HEADS)
    dil_idx = np.setdiff1d(np.arange(N_ALIBI_HEADS), diff_idx)
    return tuple(float(s) for s in all_s[diff_idx]), jnp.asarray(all_s[dil_idx], F32)


def kernel(x, ffn1_pre_g, ffn1_w_gate, ffn1_w_up, ffn1_w_down, ffn1_post_g, mix_pre_g, w_in, lambda_q1, lambda_k1, lambda_q2, lambda_k2, diff_subln_g, w_out, mix_post_g, ffn2_pre_g, ffn2_w_gate, ffn2_w_up, ffn2_w_down, ffn2_post_g):
    batch, seq, d_model = x.shape
    assert (seq, d_model) == (SEQ, D_MODEL) and ffn1_pre_g.shape[0] == 1
    slopes_d, slopes_l = _alibi_slopes()

    scale = HEAD_DIM ** -0.5 * LOG2E
    col_scale = np.ones((IN_PROJ_WIDTH,), np.float32)
    col_scale[:DIFF_QK_WIDTH] = scale
    q_l0 = 2 * DIFF_QK_WIDTH + DIFF_V_WIDTH
    col_scale[q_l0:q_l0 + DIL_WIDTH] = scale

    l = 0
    row = lambda g: g[l].reshape(1, -1)
    x2d = x.reshape(batch * seq, d_model)
    x1, proj, vt = _ffn1_inproj(
        x2d, row(ffn1_pre_g), ffn1_w_gate[l].astype(BF16), ffn1_w_up[l].astype(BF16),
        ffn1_w_down[l].astype(BF16), row(ffn1_post_g), row(mix_pre_g),
        (w_in[l] * col_scale).astype(BF16), batch)
    proj3 = proj.reshape(batch, seq, IN_PROJ_WIDTH)
    o_d = _diff_attention(slopes_d, proj3, vt, row(lambda_q1), row(lambda_k1), row(lambda_q2),
                          row(lambda_k2), diff_subln_g[l].reshape(DIFF_V_DIM, 1))
    o_l = _dilated_attention(slopes_l, proj3)
    w_o = w_out[l].astype(BF16)
    out = _outproj_ffn2(
        o_d.reshape(batch * seq, DIFF_V_WIDTH), o_l.reshape(batch * seq, DIL_WIDTH), x1,
        w_o[:DIFF_V_WIDTH], w_o[DIFF_V_WIDTH:], row(mix_post_g), row(ffn2_pre_g),
        ffn2_w_gate[l].astype(BF16), ffn2_w_up[l].astype(BF16), ffn2_w_down[l].astype(BF16),
        row(ffn2_post_g))
    return out.reshape(batch, seq, d_model)
```

```python
import functools
import math

import numpy as np
import jax
import jax.numpy as jnp
from jax import lax
from jax.experimental import pallas as pl
from jax.experimental.pallas import tpu as pltpu

F32 = jnp.float32
BF16 = jnp.bfloat16

D_MODEL = 1024
SEQ = 2048
HEAD_DIM = 64
N_DIFF_HEADS = 4
DIFF_V_DIM = 2 * HEAD_DIM
N_DIL_HEADS = 8
DIL_PATTERNS = ((128, 1), (512, 4), (2048, 16))
Q_BLOCK = 128
D_FF = 2816
RMS_EPS = 1e-6

DIFF_QK_WIDTH = N_DIFF_HEADS * 2 * HEAD_DIM
DIFF_V_WIDTH = N_DIFF_HEADS * DIFF_V_DIM
DIL_WIDTH = N_DIL_HEADS * HEAD_DIM
IN_PROJ_WIDTH = 2 * DIFF_QK_WIDTH + DIFF_V_WIDTH + 3 * DIL_WIDTH
N_ALIBI_HEADS = N_DIFF_HEADS + N_DIL_HEADS
LAM_INIT = 0.8 - 0.6 * math.exp(-0.3 * 0)

LANES = 128
NEG_BIG = -1e30
VMEM_LIMIT_FFN = 56 * 1024 * 1024
VMEM_LIMIT_ATTN = 52 * 1024 * 1024

TM = 512
TQ = 256
TK = 256


def _rms(x, g):
    ms = jnp.mean(x * x, axis=-1, keepdims=True)
    return x * lax.rsqrt(ms + RMS_EPS) * g


def _swiglu_half_step(x, pre_g, wg_ref, wu_ref, wd_ref, post_g):
    xb = _rms(x, pre_g).astype(BF16)
    g = jnp.dot(xb, wg_ref[...], preferred_element_type=F32)
    u = jnp.dot(xb, wu_ref[...], preferred_element_type=F32)
    h = (g * (1.0 / (1.0 + jnp.exp(-g))) * u).astype(BF16)
    y = jnp.dot(h, wd_ref[...], preferred_element_type=F32)
    return x + 0.5 * _rms(y, post_g)


def _ffn1_inproj_kernel(x_ref, pre_g, wg, wu, wd, post_g, mix_g, w_in, x1_ref, proj_ref, vt_ref):
    x1 = _swiglu_half_step(x_ref[...], pre_g[...], wg, wu, wd, post_g[...])
    x1_ref[...] = x1
    hb = _rms(x1, mix_g[...]).astype(BF16)
    proj = jnp.dot(hb, w_in[...], preferred_element_type=F32)
    proj_ref[...] = proj.astype(BF16)
    v_d = proj[:, 2 * DIFF_QK_WIDTH:2 * DIFF_QK_WIDTH + DIFF_V_WIDTH]
    vt_ref[...] = v_d.T.astype(BF16)


def _const_spec(shape):
    nd = len(shape)
    return pl.BlockSpec(shape, lambda *_: (0,) * nd, pipeline_mode=pl.Buffered(1))


def _ffn1_inproj(x2d, pre_g, wg, wu, wd, post_g, mix_g, w_in, batch):
    t = x2d.shape[0]
    tiles_per_seq = SEQ // TM
    row = lambda i: (i, 0)
    return pl.pallas_call(
        _ffn1_inproj_kernel,
        grid=(t // TM,),
        in_specs=[
            pl.BlockSpec((TM, D_MODEL), row),
            _const_spec((1, D_MODEL)),
            _const_spec((D_MODEL, D_FF)),
            _const_spec((D_MODEL, D_FF)),
            _const_spec((D_FF, D_MODEL)),
            _const_spec((1, D_MODEL)),
            _const_spec((1, D_MODEL)),
            _const_spec((D_MODEL, IN_PROJ_WIDTH)),
        ],
        out_specs=[
            pl.BlockSpec((TM, D_MODEL), row),
            pl.BlockSpec((TM, IN_PROJ_WIDTH), row),
            pl.BlockSpec((None, DIFF_V_WIDTH, TM), lambda i: (i // tiles_per_seq, 0, i % tiles_per_seq)),
        ],
        out_shape=[
            jax.ShapeDtypeStruct((t, D_MODEL), F32),
            jax.ShapeDtypeStruct((t, IN_PROJ_WIDTH), BF16),
            jax.ShapeDtypeStruct((batch, DIFF_V_WIDTH, SEQ), BF16),
        ],
        compiler_params=pltpu.CompilerParams(
            dimension_semantics=("arbitrary",), vmem_limit_bytes=VMEM_LIMIT_FFN),
        name="ffn1_inproj",
    )(x2d, pre_g, wg, wu, wd, post_g, mix_g, w_in)


def _outproj_ffn2_kernel(od_ref, ol_ref, x1_ref, wo_d, wo_l, mix_post_g, pre_g, wg, wu, wd, post_g, out_ref):
    h = (jnp.dot(od_ref[...], wo_d[...], preferred_element_type=F32)
         + jnp.dot(ol_ref[...], wo_l[...], preferred_element_type=F32))
    x2 = x1_ref[...] + _rms(h, mix_post_g[...])
    out_ref[...] = _swiglu_half_step(x2, pre_g[...], wg, wu, wd, post_g[...])


def _outproj_ffn2(od, ol, x1, wo_d, wo_l, mix_post_g, pre_g, wg, wu, wd, post_g):
    t = x1.shape[0]
    row = lambda i: (i, 0)
    return pl.pallas_call(
        _outproj_ffn2_kernel,
        grid=(t // TM,),
        in_specs=[
            pl.BlockSpec((TM, DIFF_V_WIDTH), row),
            pl.BlockSpec((TM, DIL_WIDTH), row),
            pl.BlockSpec((TM, D_MODEL), row),
            _const_spec((DIFF_V_WIDTH, D_MODEL)),
            _const_spec((DIL_WIDTH, D_MODEL)),
            _const_spec((1, D_MODEL)),
            _const_spec((1, D_MODEL)),
            _const_spec((D_MODEL, D_FF)),
            _const_spec((D_MODEL, D_FF)),
            _const_spec((D_FF, D_MODEL)),
            _const_spec((1, D_MODEL)),
        ],
        out_specs=pl.BlockSpec((TM, D_MODEL), row),
        out_shape=jax.ShapeDtypeStruct((t, D_MODEL), F32),
        compiler_params=pltpu.CompilerParams(
            dimension_semantics=("arbitrary",), vmem_limit_bytes=VMEM_LIMIT_FFN),
        name="outproj_ffn2",
    )(od, ol, x1, wo_d, wo_l, mix_post_g, pre_g, wg, wu, wd, post_g)


_NT = (((1,), (1,)), ((), ()))
N_MAPS = 2 * N_DIFF_HEADS
ONES_ROWS = 16
SLOPE_TERMS = 3
N_QT = SEQ // TQ
N_TILES = N_QT * (N_QT + 1) // 2
LOG2E = math.log2(math.e)


def _bf16_terms(x, n_terms):
    terms, rest = [], np.float32(x)
    for _ in range(n_terms):
        t = np.float32(np.asarray(rest, dtype=BF16))
        terms.append(float(t))
        rest = np.float32(rest - t)
    return terms


def _diff_attn_kernel(q_ref, k_ref, vt_ref, lq1, lk1, lq2, lk2, g_ref, o_ref,
                      kp, qp, vta, masks, s_buf, p_buf, acc, *, slopes):
    lam = (jnp.exp(jnp.sum(lq1[...] * lk1[...], axis=-1, keepdims=True))
           - jnp.exp(jnp.sum(lq2[...] * lk2[...], axis=-1, keepdims=True)) + LAM_INIT)
    lane = lax.broadcasted_iota(jnp.int32, (1, LANES), 1)
    first_map = lane < HEAD_DIM

    def in_lanes(base, values):
        row = jnp.zeros((1, LANES), F32)
        for i, v in enumerate(values):
            row = jnp.where(lane == base + i, v, row)
        return row.astype(BF16)

    slope_rows = [(in_lanes(HEAD_DIM, _bf16_terms(s * LOG2E, SLOPE_TERMS)),
                   in_lanes(0, _bf16_terms(s * LOG2E, SLOPE_TERMS))) for s in slopes]
    key_idx = lax.broadcasted_iota(jnp.int32, (TK, LANES), 0).astype(F32)
    idx_first = jnp.where((lane >= HEAD_DIM) & (lane < HEAD_DIM + SLOPE_TERMS), key_idx, 0.0).astype(BF16)
    idx_second = jnp.where(lane < SLOPE_TERMS, key_idx, 0.0).astype(BF16)

    def augment(i, carry):
        rows = pl.ds(pl.multiple_of(i * TK, TK), TK)
        for h in range(N_DIFF_HEADS):
            k = k_ref[rows, h * LANES:(h + 1) * LANES]
            kp[2 * h, rows, :] = jnp.where(first_map, k, idx_first)
            kp[2 * h + 1, rows, :] = jnp.where(first_map, idx_second, k)
            q = q_ref[rows, h * LANES:(h + 1) * LANES]
            qp[2 * h, rows, :] = jnp.where(first_map, q, slope_rows[h][0])
            qp[2 * h + 1, rows, :] = jnp.where(first_map, slope_rows[h][1], q)
        return carry

    lax.fori_loop(0, SEQ // TK, augment, 0)
    for h in range(N_DIFF_HEADS):
        vta[h, 0:DIFF_V_DIM, :] = vt_ref[h * DIFF_V_DIM:(h + 1) * DIFF_V_DIM, :]
        vta[h, DIFF_V_DIM:, :] = jnp.ones((ONES_ROWS, SEQ), BF16)
    causal = (lax.broadcasted_iota(jnp.int32, (TK, TQ), 1)
              >= lax.broadcasted_iota(jnp.int32, (TK, TQ), 0))
    masks[0] = jnp.zeros((TK, TQ), F32)
    masks[1] = jnp.where(causal, 0.0, NEG_BIG)

    def clear(qi, carry):
        acc[qi] = jnp.zeros((N_MAPS, DIFF_V_DIM + ONES_ROWS, TQ), F32)
        return carry

    lax.fori_loop(0, N_QT, clear, 0)

    def score_matmuls(qi, t, slot):
        qs = pl.multiple_of(qi * TQ, TQ)
        ks = pl.multiple_of((qi - t) * TK, TK)
        mask = masks[(t == 0).astype(jnp.int32)]
        for c in range(N_MAPS):
            s = lax.dot_general(kp[c, pl.ds(ks, TK), :], qp[c, pl.ds(qs, TQ), :], _NT,
                                preferred_element_type=F32)
            s_buf[slot, c] = s + mask

    def softmax_step(t, slot, maxima):
        on_diagonal = t == 0
        dist = jnp.asarray(t * TK, F32)
        new_m, alphas = [], []
        for c in range(N_MAPS):
            coff = (slopes[c // 2] * LOG2E) * dist
            m = jnp.where(on_diagonal, NEG_BIG, maxima[c])
            s = s_buf[slot, c]
            n = jnp.maximum(m, jnp.max(s, axis=0, keepdims=True) - coff)
            new_m.append(n)
            alphas.append(jnp.exp2(m - n))
            p_buf[slot, c] = jnp.exp2(s - (n + coff)).astype(BF16)
        return tuple(new_m), tuple(alphas)

    def value_matmuls(qi, t, slot, alphas):
        ks = pl.multiple_of((qi - t) * TK, TK)
        for c in range(N_MAPS):
            pv = jnp.dot(vta[c // 2, :, pl.ds(ks, TK)], p_buf[slot, c], preferred_element_type=F32)
            acc[qi, c] = alphas[c] * acc[qi, c] + pv

    def next_tile(qi, t):
        last = t == qi
        nq = jnp.where(last, qi + 1, qi)
        nt = jnp.where(last, 0, t + 1)
        return jnp.minimum(nq, N_QT - 1), nt

    zero = jnp.int32(0)
    score_matmuls(zero, zero, 0)
    p_buf[1] = jnp.zeros((N_MAPS, TK, TQ), BF16)
    row = lambda v: tuple(jnp.full((1, TQ), v, F32) for _ in range(N_MAPS))

    def two_tiles(i, carry):
        qi, t, qi_prev, t_prev, maxima, alphas = carry
        q1, t1 = next_tile(qi, t)
        score_matmuls(q1, t1, 1)
        maxima, alphas0 = softmax_step(t, 0, maxima)
        value_matmuls(qi_prev, t_prev, 1, alphas)
        q2, t2 = next_tile(q1, t1)
        score_matmuls(q2, t2, 0)
        maxima, alphas1 = softmax_step(t1, 1, maxima)
        value_matmuls(qi, t, 0, alphas0)
        return q2, t2, q1, t1, maxima, alphas1

    assert N_TILES % 2 == 0
    carry = lax.fori_loop(0, N_TILES // 2, two_tiles, (zero, zero, zero, zero, row(0.0), row(1.0)))
    _, _, qi_prev, t_prev, _, alphas = carry
    value_matmuls(qi_prev, t_prev, 1, alphas)

    def finish(qi, carry):
        qs = pl.multiple_of(qi * TQ, TQ)
        for h in range(N_DIFF_HEADS):
            a1, a2 = acc[qi, 2 * h], acc[qi, 2 * h + 1]
            l1, l2 = a1[DIFF_V_DIM:DIFF_V_DIM + 1, :], a2[DIFF_V_DIM:DIFF_V_DIM + 1, :]
            o = a1[:DIFF_V_DIM, :] * (1.0 / l1) - lam * (a2[:DIFF_V_DIM, :] * (1.0 / l2))
            ms = jnp.mean(o * o, axis=0, keepdims=True)
            y = o * lax.rsqrt(ms + RMS_EPS) * g_ref[...] * (1.0 - LAM_INIT)
            o_ref[pl.ds(qs, TQ), h * LANES:(h + 1) * LANES] = y.T.astype(BF16)
        return carry

    lax.fori_loop(0, N_QT, finish, 0)


def _diff_attention(slopes, proj3, vt, lq1, lk1, lq2, lk2, g_col):
    batch = proj3.shape[0]
    vec = pl.BlockSpec((1, HEAD_DIM), lambda b: (0, 0))
    return pl.pallas_call(
        functools.partial(_diff_attn_kernel, slopes=slopes),
        grid=(batch,),
        in_specs=[
            pl.BlockSpec((None, SEQ, DIFF_QK_WIDTH), lambda b: (b, 0, 0)),
            pl.BlockSpec((None, SEQ, DIFF_QK_WIDTH), lambda b: (b, 0, 1)),
            pl.BlockSpec((None, DIFF_V_WIDTH, SEQ), lambda b: (b, 0, 0)),
            vec, vec, vec, vec,
            pl.BlockSpec((DIFF_V_DIM, 1), lambda b: (0, 0)),
        ],
        out_specs=pl.BlockSpec((None, SEQ, DIFF_V_WIDTH), lambda b: (b, 0, 0)),
        out_shape=jax.ShapeDtypeStruct((batch, SEQ, DIFF_V_WIDTH), BF16),
        scratch_shapes=[
            pltpu.VMEM((N_MAPS, SEQ, LANES), BF16),
            pltpu.VMEM((N_MAPS, SEQ, LANES), BF16),
            pltpu.VMEM((N_DIFF_HEADS, DIFF_V_DIM + ONES_ROWS, SEQ), BF16),
            pltpu.VMEM((2, TK, TQ), F32),
            pltpu.VMEM((2, N_MAPS, TK, TQ), F32),
            pltpu.VMEM((2, N_MAPS, TK, TQ), BF16),
            pltpu.VMEM((N_QT, N_MAPS, DIFF_V_DIM + ONES_ROWS, TQ), F32),
        ],
        compiler_params=pltpu.CompilerParams(
            dimension_semantics=("arbitrary",), vmem_limit_bytes=VMEM_LIMIT_ATTN),
        name="diff_attention",
    )(proj3, proj3, vt, lq1, lk1, lq2, lk2, g_col)


N_PATTERNS = len(DIL_PATTERNS)
N_WIN = Q_BLOCK
N_BLOCKS = SEQ // Q_BLOCK
KPAD = Q_BLOCK
UNROLL = 4


def _dil_attn_kernel(slopes_ref, q_ref, k_ref, v_ref, o_ref,
                     qf, kf, vf, qa, qb, kt, vv, bias, o_acc, lse_acc):
    hp = pl.program_id(1)
    lane = lax.broadcasted_iota(jnp.int32, (1, LANES), 1)
    first_head = lane < HEAD_DIM

    qf[...] = q_ref[...].astype(F32)
    kf[...] = k_ref[...].astype(F32)
    vf[...] = v_ref[...].astype(F32)
    kt[0, :, 0:KPAD] = jnp.zeros((LANES, KPAD), BF16)
    vv[0, 0:KPAD, :] = jnp.zeros((KPAD, LANES), BF16)

    col = lax.broadcasted_iota(jnp.int32, (Q_BLOCK, 2 * Q_BLOCK), 1)
    dsub = Q_BLOCK + lax.broadcasted_iota(jnp.int32, (Q_BLOCK, 2 * Q_BLOCK), 0) - col
    valid = (dsub >= 0) & (dsub <= N_WIN)
    dsub_f = dsub.astype(F32)
    for p, (_, dil) in enumerate(DIL_PATTERNS):
        for hh in range(2):
            slope = slopes_ref[2 * hp + hh]
            b = jnp.where(valid, (-slope * (LOG2E * dil)) * dsub_f, NEG_BIG)
            bias[2 * p + hh] = b
            if p == 0:
                bias[2 * N_PATTERNS + hh] = jnp.where(col >= Q_BLOCK, b, NEG_BIG)

    def strided_rows(ref, start, dil):
        if dil == 1:
            return ref[pl.ds(start, Q_BLOCK), :]
        return ref[pl.ds(start, Q_BLOCK, stride=dil), :]

    def prepare(c, carry):
        dst = pl.multiple_of(c * Q_BLOCK, Q_BLOCK)
        for p, (_, dil) in enumerate(DIL_PATTERNS):
            blocks_per_class = N_BLOCKS // dil
            r = c >> (blocks_per_class.bit_length() - 1)
            n = c & (blocks_per_class - 1)
            start = n * (Q_BLOCK * dil) + r
            if dil == 1:
                start = pl.multiple_of(start, Q_BLOCK)
            q = strided_rows(qf, start, dil)
            qa[p, pl.ds(dst, Q_BLOCK), :] = jnp.where(first_head, q, 0.0).astype(BF16)
            qb[p, pl.ds(dst, Q_BLOCK), :] = jnp.where(first_head, 0.0, q).astype(BF16)
            kt[p, :, pl.ds(KPAD + dst, Q_BLOCK)] = strided_rows(kf, start, dil).T.astype(BF16)
            vv[p, pl.ds(KPAD + dst, Q_BLOCK), :] = strided_rows(vf, start, dil).astype(BF16)
        return carry

    lax.fori_loop(0, N_BLOCKS, prepare, 0)

    def attend(p, dil, q_row, with_prev, nat_start, bias_idx):
        n_keys = 2 * Q_BLOCK if with_prev else Q_BLOCK
        key_row = q_row + (KPAD - Q_BLOCK if with_prev else KPAD)
        if not isinstance(q_row, int):
            q_row = pl.multiple_of(q_row, Q_BLOCK)
            key_row = pl.multiple_of(key_row, Q_BLOCK)
        k_t = kt[p, :, pl.ds(key_row, n_keys)]
        v = vv[p, pl.ds(key_row, n_keys), :]
        outs = []
        for hh, q_src in enumerate((qa, qb)):
            s = jnp.dot(q_src[p, pl.ds(q_row, Q_BLOCK), :], k_t, preferred_element_type=F32)
            if with_prev:
                s = s + bias[bias_idx[hh]]
            else:
                s = s + bias[bias_idx[hh], :, Q_BLOCK:]
            m = jnp.max(s, axis=1, keepdims=True)
            e = jnp.exp2(s - m)
            l = jnp.sum(e, axis=1, keepdims=True)
            oh = jnp.dot(e.astype(BF16), v, preferred_element_type=F32) * (1.0 / l)
            outs.append((oh, m + jnp.log2(l)))
        o = jnp.where(first_head, outs[0][0], outs[1][0])
        lse = jnp.where(first_head, outs[0][1], outs[1][1])
        if dil == 1:
            nat = pl.ds(pl.multiple_of(nat_start, Q_BLOCK), Q_BLOCK)
        else:
            nat = pl.ds(nat_start, Q_BLOCK, stride=dil)
        o_acc[p, nat, :] = o
        lse_acc[p, nat, :] = lse

    def dense_blocks(i, carry):
        for u in range(UNROLL):
            n = i * UNROLL + u
            if u == 0:
                idx = tuple(jnp.where(i == 0, 2 * N_PATTERNS + hh, hh) for hh in range(2))
            else:
                idx = (0, 1)
            attend(0, 1, n * Q_BLOCK, True, n * Q_BLOCK, idx)
        return carry

    lax.fori_loop(0, N_BLOCKS // UNROLL, dense_blocks, 0)

    dil4 = DIL_PATTERNS[1][1]
    run = SEQ // dil4
    for r in range(dil4):
        attend(1, dil4, r * run, False, r, (2, 3))

    def dil4_blocks(n, carry):
        for r in range(dil4):
            attend(1, dil4, r * run + n * Q_BLOCK, True, n * (Q_BLOCK * dil4) + r, (2, 3))
        return carry

    lax.fori_loop(1, run // Q_BLOCK, dil4_blocks, 0)

    dil16 = DIL_PATTERNS[2][1]

    def dil16_blocks(i, carry):
        for u in range(UNROLL):
            r = i * UNROLL + u
            attend(2, dil16, r * Q_BLOCK, False, r, (4, 5))
        return carry

    lax.fori_loop(0, dil16 // UNROLL, dil16_blocks, 0)

    rows_per_step = 256

    def blend(i, carry):
        sl = pl.ds(pl.multiple_of(i * rows_per_step, rows_per_step), rows_per_step)
        lses = [lse_acc[p, sl, :] for p in range(N_PATTERNS)]
        mx = functools.reduce(jnp.maximum, lses)
        ws = [jnp.exp2(l - mx) for l in lses]
        den = functools.reduce(lambda a, b: a + b, ws)
        num = functools.reduce(lambda a, b: a + b, [ws[p] * o_acc[p, sl, :] for p in range(N_PATTERNS)])
        o_ref[sl, :] = (num * (1.0 / den)).astype(BF16)
        return carry

    lax.fori_loop(0, SEQ // rows_per_step, blend, 0)


def _dilated_attention(slopes, proj3):
    batch = proj3.shape[0]
    q_off = (2 * DIFF_QK_WIDTH + DIFF_V_WIDTH) // LANES
    k_off = q_off + DIL_WIDTH // LANES
    v_off = k_off + DIL_WIDTH // LANES
    n_pairs = DIL_WIDTH // LANES
    return pl.pallas_call(
        _dil_attn_kernel,
        grid=(batch, n_pairs),
        in_specs=[
            pl.BlockSpec(memory_space=pltpu.SMEM),
            pl.BlockSpec((None, SEQ, LANES), lambda b, h: (b, 0, q_off + h)),
            pl.BlockSpec((None, SEQ, LANES), lambda b, h: (b, 0, k_off + h)),
            pl.BlockSpec((None, SEQ, LANES), lambda b, h: (b, 0, v_off + h)),
        ],
        out_specs=pl.BlockSpec((None, SEQ, LANES), lambda b, h: (b, 0, h)),
        out_shape=jax.ShapeDtypeStruct((batch, SEQ, DIL_WIDTH), BF16),
        scratch_shapes=[
            pltpu.VMEM((SEQ, LANES), F32),
            pltpu.VMEM((SEQ, LANES), F32),
            pltpu.VMEM((SEQ, LANES), F32),
            pltpu.VMEM((N_PATTERNS, SEQ, LANES), BF16),
            pltpu.VMEM((N_PATTERNS, SEQ, LANES), BF16),
            pltpu.VMEM((N_PATTERNS, LANES, KPAD + SEQ), BF16),
            pltpu.VMEM((N_PATTERNS, KPAD + SEQ, LANES), BF16),
            pltpu.VMEM((2 * N_PATTERNS + 2, Q_BLOCK, 2 * Q_BLOCK), F32),
            pltpu.VMEM((N_PATTERNS, SEQ, LANES), F32),
            pltpu.VMEM((N_PATTERNS, SEQ, LANES), F32),
        ],
        compiler_params=pltpu.CompilerParams(
            dimension_semantics=("arbitrary", "arbitrary"), vmem_limit_bytes=VMEM_LIMIT_ATTN),
        name="dilated_attention",
    )(slopes, proj3, proj3, proj3)


def _alibi_slopes():
    all_s = 2.0 ** (-8.0 * np.arange(1, N_ALIBI_HEADS + 1, dtype=np.float32) / N_ALIBI_HEADS)
    diff_idx = np.arange(0, N_ALIBI_HEADS, N_ALIBI_HEADS // N_DIFF_HEADS)
    dil_idx = np.setdiff1d(np.arange(N_ALIBI_HEADS), diff_idx)
    return tuple(float(s) for s in all_s[diff_idx]), jnp.asarray(all_s[dil_idx], F32)


def kernel(x, ffn1_pre_g, ffn1_w_gate, ffn1_w_up, ffn1_w_down, ffn1_post_g, mix_pre_g, w_in, lambda_q1, lambda_k1, lambda_q2, lambda_k2, diff_subln_g, w_out, mix_post_g, ffn2_pre_g, ffn2_w_gate, ffn2_w_up, ffn2_w_down, ffn2_post_g):
    batch, seq, d_model = x.shape
    assert (seq, d_model) == (SEQ, D_MODEL) and ffn1_pre_g.shape[0] == 1
    slopes_d, slopes_l = _alibi_slopes()

    scale = HEAD_DIM ** -0.5 * LOG2E
    col_scale = np.ones((IN_PROJ_WIDTH,), np.float32)
    col_scale[:DIFF_QK_WIDTH] = scale
    q_l0 = 2 * DIFF_QK_WIDTH + DIFF_V_WIDTH
    col_scale[q_l0:q_l0 + DIL_WIDTH] = scale

    l = 0
    row = lambda g: g[l].reshape(1, -1)
    x2d = x.reshape(batch * seq, d_model)
    x1, proj, vt = _ffn1_inproj(
        x2d, row(ffn1_pre_g), ffn1_w_gate[l].astype(BF16), ffn1_w_up[l].astype(BF16),
        ffn1_w_down[l].astype(BF16), row(ffn1_post_g), row(mix_pre_g),
        (w_in[l] * col_scale).astype(BF16), batch)
    proj3 = proj.reshape(batch, seq, IN_PROJ_WIDTH)
    o_d = _diff_attention(slopes_d, proj3, vt, row(lambda_q1), row(lambda_k1), row(lambda_q2),
                          row(lambda_k2), diff_subln_g[l].reshape(DIFF_V_DIM, 1))
    o_l = _dilated_attention(slopes_l, proj3)
    w_o = w_out[l].astype(BF16)
    out = _outproj_ffn2(
        o_d.reshape(batch * seq, DIFF_V_WIDTH), o_l.reshape(batch * seq, DIL_WIDTH), x1,
        w_o[:DIFF_V_WIDTH], w_o[DIFF_V_WIDTH:], row(mix_post_g), row(ffn2_pre_g),
        ffn2_w_gate[l].astype(BF16), ffn2_w_up[l].astype(BF16), ffn2_w_down[l].astype(BF16),
        row(ffn2_post_g))
    return out.reshape(batch, seq, d_model)
```

```python
import functools
import math

import numpy as np
import jax
import jax.numpy as jnp
from jax import lax
from jax.experimental import pallas as pl
from jax.experimental.pallas import tpu as pltpu

F32 = jnp.float32
BF16 = jnp.bfloat16

D_MODEL = 1024
SEQ = 2048
HEAD_DIM = 64
N_DIFF_HEADS = 4
DIFF_V_DIM = 2 * HEAD_DIM
N_DIL_HEADS = 8
DIL_PATTERNS = ((128, 1), (512, 4), (2048, 16))
Q_BLOCK = 128
D_FF = 2816
RMS_EPS = 1e-6

DIFF_QK_WIDTH = N_DIFF_HEADS * 2 * HEAD_DIM
DIFF_V_WIDTH = N_DIFF_HEADS * DIFF_V_DIM
DIL_WIDTH = N_DIL_HEADS * HEAD_DIM
IN_PROJ_WIDTH = 2 * DIFF_QK_WIDTH + DIFF_V_WIDTH + 3 * DIL_WIDTH
N_ALIBI_HEADS = N_DIFF_HEADS + N_DIL_HEADS
LAM_INIT = 0.8 - 0.6 * math.exp(-0.3 * 0)

LANES = 128
NEG_BIG = -1e30
VMEM_LIMIT_FFN = 56 * 1024 * 1024
VMEM_LIMIT_ATTN = 52 * 1024 * 1024

TM = 512
TQ = 256
TK = 256


def _rms(x, g):
    ms = jnp.mean(x * x, axis=-1, keepdims=True)
    return x * lax.rsqrt(ms + RMS_EPS) * g


def _swiglu_half_step(x, pre_g, wg_ref, wu_ref, wd_ref, post_g):
    xb = _rms(x, pre_g).astype(BF16)
    g = jnp.dot(xb, wg_ref[...], preferred_element_type=F32)
    u = jnp.dot(xb, wu_ref[...], preferred_element_type=F32)
    h = (g * (1.0 / (1.0 + jnp.exp(-g))) * u).astype(BF16)
    y = jnp.dot(h, wd_ref[...], preferred_element_type=F32)
    return x + 0.5 * _rms(y, post_g)


def _ffn1_inproj_kernel(x_ref, pre_g, wg, wu, wd, post_g, mix_g, w_in, x1_ref, proj_ref, vt_ref):
    x1 = _swiglu_half_step(x_ref[...], pre_g[...], wg, wu, wd, post_g[...])
    x1_ref[...] = x1
    hb = _rms(x1, mix_g[...]).astype(BF16)
    proj = jnp.dot(hb, w_in[...], preferred_element_type=F32)
    proj_ref[...] = proj.astype(BF16)
    v_d = proj[:, 2 * DIFF_QK_WIDTH:2 * DIFF_QK_WIDTH + DIFF_V_WIDTH]
    vt_ref[...] = v_d.T.astype(BF16)


def _const_spec(shape):
    nd = len(shape)
    return pl.BlockSpec(shape, lambda *_: (0,) * nd, pipeline_mode=pl.Buffered(1))


def _ffn1_inproj(x2d, pre_g, wg, wu, wd, post_g, mix_g, w_in, batch):
    t = x2d.shape[0]
    tiles_per_seq = SEQ // TM
    row = lambda i: (i, 0)
    return pl.pallas_call(
        _ffn1_inproj_kernel,
        grid=(t // TM,),
        in_specs=[
            pl.BlockSpec((TM, D_MODEL), row),
            _const_spec((1, D_MODEL)),
            _const_spec((D_MODEL, D_FF)),
            _const_spec((D_MODEL, D_FF)),
            _const_spec((D_FF, D_MODEL)),
            _const_spec((1, D_MODEL)),
            _const_spec((1, D_MODEL)),
            _const_spec((D_MODEL, IN_PROJ_WIDTH)),
        ],
        out_specs=[
            pl.BlockSpec((TM, D_MODEL), row),
            pl.BlockSpec((TM, IN_PROJ_WIDTH), row),
            pl.BlockSpec((None, DIFF_V_WIDTH, TM), lambda i: (i // tiles_per_seq, 0, i % tiles_per_seq)),
        ],
        out_shape=[
            jax.ShapeDtypeStruct((t, D_MODEL), F32),
            jax.ShapeDtypeStruct((t, IN_PROJ_WIDTH), BF16),
            jax.ShapeDtypeStruct((batch, DIFF_V_WIDTH, SEQ), BF16),
        ],
        compiler_params=pltpu.CompilerParams(
            dimension_semantics=("arbitrary",), vmem_limit_bytes=VMEM_LIMIT_FFN),
        name="ffn1_inproj",
    )(x2d, pre_g, wg, wu, wd, post_g, mix_g, w_in)


def _outproj_ffn2_kernel(od_ref, ol_ref, x1_ref, wo_d, wo_l, mix_post_g, pre_g, wg, wu, wd, post_g, out_ref):
    h = (jnp.dot(od_ref[...], wo_d[...], preferred_element_type=F32)
         + jnp.dot(ol_ref[...], wo_l[...], preferred_element_type=F32))
    x2 = x1_ref[...] + _rms(h, mix_post_g[...])
    out_ref[...] = _swiglu_half_step(x2, pre_g[...], wg, wu, wd, post_g[...])


def _outproj_ffn2(od, ol, x1, wo_d, wo_l, mix_post_g, pre_g, wg, wu, wd, post_g):
    t = x1.shape[0]
    row = lambda i: (i, 0)
    return pl.pallas_call(
        _outproj_ffn2_kernel,
        grid=(t // TM,),
        in_specs=[
            pl.BlockSpec((TM, DIFF_V_WIDTH), row),
            pl.BlockSpec((TM, DIL_WIDTH), row),
            pl.BlockSpec((TM, D_MODEL), row),
            _const_spec((DIFF_V_WIDTH, D_MODEL)),
            _const_spec((DIL_WIDTH, D_MODEL)),
            _const_spec((1, D_MODEL)),
            _const_spec((1, D_MODEL)),
            _const_spec((D_MODEL, D_FF)),
            _const_spec((D_MODEL, D_FF)),
            _const_spec((D_FF, D_MODEL)),
            _const_spec((1, D_MODEL)),
        ],
        out_specs=pl.BlockSpec((TM, D_MODEL), row),
        out_shape=jax.ShapeDtypeStruct((t, D_MODEL), F32),
        compiler_params=pltpu.CompilerParams(
            dimension_semantics=("arbitrary",), vmem_limit_bytes=VMEM_LIMIT_FFN),
        name="outproj_ffn2",
    )(od, ol, x1, wo_d, wo_l, mix_post_g, pre_g, wg, wu, wd, post_g)


_NT = (((1,), (1,)), ((), ()))
N_MAPS = 2 * N_DIFF_HEADS
ONES_ROWS = 16
SLOPE_TERMS = 3
N_QT = SEQ // TQ
N_TILES = N_QT * (N_QT + 1) // 2
LOG2E = math.log2(math.e)


def _bf16_terms(x, n_terms):
    terms, rest = [], np.float32(x)
    for _ in range(n_terms):
        t = np.float32(np.asarray(rest, dtype=BF16))
        terms.append(float(t))
        rest = np.float32(rest - t)
    return terms


def _diff_attn_kernel(q_ref, k_ref, vt_ref, lq1, lk1, lq2, lk2, g_ref, o_ref,
                      kp, qp, vta, masks, s_buf, p_buf, acc, *, slopes):
    lam = (jnp.exp(jnp.sum(lq1[...] * lk1[...], axis=-1, keepdims=True))
           - jnp.exp(jnp.sum(lq2[...] * lk2[...], axis=-1, keepdims=True)) + LAM_INIT)
    lane = lax.broadcasted_iota(jnp.int32, (1, LANES), 1)
    first_map = lane < HEAD_DIM

    def in_lanes(base, values):
        row = jnp.zeros((1, LANES), F32)
        for i, v in enumerate(values):
            row = jnp.where(lane == base + i, v, row)
        return row.astype(BF16)

    slope_rows = [(in_lanes(HEAD_DIM, _bf16_terms(s * LOG2E, SLOPE_TERMS)),
                   in_lanes(0, _bf16_terms(s * LOG2E, SLOPE_TERMS))) for s in slopes]
    key_idx = lax.broadcasted_iota(jnp.int32, (TK, LANES), 0).astype(F32)
    idx_first = jnp.where((lane >= HEAD_DIM) & (lane < HEAD_DIM + SLOPE_TERMS), key_idx, 0.0).astype(BF16)
    idx_second = jnp.where(lane < SLOPE_TERMS, key_idx, 0.0).astype(BF16)

    def augment(i, carry):
        rows = pl.ds(pl.multiple_of(i * TK, TK), TK)
        for h in range(N_DIFF_HEADS):
            k = k_ref[rows, h * LANES:(h + 1) * LANES]
            kp[2 * h, rows, :] = jnp.where(first_map, k, idx_first)
            kp[2 * h + 1, rows, :] = jnp.where(first_map, idx_second, k)
            q = q_ref[rows, h * LANES:(h + 1) * LANES]
            qp[2 * h, rows, :] = jnp.where(first_map, q, slope_rows[h][0])
            qp[2 * h + 1, rows, :] = jnp.where(first_map, slope_rows[h][1], q)
        return carry

    lax.fori_loop(0, SEQ // TK, augment, 0)
    for h in range(N_DIFF_HEADS):
        vta[h, 0:DIFF_V_DIM, :] = vt_ref[h * DIFF_V_DIM:(h + 1) * DIFF_V_DIM, :]
        vta[h, DIFF_V_DIM:, :] = jnp.ones((ONES_ROWS, SEQ), BF16)
    causal = (lax.broadcasted_iota(jnp.int32, (TK, TQ), 1)
              >= lax.broadcasted_iota(jnp.int32, (TK, TQ), 0))
    masks[0] = jnp.zeros((TK, TQ), F32)
    masks[1] = jnp.where(causal, 0.0, NEG_BIG)

    def clear(qi, carry):
        acc[qi] = jnp.zeros((N_MAPS, DIFF_V_DIM + ONES_ROWS, TQ), F32)
        return carry

    lax.fori_loop(0, N_QT, clear, 0)

    def score_matmuls(qi, t, slot):
        qs = pl.multiple_of(qi * TQ, TQ)
        ks = pl.multiple_of((qi - t) * TK, TK)
        mask = masks[(t == 0).astype(jnp.int32)]
        for c in range(N_MAPS):
            s = lax.dot_general(kp[c, pl.ds(ks, TK), :], qp[c, pl.ds(qs, TQ), :], _NT,
                                preferred_element_type=F32)
            s_buf[slot, c] = s + mask

    def softmax_step(t, slot, maxima):
        on_diagonal = t == 0
        dist = jnp.asarray(t * TK, F32)
        new_m, alphas = [], []
        for c in range(N_MAPS):
            coff = (slopes[c // 2] * LOG2E) * dist
            m = jnp.where(on_diagonal, NEG_BIG, maxima[c])
            s = s_buf[slot, c]
            n = jnp.maximum(m, jnp.max(s, axis=0, keepdims=True) - coff)
            new_m.append(n)
            alphas.append(jnp.exp2(m - n))
            p_buf[slot, c] = jnp.exp2(s - (n + coff)).astype(BF16)
        return tuple(new_m), tuple(alphas)

    def value_matmuls(qi, t, slot, alphas):
        ks = pl.multiple_of((qi - t) * TK, TK)
        for c in range(N_MAPS):
            pv = jnp.dot(vta[c // 2, :, pl.ds(ks, TK)], p_buf[slot, c], preferred_element_type=F32)
            acc[qi, c] = alphas[c] * acc[qi, c] + pv

    def next_tile(qi, t):
        last = t == qi
        nq = jnp.where(last, qi + 1, qi)
        nt = jnp.where(last, 0, t + 1)
        return jnp.minimum(nq, N_QT - 1), nt

    zero = jnp.int32(0)
    score_matmuls(zero, zero, 0)
    p_buf[1] = jnp.zeros((N_MAPS, TK, TQ), BF16)
    row = lambda v: tuple(jnp.full((1, TQ), v, F32) for _ in range(N_MAPS))

    def two_tiles(i, carry):
        qi, t, qi_prev, t_prev, maxima, alphas = carry
        q1, t1 = next_tile(qi, t)
        score_matmuls(q1, t1, 1)
        maxima, alphas0 = softmax_step(t, 0, maxima)
        value_matmuls(qi_prev, t_prev, 1, alphas)
        q2, t2 = next_tile(q1, t1)
        score_matmuls(q2, t2, 0)
        maxima, alphas1 = softmax_step(t1, 1, maxima)
        value_matmuls(qi, t, 0, alphas0)
        return q2, t2, q1, t1, maxima, alphas1

    assert N_TILES % 2 == 0
    carry = lax.fori_loop(0, N_TILES // 2, two_tiles, (zero, zero, zero, zero, row(0.0), row(1.0)))
    _, _, qi_prev, t_prev, _, alphas = carry
    value_matmuls(qi_prev, t_prev, 1, alphas)

    def finish(qi, carry):
        qs = pl.multiple_of(qi * TQ, TQ)
        for h in range(N_DIFF_HEADS):
            a1, a2 = acc[qi, 2 * h], acc[qi, 2 * h + 1]
            l1, l2 = a1[DIFF_V_DIM:DIFF_V_DIM + 1, :], a2[DIFF_V_DIM:DIFF_V_DIM + 1, :]
            o = a1[:DIFF_V_DIM, :] * (1.0 / l1) - lam * (a2[:DIFF_V_DIM, :] * (1.0 / l2))
            ms = jnp.mean(o * o, axis=0, keepdims=True)
            y = o * lax.rsqrt(ms + RMS_EPS) * g_ref[...] * (1.0 - LAM_INIT)
            o_ref[pl.ds(qs, TQ), h * LANES:(h + 1) * LANES] = y.T.astype(BF16)
        return carry

    lax.fori_loop(0, N_QT, finish, 0)


def _diff_attention(slopes, proj3, vt, lq1, lk1, lq2, lk2, g_col):
    batch = proj3.shape[0]
    vec = pl.BlockSpec((1, HEAD_DIM), lambda b: (0, 0))
    return pl.pallas_call(
        functools.partial(_diff_attn_kernel, slopes=slopes),
        grid=(batch,),
        in_specs=[
            pl.BlockSpec((None, SEQ, DIFF_QK_WIDTH), lambda b: (b, 0, 0)),
            pl.BlockSpec((None, SEQ, DIFF_QK_WIDTH), lambda b: (b, 0, 1)),
            pl.BlockSpec((None, DIFF_V_WIDTH, SEQ), lambda b: (b, 0, 0)),
            vec, vec, vec, vec,
            pl.BlockSpec((DIFF_V_DIM, 1), lambda b: (0, 0)),
        ],
        out_specs=pl.BlockSpec((None, SEQ, DIFF_V_WIDTH), lambda b: (b, 0, 0)),
        out_shape=jax.ShapeDtypeStruct((batch, SEQ, DIFF_V_WIDTH), BF16),
        scratch_shapes=[
            pltpu.VMEM((N_MAPS, SEQ, LANES), BF16),
            pltpu.VMEM((N_MAPS, SEQ, LANES), BF16),
            pltpu.VMEM((N_DIFF_HEADS, DIFF_V_DIM + ONES_ROWS, SEQ), BF16),
            pltpu.VMEM((2, TK, TQ), F32),
            pltpu.VMEM((2, N_MAPS, TK, TQ), F32),
            pltpu.VMEM((2, N_MAPS, TK, TQ), BF16),
            pltpu.VMEM((N_QT, N_MAPS, DIFF_V_DIM + ONES_ROWS, TQ), F32),
        ],
        compiler_params=pltpu.CompilerParams(
            dimension_semantics=("arbitrary",), vmem_limit_bytes=VMEM_LIMIT_ATTN),
        name="diff_attention",
    )(proj3, proj3, vt, lq1, lk1, lq2, lk2, g_col)


N_PATTERNS = len(DIL_PATTERNS)
N_WIN = Q_BLOCK
N_BLOCKS = SEQ // Q_BLOCK
KPAD = Q_BLOCK


def _dil_attn_kernel(slopes_ref, q_ref, k_ref, v_ref, o_ref,
                     qf, kf, vf, q4, k4, v4, qa, qb, kt, va, vb, bias, o_acc, l_acc, m_acc, m_swapped):
    hp = pl.program_id(1)
    lane = lax.broadcasted_iota(jnp.int32, (1, LANES), 1)
    first_head = lane < HEAD_DIM

    qf[...] = q_ref[...].astype(F32)
    kf[...] = k_ref[...].astype(F32)
    vf[...] = v_ref[...].astype(F32)
    kt[0, :, 0:KPAD] = jnp.zeros((LANES, KPAD), BF16)
    va[0, 0:KPAD, :] = jnp.zeros((KPAD, LANES), BF16)
    vb[0, 0:KPAD, :] = jnp.zeros((KPAD, LANES), BF16)

    col = lax.broadcasted_iota(jnp.int32, (Q_BLOCK, 2 * Q_BLOCK), 1)
    dsub = Q_BLOCK + lax.broadcasted_iota(jnp.int32, (Q_BLOCK, 2 * Q_BLOCK), 0) - col
    valid = (dsub >= 0) & (dsub <= N_WIN)
    dsub_f = dsub.astype(F32)
    for p, (_, dil) in enumerate(DIL_PATTERNS):
        for hh in range(2):
            slope = slopes_ref[2 * hp + hh]
            b = jnp.where(valid, (-slope * (LOG2E * dil)) * dsub_f, NEG_BIG)
            bias[2 * p + hh] = b
            if p == 0:
                bias[2 * N_PATTERNS + hh] = jnp.where(col >= Q_BLOCK, b, NEG_BIG)

    def emit(p, block, q, k, v):
        dst = pl.multiple_of(block * Q_BLOCK, Q_BLOCK)
        qa[p, pl.ds(dst, Q_BLOCK), :] = jnp.where(first_head, q, 0.0).astype(BF16)
        qb[p, pl.ds(dst, Q_BLOCK), :] = jnp.where(first_head, 0.0, q).astype(BF16)
        kt[p, :, pl.ds(KPAD + dst, Q_BLOCK)] = k.T.astype(BF16)
        va[p, pl.ds(KPAD + dst, Q_BLOCK), :] = jnp.where(first_head, v, 1.0).astype(BF16)
        vb[p, pl.ds(KPAD + dst, Q_BLOCK), :] = jnp.where(first_head, 1.0, v).astype(BF16)

    step = DIL_PATTERNS[1][1]
    assert DIL_PATTERNS[2][1] == step * step
    run4 = SEQ // step

    def first_level(r4, carry):
        for n in range(step):
            block = r4 * step + n
            rows = pl.ds(pl.multiple_of(block * Q_BLOCK, Q_BLOCK), Q_BLOCK)
            emit(0, block, qf[rows, :], kf[rows, :], vf[rows, :])
            src = pl.ds(n * (Q_BLOCK * step) + r4, Q_BLOCK, stride=step)
            q, k, v = qf[src, :], kf[src, :], vf[src, :]
            q4[rows, :] = q
            k4[rows, :] = k
            v4[rows, :] = v
            emit(1, block, q, k, v)
        return carry

    def second_level(a, carry):
        for r4 in range(step):
            src = pl.ds(r4 * run4 + a, Q_BLOCK, stride=step)
            emit(2, a * step + r4, q4[src, :], k4[src, :], v4[src, :])
        return carry

    assert N_BLOCKS == step * step
    lax.fori_loop(0, step, first_level, 0)
    lax.fori_loop(0, step, second_level, 0)

    def attend(p, dil, q_row, with_prev, nat_start, bias_idx):
        n_keys = 2 * Q_BLOCK if with_prev else Q_BLOCK
        key_row = q_row + (KPAD - Q_BLOCK if with_prev else KPAD)
        if not isinstance(q_row, int):
            q_row = pl.multiple_of(q_row, Q_BLOCK)
            key_row = pl.multiple_of(key_row, Q_BLOCK)
        k_t = kt[p, :, pl.ds(key_row, n_keys)]
        outs = []
        for hh, (q_src, v_src) in enumerate(((qa, va), (qb, vb))):
            s = jnp.dot(q_src[p, pl.ds(q_row, Q_BLOCK), :], k_t, preferred_element_type=F32)
            if with_prev:
                s = s + bias[bias_idx[hh]]
            else:
                s = s + bias[bias_idx[hh], :, Q_BLOCK:]
            m = jnp.max(s, axis=1, keepdims=True)
            e = jnp.exp2(s - m).astype(BF16)
            ov = jnp.dot(e, v_src[p, pl.ds(key_row, n_keys), :], preferred_element_type=F32)
            outs.append((ov, m))
        if dil == 1:
            nat = pl.ds(nat_start, Q_BLOCK)
        else:
            nat = pl.ds(nat_start, Q_BLOCK, stride=dil)
        o_acc[p, nat, :] = jnp.where(first_head, outs[0][0], outs[1][0])
        l_acc[p, nat, :] = jnp.where(first_head, outs[1][0], outs[0][0])
        m_acc[p, nat, :] = jnp.where(first_head, outs[0][1], outs[1][1])
        m_swapped[p, nat, :] = jnp.where(first_head, outs[1][1], outs[0][1])

    for n in range(N_BLOCKS):
        first = 2 * N_PATTERNS if n == 0 else 0
        attend(0, 1, n * Q_BLOCK, True, n * Q_BLOCK, (first, first + 1))

    dil4 = DIL_PATTERNS[1][1]
    run = SEQ // dil4
    for n in range(run // Q_BLOCK):
        for r in range(dil4):
            attend(1, dil4, r * run + n * Q_BLOCK, n > 0, n * (Q_BLOCK * dil4) + r, (2, 3))

    dil16 = DIL_PATTERNS[2][1]
    for r in range(dil16):
        attend(2, dil16, r * Q_BLOCK, False, r, (4, 5))

    rows_per_step = 256

    def blend(i, carry):
        sl = pl.ds(pl.multiple_of(i * rows_per_step, rows_per_step), rows_per_step)
        add = lambda a, b: a + b

        def weighted(m_ref, x_ref):
            ms = [m_ref[p, sl, :] for p in range(N_PATTERNS)]
            mx = functools.reduce(jnp.maximum, ms)
            return functools.reduce(add, [jnp.exp2(ms[p] - mx) * x_ref[p, sl, :] for p in range(N_PATTERNS)])

        num = weighted(m_acc, o_acc)
        den = pltpu.roll(weighted(m_swapped, l_acc), HEAD_DIM, axis=1)
        o_ref[sl, :] = (num * (1.0 / den)).astype(BF16)
        return carry

    lax.fori_loop(0, SEQ // rows_per_step, blend, 0)


def _dilated_attention(slopes, proj3):
    batch = proj3.shape[0]
    q_off = (2 * DIFF_QK_WIDTH + DIFF_V_WIDTH) // LANES
    k_off = q_off + DIL_WIDTH // LANES
    v_off = k_off + DIL_WIDTH // LANES
    n_pairs = DIL_WIDTH // LANES
    return pl.pallas_call(
        _dil_attn_kernel,
        grid=(batch, n_pairs),
        in_specs=[
            pl.BlockSpec(memory_space=pltpu.SMEM),
            pl.BlockSpec((None, SEQ, LANES), lambda b, h: (b, 0, q_off + h)),
            pl.BlockSpec((None, SEQ, LANES), lambda b, h: (b, 0, k_off + h)),
            pl.BlockSpec((None, SEQ, LANES), lambda b, h: (b, 0, v_off + h)),
        ],
        out_specs=pl.BlockSpec((None, SEQ, LANES), lambda b, h: (b, 0, h)),
        out_shape=jax.ShapeDtypeStruct((batch, SEQ, DIL_WIDTH), BF16),
        scratch_shapes=[
            pltpu.VMEM((SEQ, LANES), F32),
            pltpu.VMEM((SEQ, LANES), F32),
            pltpu.VMEM((SEQ, LANES), F32),
            pltpu.VMEM((SEQ, LANES), F32),
            pltpu.VMEM((SEQ, LANES), F32),
            pltpu.VMEM((SEQ, LANES), F32),
            pltpu.VMEM((N_PATTERNS, SEQ, LANES), BF16),
            pltpu.VMEM((N_PATTERNS, SEQ, LANES), BF16),
            pltpu.VMEM((N_PATTERNS, LANES, KPAD + SEQ), BF16),
            pltpu.VMEM((N_PATTERNS, KPAD + SEQ, LANES), BF16),
            pltpu.VMEM((N_PATTERNS, KPAD + SEQ, LANES), BF16),
            pltpu.VMEM((2 * N_PATTERNS + 2, Q_BLOCK, 2 * Q_BLOCK), F32),
            pltpu.VMEM((N_PATTERNS, SEQ, LANES), F32),
            pltpu.VMEM((N_PATTERNS, SEQ, LANES), F32),
            pltpu.VMEM((N_PATTERNS, SEQ, LANES), F32),
            pltpu.VMEM((N_PATTERNS, SEQ, LANES), F32),
        ],
        compiler_params=pltpu.CompilerParams(
            dimension_semantics=("arbitrary", "arbitrary"), vmem_limit_bytes=VMEM_LIMIT_ATTN),
        name="dilated_attention",
    )(slopes, proj3, proj3, proj3)


def _alibi_slopes():
    all_s = 2.0 ** (-8.0 * np.arange(1, N_ALIBI_HEADS + 1, dtype=np.float32) / N_ALIBI_HEADS)
    diff_idx = np.arange(0, N_ALIBI_HEADS, N_ALIBI_HEADS // N_DIFF_HEADS)
    dil_idx = np.setdiff1d(np.arange(N_ALIBI_HEADS), diff_idx)
    return tuple(float(s) for s in all_s[diff_idx]), jnp.asarray(all_s[dil_idx], F32)


def kernel(x, ffn1_pre_g, ffn1_w_gate, ffn1_w_up, ffn1_w_down, ffn1_post_g, mix_pre_g, w_in, lambda_q1, lambda_k1, lambda_q2, lambda_k2, diff_subln_g, w_out, mix_post_g, ffn2_pre_g, ffn2_w_gate, ffn2_w_up, ffn2_w_down, ffn2_post_g):
    batch, seq, d_model = x.shape
    assert (seq, d_model) == (SEQ, D_MODEL) and ffn1_pre_g.shape[0] == 1
    slopes_d, slopes_l = _alibi_slopes()

    scale = HEAD_DIM ** -0.5 * LOG2E
    col_scale = np.ones((IN_PROJ_WIDTH,), np.float32)
    col_scale[:DIFF_QK_WIDTH] = scale
    q_l0 = 2 * DIFF_QK_WIDTH + DIFF_V_WIDTH
    col_scale[q_l0:q_l0 + DIL_WIDTH] = scale

    l = 0
    row = lambda g: g[l].reshape(1, -1)
    x2d = x.reshape(batch * seq, d_model)
    x1, proj, vt = _ffn1_inproj(
        x2d, row(ffn1_pre_g), ffn1_w_gate[l].astype(BF16), ffn1_w_up[l].astype(BF16),
        ffn1_w_down[l].astype(BF16), row(ffn1_post_g), row(mix_pre_g),
        (w_in[l] * col_scale).astype(BF16), batch)
    proj3 = proj.reshape(batch, seq, IN_PROJ_WIDTH)
    o_d = _diff_attention(slopes_d, proj3, vt, row(lambda_q1), row(lambda_k1), row(lambda_q2),
                          row(lambda_k2), diff_subln_g[l].reshape(DIFF_V_DIM, 1))
    o_l = _dilated_attention(slopes_l, proj3)
    w_o = w_out[l].astype(BF16)
    out = _outproj_ffn2(
        o_d.reshape(batch * seq, DIFF_V_WIDTH), o_l.reshape(batch * seq, DIL_WIDTH), x1,
        w_o[:DIFF_V_WIDTH], w_o[DIFF_V_WIDTH:], row(mix_post_g), row(ffn2_pre_g),
        ffn2_w_gate[l].astype(BF16), ffn2_w_up[l].astype(BF16), ffn2_w_down[l].astype(BF16),
        row(ffn2_post_g))
    return out.reshape(batch, seq, d_model)
```

```python
import functools
import math

import numpy as np
import jax
import jax.numpy as jnp
from jax import lax
from jax.experimental import pallas as pl
from jax.experimental.pallas import tpu as pltpu

F32 = jnp.float32
BF16 = jnp.bfloat16

D_MODEL = 1024
SEQ = 2048
HEAD_DIM = 64
N_DIFF_HEADS = 4
DIFF_V_DIM = 2 * HEAD_DIM
N_DIL_HEADS = 8
DIL_PATTERNS = ((128, 1), (512, 4), (2048, 16))
Q_BLOCK = 128
D_FF = 2816
RMS_EPS = 1e-6

DIFF_QK_WIDTH = N_DIFF_HEADS * 2 * HEAD_DIM
DIFF_V_WIDTH = N_DIFF_HEADS * DIFF_V_DIM
DIL_WIDTH = N_DIL_HEADS * HEAD_DIM
IN_PROJ_WIDTH = 2 * DIFF_QK_WIDTH + DIFF_V_WIDTH + 3 * DIL_WIDTH
N_ALIBI_HEADS = N_DIFF_HEADS + N_DIL_HEADS
LAM_INIT = 0.8 - 0.6 * math.exp(-0.3 * 0)

LANES = 128
NEG_BIG = -1e30
VMEM_LIMIT_FFN = 56 * 1024 * 1024
VMEM_LIMIT_ATTN = 52 * 1024 * 1024

TM = 512
TM_SUB = 256
TQ = 256
TK = 256


def _rms(x, g):
    ms = jnp.mean(x * x, axis=-1, keepdims=True)
    return x * lax.rsqrt(ms + RMS_EPS) * g


ROW_GROUPS = [slice(r * TM_SUB, (r + 1) * TM_SUB) for r in range(TM // TM_SUB)]


def _swiglu_half_step(xs, pre_g, wg_ref, wu_ref, wd_ref, post_g):
    xb = [_rms(x, pre_g).astype(BF16) for x in xs]
    gu = [(jnp.dot(b, wg_ref[...], preferred_element_type=F32),
           jnp.dot(b, wu_ref[...], preferred_element_type=F32)) for b in xb]
    ys = []
    for g, u in gu:
        h = (g * (1.0 / (1.0 + jnp.exp(-g))) * u).astype(BF16)
        ys.append(jnp.dot(h, wd_ref[...], preferred_element_type=F32))
    return [x + 0.5 * _rms(y, post_g) for x, y in zip(xs, ys)]


def _ffn1_inproj_kernel(x_ref, pre_g, wg, wu, wd, post_g, mix_g, w_in, x1_ref, proj_ref, vt_ref):
    x1s = _swiglu_half_step([x_ref[rows, :] for rows in ROW_GROUPS], pre_g[...], wg, wu, wd, post_g[...])
    for rows, x1 in zip(ROW_GROUPS, x1s):
        x1_ref[rows, :] = x1
        hb = _rms(x1, mix_g[...]).astype(BF16)
        proj = jnp.dot(hb, w_in[...], preferred_element_type=F32)
        proj_ref[rows, :] = proj.astype(BF16)
        v_d = proj[:, 2 * DIFF_QK_WIDTH:2 * DIFF_QK_WIDTH + DIFF_V_WIDTH]
        vt_ref[:, rows] = v_d.T.astype(BF16)


def _const_spec(shape):
    nd = len(shape)
    return pl.BlockSpec(shape, lambda *_: (0,) * nd, pipeline_mode=pl.Buffered(1))


def _ffn1_inproj(x2d, pre_g, wg, wu, wd, post_g, mix_g, w_in, batch):
    t = x2d.shape[0]
    tiles_per_seq = SEQ // TM
    row = lambda i: (i, 0)
    return pl.pallas_call(
        _ffn1_inproj_kernel,
        grid=(t // TM,),
        in_specs=[
            pl.BlockSpec((TM, D_MODEL), row),
            _const_spec((1, D_MODEL)),
            _const_spec((D_MODEL, D_FF)),
            _const_spec((D_MODEL, D_FF)),
            _const_spec((D_FF, D_MODEL)),
            _const_spec((1, D_MODEL)),
            _const_spec((1, D_MODEL)),
            _const_spec((D_MODEL, IN_PROJ_WIDTH)),
        ],
        out_specs=[
            pl.BlockSpec((TM, D_MODEL), row),
            pl.BlockSpec((TM, IN_PROJ_WIDTH), row),
            pl.BlockSpec((None, DIFF_V_WIDTH, TM), lambda i: (i // tiles_per_seq, 0, i % tiles_per_seq)),
        ],
        out_shape=[
            jax.ShapeDtypeStruct((t, D_MODEL), F32),
            jax.ShapeDtypeStruct((t, IN_PROJ_WIDTH), BF16),
            jax.ShapeDtypeStruct((batch, DIFF_V_WIDTH, SEQ), BF16),
        ],
        compiler_params=pltpu.CompilerParams(
            dimension_semantics=("arbitrary",), vmem_limit_bytes=VMEM_LIMIT_FFN),
        name="ffn1_inproj",
    )(x2d, pre_g, wg, wu, wd, post_g, mix_g, w_in)


def _outproj_ffn2_kernel(od_ref, ol_ref, x1_ref, wo_d, wo_l, mix_post_g, pre_g, wg, wu, wd, post_g, out_ref):
    x2s = []
    for rows in ROW_GROUPS:
        h = (jnp.dot(od_ref[rows, :], wo_d[...], preferred_element_type=F32)
             + jnp.dot(ol_ref[rows, :], wo_l[...], preferred_element_type=F32))
        x2s.append(x1_ref[rows, :] + _rms(h, mix_post_g[...]))
    outs = _swiglu_half_step(x2s, pre_g[...], wg, wu, wd, post_g[...])
    for rows, out in zip(ROW_GROUPS, outs):
        out_ref[rows, :] = out


def _outproj_ffn2(od, ol, x1, wo_d, wo_l, mix_post_g, pre_g, wg, wu, wd, post_g):
    t = x1.shape[0]
    row = lambda i: (i, 0)
    return pl.pallas_call(
        _outproj_ffn2_kernel,
        grid=(t // TM,),
        in_specs=[
            pl.BlockSpec((TM, DIFF_V_WIDTH), row),
            pl.BlockSpec((TM, DIL_WIDTH), row),
            pl.BlockSpec((TM, D_MODEL), row),
            _const_spec((DIFF_V_WIDTH, D_MODEL)),
            _const_spec((DIL_WIDTH, D_MODEL)),
            _const_spec((1, D_MODEL)),
            _const_spec((1, D_MODEL)),
            _const_spec((D_MODEL, D_FF)),
            _const_spec((D_MODEL, D_FF)),
            _const_spec((D_FF, D_MODEL)),
            _const_spec((1, D_MODEL)),
        ],
        out_specs=pl.BlockSpec((TM, D_MODEL), row),
        out_shape=jax.ShapeDtypeStruct((t, D_MODEL), F32),
        compiler_params=pltpu.CompilerParams(
            dimension_semantics=("arbitrary",), vmem_limit_bytes=VMEM_LIMIT_FFN),
        name="outproj_ffn2",
    )(od, ol, x1, wo_d, wo_l, mix_post_g, pre_g, wg, wu, wd, post_g)


_NT = (((1,), (1,)), ((), ()))
N_MAPS = 2 * N_DIFF_HEADS
ONES_ROWS = 16
SLOPE_TERMS = 3
N_QT = SEQ // TQ
N_TILES = N_QT * (N_QT + 1) // 2
LOG2E = math.log2(math.e)


def _bf16_terms(x, n_terms):
    terms, rest = [], np.float32(x)
    for _ in range(n_terms):
        t = np.float32(np.asarray(rest, dtype=BF16))
        terms.append(float(t))
        rest = np.float32(rest - t)
    return terms


def _diff_attn_kernel(q_ref, k_ref, vt_ref, lq1, lk1, lq2, lk2, g_ref, o_ref,
                      kp, qp, vta, masks, s_buf, p_buf, acc, *, slopes):
    lam = (jnp.exp(jnp.sum(lq1[...] * lk1[...], axis=-1, keepdims=True))
           - jnp.exp(jnp.sum(lq2[...] * lk2[...], axis=-1, keepdims=True)) + LAM_INIT)
    lane = lax.broadcasted_iota(jnp.int32, (1, LANES), 1)
    first_map = lane < HEAD_DIM

    def in_lanes(base, values):
        row = jnp.zeros((1, LANES), F32)
        for i, v in enumerate(values):
            row = jnp.where(lane == base + i, v, row)
        return row.astype(BF16)

    slope_rows = [(in_lanes(HEAD_DIM, _bf16_terms(s * LOG2E, SLOPE_TERMS)),
                   in_lanes(0, _bf16_terms(s * LOG2E, SLOPE_TERMS))) for s in slopes]
    key_idx = lax.broadcasted_iota(jnp.int32, (TK, LANES), 0).astype(F32)
    idx_first = jnp.where((lane >= HEAD_DIM) & (lane < HEAD_DIM + SLOPE_TERMS), key_idx, 0.0).astype(BF16)
    idx_second = jnp.where(lane < SLOPE_TERMS, key_idx, 0.0).astype(BF16)

    def augment(i, carry):
        rows = pl.ds(pl.multiple_of(i * TK, TK), TK)
        for h in range(N_DIFF_HEADS):
            k = k_ref[rows, h * LANES:(h + 1) * LANES]
            kp[2 * h, rows, :] = jnp.where(first_map, k, idx_first)
            kp[2 * h + 1, rows, :] = jnp.where(first_map, idx_second, k)
            q = q_ref[rows, h * LANES:(h + 1) * LANES]
            qp[2 * h, rows, :] = jnp.where(first_map, q, slope_rows[h][0])
            qp[2 * h + 1, rows, :] = jnp.where(first_map, slope_rows[h][1], q)
        return carry

    lax.fori_loop(0, SEQ // TK, augment, 0)
    for h in range(N_DIFF_HEADS):
        vta[h, 0:DIFF_V_DIM, :] = vt_ref[h * DIFF_V_DIM:(h + 1) * DIFF_V_DIM, :]
        vta[h, DIFF_V_DIM:, :] = jnp.ones((ONES_ROWS, SEQ), BF16)
    causal = (lax.broadcasted_iota(jnp.int32, (TK, TQ), 1)
              >= lax.broadcasted_iota(jnp.int32, (TK, TQ), 0))
    masks[0] = jnp.zeros((TK, TQ), F32)
    masks[1] = jnp.where(causal, 0.0, NEG_BIG)

    def clear(qi, carry):
        acc[qi] = jnp.zeros((N_MAPS, DIFF_V_DIM + ONES_ROWS, TQ), F32)
        return carry

    lax.fori_loop(0, N_QT, clear, 0)

    def score_matmuls(qi, t, slot):
        qs = pl.multiple_of(qi * TQ, TQ)
        ks = pl.multiple_of((qi - t) * TK, TK)
        mask = masks[(t == 0).astype(jnp.int32)]
        tile_max = []
        for c in range(N_MAPS):
            s = lax.dot_general(kp[c, pl.ds(ks, TK), :], qp[c, pl.ds(qs, TQ), :], _NT,
                                preferred_element_type=F32) + mask
            s_buf[slot, c] = s
            tile_max.append(jnp.max(s, axis=0, keepdims=True))
        return tuple(tile_max)

    def softmax_step(t, slot, tile_max, maxima):
        on_diagonal = t == 0
        dist = jnp.asarray(t * TK, F32)
        new_m, alphas = [], []
        for c in range(N_MAPS):
            coff = (slopes[c // 2] * LOG2E) * dist
            m = jnp.where(on_diagonal, NEG_BIG, maxima[c])
            n = jnp.maximum(m, tile_max[c] - coff)
            new_m.append(n)
            alphas.append(jnp.exp2(m - n))
            p_buf[slot, c] = jnp.exp2(s_buf[slot, c] - (n + coff)).astype(BF16)
        return tuple(new_m), tuple(alphas)

    def value_matmuls(qi, t, slot, alphas):
        ks = pl.multiple_of((qi - t) * TK, TK)
        for c in range(N_MAPS):
            pv = jnp.dot(vta[c // 2, :, pl.ds(ks, TK)], p_buf[slot, c], preferred_element_type=F32)
            acc[qi, c] = alphas[c] * acc[qi, c] + pv

    def next_tile(qi, t):
        last = t == qi
        nq = jnp.where(last, qi + 1, qi)
        nt = jnp.where(last, 0, t + 1)
        return jnp.minimum(nq, N_QT - 1), nt

    zero = jnp.int32(0)
    first_max = score_matmuls(zero, zero, 0)
    p_buf[1] = jnp.zeros((N_MAPS, TK, TQ), BF16)
    row = lambda v: tuple(jnp.full((1, TQ), v, F32) for _ in range(N_MAPS))

    def two_tiles(i, carry):
        qi, t, qi_prev, t_prev, tile_max, maxima, alphas = carry
        q1, t1 = next_tile(qi, t)
        tile_max1 = score_matmuls(q1, t1, 1)
        maxima, alphas0 = softmax_step(t, 0, tile_max, maxima)
        value_matmuls(qi_prev, t_prev, 1, alphas)
        q2, t2 = next_tile(q1, t1)
        tile_max2 = score_matmuls(q2, t2, 0)
        maxima, alphas1 = softmax_step(t1, 1, tile_max1, maxima)
        value_matmuls(qi, t, 0, alphas0)
        return q2, t2, q1, t1, tile_max2, maxima, alphas1

    assert N_TILES % 2 == 0
    carry = lax.fori_loop(0, N_TILES // 2, two_tiles,
                          (zero, zero, zero, zero, first_max, row(0.0), row(1.0)))
    _, _, qi_prev, t_prev, _, _, alphas = carry
    value_matmuls(qi_prev, t_prev, 1, alphas)

    def finish(qi, carry):
        qs = pl.multiple_of(qi * TQ, TQ)
        for h in range(N_DIFF_HEADS):
            a1, a2 = acc[qi, 2 * h], acc[qi, 2 * h + 1]
            l1, l2 = a1[DIFF_V_DIM:DIFF_V_DIM + 1, :], a2[DIFF_V_DIM:DIFF_V_DIM + 1, :]
            o = a1[:DIFF_V_DIM, :] * (1.0 / l1) - lam * (a2[:DIFF_V_DIM, :] * (1.0 / l2))
            ms = jnp.mean(o * o, axis=0, keepdims=True)
            y = o * lax.rsqrt(ms + RMS_EPS) * g_ref[...] * (1.0 - LAM_INIT)
            o_ref[pl.ds(qs, TQ), h * LANES:(h + 1) * LANES] = y.T.astype(BF16)
        return carry

    lax.fori_loop(0, N_QT, finish, 0)


def _diff_attention(slopes, proj3, vt, lq1, lk1, lq2, lk2, g_col):
    batch = proj3.shape[0]
    vec = pl.BlockSpec((1, HEAD_DIM), lambda b: (0, 0))
    return pl.pallas_call(
        functools.partial(_diff_attn_kernel, slopes=slopes),
        grid=(batch,),
        in_specs=[
            pl.BlockSpec((None, SEQ, DIFF_QK_WIDTH), lambda b: (b, 0, 0)),
            pl.BlockSpec((None, SEQ, DIFF_QK_WIDTH), lambda b: (b, 0, 1)),
            pl.BlockSpec((None, DIFF_V_WIDTH, SEQ), lambda b: (b, 0, 0)),
            vec, vec, vec, vec,
            pl.BlockSpec((DIFF_V_DIM, 1), lambda b: (0, 0)),
        ],
        out_specs=pl.BlockSpec((None, SEQ, DIFF_V_WIDTH), lambda b: (b, 0, 0)),
        out_shape=jax.ShapeDtypeStruct((batch, SEQ, DIFF_V_WIDTH), BF16),
        scratch_shapes=[
            pltpu.VMEM((N_MAPS, SEQ, LANES), BF16),
            pltpu.VMEM((N_MAPS, SEQ, LANES), BF16),
            pltpu.VMEM((N_DIFF_HEADS, DIFF_V_DIM + ONES_ROWS, SEQ), BF16),
            pltpu.VMEM((2, TK, TQ), F32),
            pltpu.VMEM((2, N_MAPS, TK, TQ), F32),
            pltpu.VMEM((2, N_MAPS, TK, TQ), BF16),
            pltpu.VMEM((N_QT, N_MAPS, DIFF_V_DIM + ONES_ROWS, TQ), F32),
        ],
        compiler_params=pltpu.CompilerParams(
            dimension_semantics=("arbitrary",), vmem_limit_bytes=VMEM_LIMIT_ATTN),
        name="diff_attention",
    )(proj3, proj3, vt, lq1, lk1, lq2, lk2, g_col)


N_PATTERNS = len(DIL_PATTERNS)
N_WIN = Q_BLOCK
N_BLOCKS = SEQ // Q_BLOCK
KPAD = Q_BLOCK


def _dil_attn_kernel(slopes_ref, q_ref, k_ref, v_ref, o_ref,
                     qf, kf, vf, q4, k4, v4, qa, qb, kt, va, vb, bias, o_acc, l_acc, m_acc, m_swapped):
    hp = pl.program_id(1)
    lane = lax.broadcasted_iota(jnp.int32, (1, LANES), 1)
    first_head = lane < HEAD_DIM

    qf[...] = q_ref[...].astype(F32)
    kf[...] = k_ref[...].astype(F32)
    vf[...] = v_ref[...].astype(F32)
    kt[0, :, 0:KPAD] = jnp.zeros((LANES, KPAD), BF16)
    va[0, 0:KPAD, :] = jnp.zeros((KPAD, LANES), BF16)
    vb[0, 0:KPAD, :] = jnp.zeros((KPAD, LANES), BF16)

    col = lax.broadcasted_iota(jnp.int32, (Q_BLOCK, 2 * Q_BLOCK), 1)
    dsub = Q_BLOCK + lax.broadcasted_iota(jnp.int32, (Q_BLOCK, 2 * Q_BLOCK), 0) - col
    valid = (dsub >= 0) & (dsub <= N_WIN)
    dsub_f = dsub.astype(F32)
    for p, (_, dil) in enumerate(DIL_PATTERNS):
        for hh in range(2):
            slope = slopes_ref[2 * hp + hh]
            b = jnp.where(valid, (-slope * (LOG2E * dil)) * dsub_f, NEG_BIG)
            bias[2 * p + hh] = b
            if p == 0:
                bias[2 * N_PATTERNS + hh] = jnp.where(col >= Q_BLOCK, b, NEG_BIG)

    def emit(p, block, q, k, v):
        dst = pl.multiple_of(block * Q_BLOCK, Q_BLOCK)
        qa[p, pl.ds(dst, Q_BLOCK), :] = jnp.where(first_head, q, 0.0).astype(BF16)
        qb[p, pl.ds(dst, Q_BLOCK), :] = jnp.where(first_head, 0.0, q).astype(BF16)
        kt[p, :, pl.ds(KPAD + dst, Q_BLOCK)] = k.T.astype(BF16)
        va[p, pl.ds(KPAD + dst, Q_BLOCK), :] = jnp.where(first_head, v, 1.0).astype(BF16)
        vb[p, pl.ds(KPAD + dst, Q_BLOCK), :] = jnp.where(first_head, 1.0, v).astype(BF16)

    step = DIL_PATTERNS[1][1]
    assert DIL_PATTERNS[2][1] == step * step
    run4 = SEQ // step

    def first_level(r4, carry):
        for n in range(step):
            block = r4 * step + n
            rows = pl.ds(pl.multiple_of(block * Q_BLOCK, Q_BLOCK), Q_BLOCK)
            emit(0, block, qf[rows, :], kf[rows, :], vf[rows, :])
            src = pl.ds(n * (Q_BLOCK * step) + r4, Q_BLOCK, stride=step)
            q, k, v = qf[src, :], kf[src, :], vf[src, :]
            q4[rows, :] = q
            k4[rows, :] = k
            v4[rows, :] = v
            emit(1, block, q, k, v)
        return carry

    def second_level(a, carry):
        for r4 in range(step):
            src = pl.ds(r4 * run4 + a, Q_BLOCK, stride=step)
            emit(2, a * step + r4, q4[src, :], k4[src, :], v4[src, :])
        return carry

    assert N_BLOCKS == step * step
    lax.fori_loop(0, step, first_level, 0)
    lax.fori_loop(0, step, second_level, 0)

    def attend(p, dil, q_row, with_prev, nat_start, bias_idx):
        n_keys = 2 * Q_BLOCK if with_prev else Q_BLOCK
        key_row = q_row + (KPAD - Q_BLOCK if with_prev else KPAD)
        if not isinstance(q_row, int):
            q_row = pl.multiple_of(q_row, Q_BLOCK)
            key_row = pl.multiple_of(key_row, Q_BLOCK)
        k_t = kt[p, :, pl.ds(key_row, n_keys)]
        outs = []
        for hh, (q_src, v_src) in enumerate(((qa, va), (qb, vb))):
            s = jnp.dot(q_src[p, pl.ds(q_row, Q_BLOCK), :], k_t, preferred_element_type=F32)
            if with_prev:
                s = s + bias[bias_idx[hh]]
            else:
                s = s + bias[bias_idx[hh], :, Q_BLOCK:]
            m = jnp.max(s, axis=1, keepdims=True)
            e = jnp.exp2(s - m).astype(BF16)
            ov = jnp.dot(e, v_src[p, pl.ds(key_row, n_keys), :], preferred_element_type=F32)
            outs.append((ov, m))
        if dil == 1:
            nat = pl.ds(nat_start, Q_BLOCK)
        else:
            nat = pl.ds(nat_start, Q_BLOCK, stride=dil)
        o_acc[p, nat, :] = jnp.where(first_head, outs[0][0], outs[1][0])
        l_acc[p, nat, :] = jnp.where(first_head, outs[1][0], outs[0][0])
        m_acc[p, nat, :] = jnp.where(first_head, outs[0][1], outs[1][1])
        m_swapped[p, nat, :] = jnp.where(first_head, outs[1][1], outs[0][1])

    for n in range(N_BLOCKS):
        first = 2 * N_PATTERNS if n == 0 else 0
        attend(0, 1, n * Q_BLOCK, True, n * Q_BLOCK, (first, first + 1))

    dil4 = DIL_PATTERNS[1][1]
    run = SEQ // dil4
    for n in range(run // Q_BLOCK):
        for r in range(dil4):
            attend(1, dil4, r * run + n * Q_BLOCK, n > 0, n * (Q_BLOCK * dil4) + r, (2, 3))

    dil16 = DIL_PATTERNS[2][1]
    for r in range(dil16):
        attend(2, dil16, r * Q_BLOCK, False, r, (4, 5))

    rows_per_step = 256

    def blend(i, carry):
        sl = pl.ds(pl.multiple_of(i * rows_per_step, rows_per_step), rows_per_step)
        add = lambda a, b: a + b

        def weighted(m_ref, x_ref):
            ms = [m_ref[p, sl, :] for p in range(N_PATTERNS)]
            mx = functools.reduce(jnp.maximum, ms)
            return functools.reduce(add, [jnp.exp2(ms[p] - mx) * x_ref[p, sl, :] for p in range(N_PATTERNS)])

        num = weighted(m_acc, o_acc)
        den = pltpu.roll(weighted(m_swapped, l_acc), HEAD_DIM, axis=1)
        o_ref[sl, :] = (num * (1.0 / den)).astype(BF16)
        return carry

    lax.fori_loop(0, SEQ // rows_per_step, blend, 0)


def _dilated_attention(slopes, proj3):
    batch = proj3.shape[0]
    q_off = (2 * DIFF_QK_WIDTH + DIFF_V_WIDTH) // LANES
    k_off = q_off + DIL_WIDTH // LANES
    v_off = k_off + DIL_WIDTH // LANES
    n_pairs = DIL_WIDTH // LANES
    return pl.pallas_call(
        _dil_attn_kernel,
        grid=(batch, n_pairs),
        in_specs=[
            pl.BlockSpec(memory_space=pltpu.SMEM),
            pl.BlockSpec((None, SEQ, LANES), lambda b, h: (b, 0, q_off + h)),
            pl.BlockSpec((None, SEQ, LANES), lambda b, h: (b, 0, k_off + h)),
            pl.BlockSpec((None, SEQ, LANES), lambda b, h: (b, 0, v_off + h)),
        ],
        out_specs=pl.BlockSpec((None, SEQ, LANES), lambda b, h: (b, 0, h)),
        out_shape=jax.ShapeDtypeStruct((batch, SEQ, DIL_WIDTH), BF16),
        scratch_shapes=[
            pltpu.VMEM((SEQ, LANES), F32),
            pltpu.VMEM((SEQ, LANES), F32),
            pltpu.VMEM((SEQ, LANES), F32),
            pltpu.VMEM((SEQ, LANES), F32),
            pltpu.VMEM((SEQ, LANES), F32),
            pltpu.VMEM((SEQ, LANES), F32),
            pltpu.VMEM((N_PATTERNS, SEQ, LANES), BF16),
            pltpu.VMEM((N_PATTERNS, SEQ, LANES), BF16),
            pltpu.VMEM((N_PATTERNS, LANES, KPAD + SEQ), BF16),
            pltpu.VMEM((N_PATTERNS, KPAD + SEQ, LANES), BF16),
            pltpu.VMEM((N_PATTERNS, KPAD + SEQ, LANES), BF16),
            pltpu.VMEM((2 * N_PATTERNS + 2, Q_BLOCK, 2 * Q_BLOCK), F32),
            pltpu.VMEM((N_PATTERNS, SEQ, LANES), F32),
            pltpu.VMEM((N_PATTERNS, SEQ, LANES), F32),
            pltpu.VMEM((N_PATTERNS, SEQ, LANES), F32),
            pltpu.VMEM((N_PATTERNS, SEQ, LANES), F32),
        ],
        compiler_params=pltpu.CompilerParams(
            dimension_semantics=("arbitrary", "arbitrary"), vmem_limit_bytes=VMEM_LIMIT_ATTN),
        name="dilated_attention",
    )(slopes, proj3, proj3, proj3)


def _alibi_slopes():
    all_s = 2.0 ** (-8.0 * np.arange(1, N_ALIBI_HEADS + 1, dtype=np.float32) / N_ALIBI_HEADS)
    diff_idx = np.arange(0, N_ALIBI_HEADS, N_ALIBI_HEADS // N_DIFF_HEADS)
    dil_idx = np.setdiff1d(np.arange(N_ALIBI_HEADS), diff_idx)
    return tuple(float(s) for s in all_s[diff_idx]), jnp.asarray(all_s[dil_idx], F32)


def kernel(x, ffn1_pre_g, ffn1_w_gate, ffn1_w_up, ffn1_w_down, ffn1_post_g, mix_pre_g, w_in, lambda_q1, lambda_k1, lambda_q2, lambda_k2, diff_subln_g, w_out, mix_post_g, ffn2_pre_g, ffn2_w_gate, ffn2_w_up, ffn2_w_down, ffn2_post_g):
    batch, seq, d_model = x.shape
    assert (seq, d_model) == (SEQ, D_MODEL) and ffn1_pre_g.shape[0] == 1
    slopes_d, slopes_l = _alibi_slopes()

    scale = HEAD_DIM ** -0.5 * LOG2E
    col_scale = np.ones((IN_PROJ_WIDTH,), np.float32)
    col_scale[:DIFF_QK_WIDTH] = scale
    q_l0 = 2 * DIFF_QK_WIDTH + DIFF_V_WIDTH
    col_scale[q_l0:q_l0 + DIL_WIDTH] = scale

    l = 0
    row = lambda g: g[l].reshape(1, -1)
    x2d = x.reshape(batch * seq, d_model)
    x1, proj, vt = _ffn1_inproj(
        x2d, row(ffn1_pre_g), ffn1_w_gate[l].astype(BF16), ffn1_w_up[l].astype(BF16),
        ffn1_w_down[l].astype(BF16), row(ffn1_post_g), row(mix_pre_g),
        (w_in[l] * col_scale).astype(BF16), batch)
    proj3 = proj.reshape(batch, seq, IN_PROJ_WIDTH)
    o_d = _diff_attention(slopes_d, proj3, vt, row(lambda_q1), row(lambda_k1), row(lambda_q2),
                          row(lambda_k2), diff_subln_g[l].reshape(DIFF_V_DIM, 1))
    o_l = _dilated_attention(slopes_l, proj3)
    w_o = w_out[l].astype(BF16)
    out = _outproj_ffn2(
        o_d.reshape(batch * seq, DIFF_V_WIDTH), o_l.reshape(batch * seq, DIL_WIDTH), x1,
        w_o[:DIFF_V_WIDTH], w_o[DIFF_V_WIDTH:], row(mix_post_g), row(ffn2_pre_g),
        ffn2_w_gate[l].astype(BF16), ffn2_w_up[l].astype(BF16), ffn2_w_down[l].astype(BF16),
        row(ffn2_post_g))
    return out.reshape(batch, seq, d_model)
```

```python
import functools
import math

import numpy as np
import jax
import jax.numpy as jnp
from jax import lax
from jax.experimental import pallas as pl
from jax.experimental.pallas import tpu as pltpu

F32 = jnp.float32
BF16 = jnp.bfloat16

D_MODEL = 1024
SEQ = 2048
HEAD_DIM = 64
N_DIFF_HEADS = 4
DIFF_V_DIM = 2 * HEAD_DIM
N_DIL_HEADS = 8
DIL_PATTERNS = ((128, 1), (512, 4), (2048, 16))
Q_BLOCK = 128
D_FF = 2816
RMS_EPS = 1e-6

DIFF_QK_WIDTH = N_DIFF_HEADS * 2 * HEAD_DIM
DIFF_V_WIDTH = N_DIFF_HEADS * DIFF_V_DIM
DIL_WIDTH = N_DIL_HEADS * HEAD_DIM
IN_PROJ_WIDTH = 2 * DIFF_QK_WIDTH + DIFF_V_WIDTH + 3 * DIL_WIDTH
DIL_OFFSET = 2 * DIFF_QK_WIDTH + DIFF_V_WIDTH
N_ALIBI_HEADS = N_DIFF_HEADS + N_DIL_HEADS
LAM_INIT = 0.8 - 0.6 * math.exp(-0.3 * 0)

LANES = 128
NEG_BIG = -1e30
VMEM_LIMIT_FFN = 56 * 1024 * 1024
VMEM_LIMIT_ATTN = 52 * 1024 * 1024

TM_IN = 512
TM_OUT = 1024
TM_SUB = 256
TQ = 256
TK = 256


def _rms(x, g):
    ms = jnp.mean(x * x, axis=-1, keepdims=True)
    return x * lax.rsqrt(ms + RMS_EPS) * g


def _row_groups(tm):
    return [slice(r * TM_SUB, (r + 1) * TM_SUB) for r in range(tm // TM_SUB)]


def _swiglu_half_step(xs, pre_g, wg_ref, wu_ref, wd_ref, post_g):
    xb = [_rms(x, pre_g).astype(BF16) for x in xs]
    gu = [(jnp.dot(b, wg_ref[...], preferred_element_type=F32),
           jnp.dot(b, wu_ref[...], preferred_element_type=F32)) for b in xb]
    ys = []
    for g, u in gu:
        h = (g * (1.0 / (1.0 + jnp.exp(-g))) * u).astype(BF16)
        ys.append(jnp.dot(h, wd_ref[...], preferred_element_type=F32))
    return [x + 0.5 * _rms(y, post_g) for x, y in zip(xs, ys)]


def _ffn1_inproj_kernel(x_ref, pre_g, wg, wu, wd, post_g, mix_g, w_in, x1_ref, qk_ref, vt_ref, dil_ref):
    groups = _row_groups(TM_IN)
    x1s = _swiglu_half_step([x_ref[rows, :] for rows in groups], pre_g[...], wg, wu, wd, post_g[...])
    for rows, x1 in zip(groups, x1s):
        x1_ref[rows, :] = x1
        hb = _rms(x1, mix_g[...]).astype(BF16)
        proj = jnp.dot(hb, w_in[...], preferred_element_type=F32)
        qk_ref[rows, :] = proj[:, :2 * DIFF_QK_WIDTH].astype(BF16)
        v_d = proj[:, 2 * DIFF_QK_WIDTH:DIL_OFFSET]
        vt_ref[:, rows] = v_d.T.astype(BF16)
        dil_ref[rows, :] = proj[:, DIL_OFFSET:]


def _const_spec(shape):
    nd = len(shape)
    return pl.BlockSpec(shape, lambda *_: (0,) * nd, pipeline_mode=pl.Buffered(1))


def _ffn1_inproj(x2d, pre_g, wg, wu, wd, post_g, mix_g, w_in, batch):
    t = x2d.shape[0]
    tm = TM_IN
    tiles_per_seq = SEQ // tm
    row = lambda i: (i, 0)
    return pl.pallas_call(
        _ffn1_inproj_kernel,
        grid=(t // tm,),
        in_specs=[
            pl.BlockSpec((tm, D_MODEL), row),
            _const_spec((1, D_MODEL)),
            _const_spec((D_MODEL, D_FF)),
            _const_spec((D_MODEL, D_FF)),
            _const_spec((D_FF, D_MODEL)),
            _const_spec((1, D_MODEL)),
            _const_spec((1, D_MODEL)),
            _const_spec((D_MODEL, IN_PROJ_WIDTH)),
        ],
        out_specs=[
            pl.BlockSpec((tm, D_MODEL), row),
            pl.BlockSpec((tm, 2 * DIFF_QK_WIDTH), row),
            pl.BlockSpec((None, DIFF_V_WIDTH, tm), lambda i: (i // tiles_per_seq, 0, i % tiles_per_seq)),
            pl.BlockSpec((tm, 3 * DIL_WIDTH), row),
        ],
        out_shape=[
            jax.ShapeDtypeStruct((t, D_MODEL), F32),
            jax.ShapeDtypeStruct((t, 2 * DIFF_QK_WIDTH), BF16),
            jax.ShapeDtypeStruct((batch, DIFF_V_WIDTH, SEQ), BF16),
            jax.ShapeDtypeStruct((t, 3 * DIL_WIDTH), F32),
        ],
        compiler_params=pltpu.CompilerParams(
            dimension_semantics=("arbitrary",), vmem_limit_bytes=VMEM_LIMIT_FFN),
        name="ffn1_inproj",
    )(x2d, pre_g, wg, wu, wd, post_g, mix_g, w_in)


def _outproj_ffn2_kernel(od_ref, ol_ref, x1_ref, wo_d, wo_l, mix_post_g, pre_g, wg, wu, wd, post_g, out_ref):
    groups = _row_groups(TM_OUT)
    x2s = []
    for rows in groups:
        h = (jnp.dot(od_ref[rows, :], wo_d[...], preferred_element_type=F32)
             + jnp.dot(ol_ref[rows, :], wo_l[...], preferred_element_type=F32))
        x2s.append(x1_ref[rows, :] + _rms(h, mix_post_g[...]))
    outs = _swiglu_half_step(x2s, pre_g[...], wg, wu, wd, post_g[...])
    for rows, out in zip(groups, outs):
        out_ref[rows, :] = out


def _outproj_ffn2(od, ol, x1, wo_d, wo_l, mix_post_g, pre_g, wg, wu, wd, post_g):
    t = x1.shape[0]
    tm = TM_OUT
    row = lambda i: (i, 0)
    return pl.pallas_call(
        _outproj_ffn2_kernel,
        grid=(t // tm,),
        in_specs=[
            pl.BlockSpec((tm, DIFF_V_WIDTH), row),
            pl.BlockSpec((tm, DIL_WIDTH), row),
            pl.BlockSpec((tm, D_MODEL), row),
            _const_spec((DIFF_V_WIDTH, D_MODEL)),
            _const_spec((DIL_WIDTH, D_MODEL)),
            _const_spec((1, D_MODEL)),
            _const_spec((1, D_MODEL)),
            _const_spec((D_MODEL, D_FF)),
            _const_spec((D_MODEL, D_FF)),
            _const_spec((D_FF, D_MODEL)),
            _const_spec((1, D_MODEL)),
        ],
        out_specs=pl.BlockSpec((tm, D_MODEL), row),
        out_shape=jax.ShapeDtypeStruct((t, D_MODEL), F32),
        compiler_params=pltpu.CompilerParams(
            dimension_semantics=("arbitrary",), vmem_limit_bytes=VMEM_LIMIT_FFN),
        name="outproj_ffn2",
    )(od, ol, x1, wo_d, wo_l, mix_post_g, pre_g, wg, wu, wd, post_g)


_NT = (((1,), (1,)), ((), ()))
N_MAPS = 2 * N_DIFF_HEADS
ONES_ROWS = 16
SLOPE_TERMS = 3
N_QT = SEQ // TQ
N_TILES = N_QT * (N_QT + 1) // 2
TILES_PER_TRIP = 2
LOG2E = math.log2(math.e)


def _bf16_terms(x, n_terms):
    terms, rest = [], np.float32(x)
    for _ in range(n_terms):
        t = np.float32(np.asarray(rest, dtype=BF16))
        terms.append(float(t))
        rest = np.float32(rest - t)
    return terms


def _diff_attn_kernel(q_ref, k_ref, vt_ref, lq1, lk1, lq2, lk2, g_ref, o_ref,
                      kp, qp, vta, masks, s_buf, p_buf, acc, *, slopes):
    lam = (jnp.exp(jnp.sum(lq1[...] * lk1[...], axis=-1, keepdims=True))
           - jnp.exp(jnp.sum(lq2[...] * lk2[...], axis=-1, keepdims=True)) + LAM_INIT)
    lane = lax.broadcasted_iota(jnp.int32, (1, LANES), 1)
    first_map = lane < HEAD_DIM

    def in_lanes(base, values):
        row = jnp.zeros((1, LANES), F32)
        for i, v in enumerate(values):
            row = jnp.where(lane == base + i, v, row)
        return row.astype(BF16)

    slope_rows = [(in_lanes(HEAD_DIM, _bf16_terms(s * LOG2E, SLOPE_TERMS)),
                   in_lanes(0, _bf16_terms(s * LOG2E, SLOPE_TERMS))) for s in slopes]
    key_idx = lax.broadcasted_iota(jnp.int32, (TK, LANES), 0).astype(F32)
    idx_first = jnp.where((lane >= HEAD_DIM) & (lane < HEAD_DIM + SLOPE_TERMS), key_idx, 0.0).astype(BF16)
    idx_second = jnp.where(lane < SLOPE_TERMS, key_idx, 0.0).astype(BF16)

    def augment(i, carry):
        rows = pl.ds(pl.multiple_of(i * TK, TK), TK)
        for h in range(N_DIFF_HEADS):
            k = k_ref[rows, h * LANES:(h + 1) * LANES]
            kp[2 * h, rows, :] = jnp.where(first_map, k, idx_first)
            kp[2 * h + 1, rows, :] = jnp.where(first_map, idx_second, k)
            q = q_ref[rows, h * LANES:(h + 1) * LANES]
            qp[2 * h, rows, :] = jnp.where(first_map, q, slope_rows[h][0])
            qp[2 * h + 1, rows, :] = jnp.where(first_map, slope_rows[h][1], q)
        return carry

    lax.fori_loop(0, SEQ // TK, augment, 0)
    for h in range(N_DIFF_HEADS):
        vta[h, 0:DIFF_V_DIM, :] = vt_ref[h * DIFF_V_DIM:(h + 1) * DIFF_V_DIM, :]
        vta[h, DIFF_V_DIM:, :] = jnp.ones((ONES_ROWS, SEQ), BF16)
    causal = (lax.broadcasted_iota(jnp.int32, (TK, TQ), 1)
              >= lax.broadcasted_iota(jnp.int32, (TK, TQ), 0))
    masks[0] = jnp.zeros((TK, TQ), F32)
    masks[1] = jnp.where(causal, 0.0, NEG_BIG)

    def clear(qi, carry):
        acc[qi] = jnp.zeros((N_MAPS, DIFF_V_DIM + ONES_ROWS, TQ), F32)
        return carry

    lax.fori_loop(0, N_QT, clear, 0)

    def score_matmuls(qi, t, slot):
        qs = pl.multiple_of(qi * TQ, TQ)
        ks = pl.multiple_of((qi - t) * TK, TK)
        mask = masks[(t == 0).astype(jnp.int32)]
        tile_max = []
        for c in range(N_MAPS):
            s = lax.dot_general(kp[c, pl.ds(ks, TK), :], qp[c, pl.ds(qs, TQ), :], _NT,
                                preferred_element_type=F32) + mask
            s_buf[slot, c] = s
            tile_max.append(jnp.max(s, axis=0, keepdims=True))
        return tuple(tile_max)

    def softmax_step(t, slot, tile_max, maxima):
        on_diagonal = t == 0
        dist = jnp.asarray(t * TK, F32)
        new_m, alphas = [], []
        for c in range(N_MAPS):
            coff = (slopes[c // 2] * LOG2E) * dist
            m = jnp.where(on_diagonal, NEG_BIG, maxima[c])
            n = jnp.maximum(m, tile_max[c] - coff)
            new_m.append(n)
            alphas.append(jnp.exp2(m - n))
            p_buf[slot, c] = jnp.exp2(s_buf[slot, c] - (n + coff)).astype(BF16)
        return tuple(new_m), tuple(alphas)

    def value_matmuls(qi, t, slot, alphas):
        ks = pl.multiple_of((qi - t) * TK, TK)
        for c in range(N_MAPS):
            pv = jnp.dot(vta[c // 2, :, pl.ds(ks, TK)], p_buf[slot, c], preferred_element_type=F32)
            acc[qi, c] = alphas[c] * acc[qi, c] + pv

    def next_tile(qi, t):
        last = t == qi
        nq = jnp.where(last, qi + 1, qi)
        nt = jnp.where(last, 0, t + 1)
        return jnp.minimum(nq, N_QT - 1), nt

    zero = jnp.int32(0)
    first_max = score_matmuls(zero, zero, 0)
    p_buf[1] = jnp.zeros((N_MAPS, TK, TQ), BF16)
    row = lambda v: tuple(jnp.full((1, TQ), v, F32) for _ in range(N_MAPS))

    def several_tiles(i, carry):
        qi, t, qi_prev, t_prev, tile_max, maxima, alphas = carry
        for u in range(TILES_PER_TRIP):
            slot = u % 2
            q_next, t_next = next_tile(qi, t)
            next_max = score_matmuls(q_next, t_next, 1 - slot)
            maxima, new_alphas = softmax_step(t, slot, tile_max, maxima)
            value_matmuls(qi_prev, t_prev, 1 - slot, alphas)
            qi_prev, t_prev, alphas = qi, t, new_alphas
            qi, t, tile_max = q_next, t_next, next_max
        return qi, t, qi_prev, t_prev, tile_max, maxima, alphas

    assert TILES_PER_TRIP % 2 == 0 and N_TILES % TILES_PER_TRIP == 0
    carry = lax.fori_loop(0, N_TILES // TILES_PER_TRIP, several_tiles,
                          (zero, zero, zero, zero, first_max, row(0.0), row(1.0)))
    _, _, qi_prev, t_prev, _, _, alphas = carry
    value_matmuls(qi_prev, t_prev, 1, alphas)

    def finish(qi, carry):
        qs = pl.multiple_of(qi * TQ, TQ)
        for h in range(N_DIFF_HEADS):
            a1, a2 = acc[qi, 2 * h], acc[qi, 2 * h + 1]
            l1, l2 = a1[DIFF_V_DIM:DIFF_V_DIM + 1, :], a2[DIFF_V_DIM:DIFF_V_DIM + 1, :]
            o = a1[:DIFF_V_DIM, :] * (1.0 / l1) - lam * (a2[:DIFF_V_DIM, :] * (1.0 / l2))
            ms = jnp.mean(o * o, axis=0, keepdims=True)
            y = o * lax.rsqrt(ms + RMS_EPS) * g_ref[...] * (1.0 - LAM_INIT)
            o_ref[pl.ds(qs, TQ), h * LANES:(h + 1) * LANES] = y.T.astype(BF16)
        return carry

    lax.fori_loop(0, N_QT, finish, 0)


def _diff_attention(slopes, qk3, vt, lq1, lk1, lq2, lk2, g_col):
    batch = qk3.shape[0]
    vec = pl.BlockSpec((1, HEAD_DIM), lambda b: (0, 0))
    return pl.pallas_call(
        functools.partial(_diff_attn_kernel, slopes=slopes),
        grid=(batch,),
        in_specs=[
            pl.BlockSpec((None, SEQ, DIFF_QK_WIDTH), lambda b: (b, 0, 0)),
            pl.BlockSpec((None, SEQ, DIFF_QK_WIDTH), lambda b: (b, 0, 1)),
            pl.BlockSpec((None, DIFF_V_WIDTH, SEQ), lambda b: (b, 0, 0)),
            vec, vec, vec, vec,
            pl.BlockSpec((DIFF_V_DIM, 1), lambda b: (0, 0)),
        ],
        out_specs=pl.BlockSpec((None, SEQ, DIFF_V_WIDTH), lambda b: (b, 0, 0)),
        out_shape=jax.ShapeDtypeStruct((batch, SEQ, DIFF_V_WIDTH), BF16),
        scratch_shapes=[
            pltpu.VMEM((N_MAPS, SEQ, LANES), BF16),
            pltpu.VMEM((N_MAPS, SEQ, LANES), BF16),
            pltpu.VMEM((N_DIFF_HEADS, DIFF_V_DIM + ONES_ROWS, SEQ), BF16),
            pltpu.VMEM((2, TK, TQ), F32),
            pltpu.VMEM((2, N_MAPS, TK, TQ), F32),
            pltpu.VMEM((2, N_MAPS, TK, TQ), BF16),
            pltpu.VMEM((N_QT, N_MAPS, DIFF_V_DIM + ONES_ROWS, TQ), F32),
        ],
        compiler_params=pltpu.CompilerParams(
            dimension_semantics=("arbitrary",), vmem_limit_bytes=VMEM_LIMIT_ATTN),
        name="diff_attention",
    )(qk3, qk3, vt, lq1, lk1, lq2, lk2, g_col)


N_PATTERNS = len(DIL_PATTERNS)
N_WIN = Q_BLOCK
N_BLOCKS = SEQ // Q_BLOCK
KPAD = Q_BLOCK
PREP_UNROLL = 4


def _dil_attn_kernel(slopes_ref, q_ref, k_ref, v_ref, o_ref,
                     q4, k4, v4, qa, qb, kt, va, vb, bias, o_acc, l_acc, m_acc, m_swapped):
    hp = pl.program_id(1)
    lane = lax.broadcasted_iota(jnp.int32, (1, LANES), 1)
    first_head = lane < HEAD_DIM

    kt[0, :, 0:KPAD] = jnp.zeros((LANES, KPAD), BF16)
    va[0, 0:KPAD, :] = jnp.zeros((KPAD, LANES), BF16)
    vb[0, 0:KPAD, :] = jnp.zeros((KPAD, LANES), BF16)

    col = lax.broadcasted_iota(jnp.int32, (Q_BLOCK, 2 * Q_BLOCK), 1)
    dsub = Q_BLOCK + lax.broadcasted_iota(jnp.int32, (Q_BLOCK, 2 * Q_BLOCK), 0) - col
    valid = (dsub >= 0) & (dsub <= N_WIN)
    dsub_f = dsub.astype(F32)
    for p, (_, dil) in enumerate(DIL_PATTERNS):
        for hh in range(2):
            slope = slopes_ref[2 * hp + hh]
            b = jnp.where(valid, (-slope * (LOG2E * dil)) * dsub_f, NEG_BIG)
            bias[2 * p + hh] = b
            if p == 0:
                bias[2 * N_PATTERNS + hh] = jnp.where(col >= Q_BLOCK, b, NEG_BIG)

    def emit(p, block, q, k, v):
        dst = pl.multiple_of(block * Q_BLOCK, Q_BLOCK)
        qa[p, pl.ds(dst, Q_BLOCK), :] = jnp.where(first_head, q, 0.0).astype(BF16)
        qb[p, pl.ds(dst, Q_BLOCK), :] = jnp.where(first_head, 0.0, q).astype(BF16)
        kt[p, :, pl.ds(KPAD + dst, Q_BLOCK)] = k.T.astype(BF16)
        va[p, pl.ds(KPAD + dst, Q_BLOCK), :] = jnp.where(first_head, v, 1.0).astype(BF16)
        vb[p, pl.ds(KPAD + dst, Q_BLOCK), :] = jnp.where(first_head, 1.0, v).astype(BF16)

    step = DIL_PATTERNS[1][1]
    assert DIL_PATTERNS[2][1] == step * step
    run4 = SEQ // step

    def first_level(i, carry):
        for u in range(PREP_UNROLL):
            block = i * PREP_UNROLL + u
            r4 = block >> (step.bit_length() - 1)
            n = block & (step - 1)
            rows = pl.ds(pl.multiple_of(block * Q_BLOCK, Q_BLOCK), Q_BLOCK)
            emit(0, block, q_ref[rows, :], k_ref[rows, :], v_ref[rows, :])
            src = pl.ds(n * (Q_BLOCK * step) + r4, Q_BLOCK, stride=step)
            q, k, v = q_ref[src, :], k_ref[src, :], v_ref[src, :]
            q4[rows, :] = q
            k4[rows, :] = k
            v4[rows, :] = v
            emit(1, block, q, k, v)
        return carry

    def second_level(a, carry):
        for r4 in range(step):
            src = pl.ds(r4 * run4 + a, Q_BLOCK, stride=step)
            emit(2, a * step + r4, q4[src, :], k4[src, :], v4[src, :])
        return carry

    assert N_BLOCKS == step * step
    lax.fori_loop(0, N_BLOCKS // PREP_UNROLL, first_level, 0)
    lax.fori_loop(0, step, second_level, 0)

    def attend(p, dil, q_row, with_prev, nat_start, bias_idx):
        n_keys = 2 * Q_BLOCK if with_prev else Q_BLOCK
        key_row = q_row + (KPAD - Q_BLOCK if with_prev else KPAD)
        if not isinstance(q_row, int):
            q_row = pl.multiple_of(q_row, Q_BLOCK)
            key_row = pl.multiple_of(key_row, Q_BLOCK)
        k_t = kt[p, :, pl.ds(key_row, n_keys)]
        outs = []
        for hh, (q_src, v_src) in enumerate(((qa, va), (qb, vb))):
            s = jnp.dot(q_src[p, pl.ds(q_row, Q_BLOCK), :], k_t, preferred_element_type=F32)
            if with_prev:
                s = s + bias[bias_idx[hh]]
            else:
                s = s + bias[bias_idx[hh], :, Q_BLOCK:]
            m = jnp.max(s, axis=1, keepdims=True)
            e = jnp.exp2(s - m).astype(BF16)
            ov = jnp.dot(e, v_src[p, pl.ds(key_row, n_keys), :], preferred_element_type=F32)
            outs.append((ov, m))
        if dil == 1:
            nat = pl.ds(nat_start, Q_BLOCK)
        else:
            nat = pl.ds(nat_start, Q_BLOCK, stride=dil)
        o_acc[p, nat, :] = jnp.where(first_head, outs[0][0], outs[1][0])
        l_acc[p, nat, :] = jnp.where(first_head, outs[1][0], outs[0][0])
        m_acc[p, nat, :] = jnp.where(first_head, outs[0][1], outs[1][1])
        m_swapped[p, nat, :] = jnp.where(first_head, outs[1][1], outs[0][1])

    for n in range(N_BLOCKS):
        first = 2 * N_PATTERNS if n == 0 else 0
        attend(0, 1, n * Q_BLOCK, True, n * Q_BLOCK, (first, first + 1))

    dil4 = DIL_PATTERNS[1][1]
    run = SEQ // dil4
    for n in range(run // Q_BLOCK):
        for r in range(dil4):
            attend(1, dil4, r * run + n * Q_BLOCK, n > 0, n * (Q_BLOCK * dil4) + r, (2, 3))

    dil16 = DIL_PATTERNS[2][1]
    for r in range(dil16):
        attend(2, dil16, r * Q_BLOCK, False, r, (4, 5))

    rows_per_step = 256

    def blend(i, carry):
        sl = pl.ds(pl.multiple_of(i * rows_per_step, rows_per_step), rows_per_step)
        add = lambda a, b: a + b

        def weighted(m_ref, x_ref):
            ms = [m_ref[p, sl, :] for p in range(N_PATTERNS)]
            mx = functools.reduce(jnp.maximum, ms)
            return functools.reduce(add, [jnp.exp2(ms[p] - mx) * x_ref[p, sl, :] for p in range(N_PATTERNS)])

        num = weighted(m_acc, o_acc)
        den = pltpu.roll(weighted(m_swapped, l_acc), HEAD_DIM, axis=1)
        o_ref[sl, :] = (num * (1.0 / den)).astype(BF16)
        return carry

    lax.fori_loop(0, SEQ // rows_per_step, blend, 0)


def _dilated_attention(slopes, dil3):
    batch = dil3.shape[0]
    n_pairs = DIL_WIDTH // LANES
    q_off, k_off, v_off = 0, n_pairs, 2 * n_pairs
    return pl.pallas_call(
        _dil_attn_kernel,
        grid=(batch, n_pairs),
        in_specs=[
            pl.BlockSpec(memory_space=pltpu.SMEM),
            pl.BlockSpec((None, SEQ, LANES), lambda b, h: (b, 0, q_off + h)),
            pl.BlockSpec((None, SEQ, LANES), lambda b, h: (b, 0, k_off + h)),
            pl.BlockSpec((None, SEQ, LANES), lambda b, h: (b, 0, v_off + h)),
        ],
        out_specs=pl.BlockSpec((None, SEQ, LANES), lambda b, h: (b, 0, h)),
        out_shape=jax.ShapeDtypeStruct((batch, SEQ, DIL_WIDTH), BF16),
        scratch_shapes=[
            pltpu.VMEM((SEQ, LANES), F32),
            pltpu.VMEM((SEQ, LANES), F32),
            pltpu.VMEM((SEQ, LANES), F32),
            pltpu.VMEM((N_PATTERNS, SEQ, LANES), BF16),
            pltpu.VMEM((N_PATTERNS, SEQ, LANES), BF16),
            pltpu.VMEM((N_PATTERNS, LANES, KPAD + SEQ), BF16),
            pltpu.VMEM((N_PATTERNS, KPAD + SEQ, LANES), BF16),
            pltpu.VMEM((N_PATTERNS, KPAD + SEQ, LANES), BF16),
            pltpu.VMEM((2 * N_PATTERNS + 2, Q_BLOCK, 2 * Q_BLOCK), F32),
            pltpu.VMEM((N_PATTERNS, SEQ, LANES), F32),
            pltpu.VMEM((N_PATTERNS, SEQ, LANES), F32),
            pltpu.VMEM((N_PATTERNS, SEQ, LANES), F32),
            pltpu.VMEM((N_PATTERNS, SEQ, LANES), F32),
        ],
        compiler_params=pltpu.CompilerParams(
            dimension_semantics=("arbitrary", "arbitrary"), vmem_limit_bytes=VMEM_LIMIT_ATTN),
        name="dilated_attention",
    )(slopes, dil3, dil3, dil3)


def _alibi_slopes():
    all_s = 2.0 ** (-8.0 * np.arange(1, N_ALIBI_HEADS + 1, dtype=np.float32) / N_ALIBI_HEADS)
    diff_idx = np.arange(0, N_ALIBI_HEADS, N_ALIBI_HEADS // N_DIFF_HEADS)
    dil_idx = np.setdiff1d(np.arange(N_ALIBI_HEADS), diff_idx)
    return tuple(float(s) for s in all_s[diff_idx]), jnp.asarray(all_s[dil_idx], F32)


def kernel(x, ffn1_pre_g, ffn1_w_gate, ffn1_w_up, ffn1_w_down, ffn1_post_g, mix_pre_g, w_in, lambda_q1, lambda_k1, lambda_q2, lambda_k2, diff_subln_g, w_out, mix_post_g, ffn2_pre_g, ffn2_w_gate, ffn2_w_up, ffn2_w_down, ffn2_post_g):
    batch, seq, d_model = x.shape
    assert (seq, d_model) == (SEQ, D_MODEL) and ffn1_pre_g.shape[0] == 1
    slopes_d, slopes_l = _alibi_slopes()

    scale = HEAD_DIM ** -0.5 * LOG2E
    col_scale = np.ones((IN_PROJ_WIDTH,), np.float32)
    col_scale[:DIFF_QK_WIDTH] = scale
    q_l0 = 2 * DIFF_QK_WIDTH + DIFF_V_WIDTH
    col_scale[q_l0:q_l0 + DIL_WIDTH] = scale

    l = 0
    row = lambda g: g[l].reshape(1, -1)
    x2d = x.reshape(batch * seq, d_model)
    x1, qk, vt, dil = _ffn1_inproj(
        x2d, row(ffn1_pre_g), ffn1_w_gate[l].astype(BF16), ffn1_w_up[l].astype(BF16),
        ffn1_w_down[l].astype(BF16), row(ffn1_post_g), row(mix_pre_g),
        (w_in[l] * col_scale).astype(BF16), batch)
    o_d = _diff_attention(slopes_d, qk.reshape(batch, seq, 2 * DIFF_QK_WIDTH), vt, row(lambda_q1),
                          row(lambda_k1), row(lambda_q2), row(lambda_k2),
                          diff_subln_g[l].reshape(DIFF_V_DIM, 1))
    o_l = _dilated_attention(slopes_l, dil.reshape(batch, seq, 3 * DIL_WIDTH))
    w_o = w_out[l].astype(BF16)
    out = _outproj_ffn2(
        o_d.reshape(batch * seq, DIFF_V_WIDTH), o_l.reshape(batch * seq, DIL_WIDTH), x1,
        w_o[:DIFF_V_WIDTH], w_o[DIFF_V_WIDTH:], row(mix_post_g), row(ffn2_pre_g),
        ffn2_w_gate[l].astype(BF16), ffn2_w_up[l].astype(BF16), ffn2_w_down[l].astype(BF16),
        row(ffn2_post_g))
    return out.reshape(batch, seq, d_model)
```

```python
import functools
import math

import numpy as np
import jax
import jax.numpy as jnp
from jax import lax
from jax.experimental import pallas as pl
from jax.experimental.pallas import tpu as pltpu

F32 = jnp.float32
BF16 = jnp.bfloat16

D_MODEL = 1024
SEQ = 2048
HEAD_DIM = 64
N_DIFF_HEADS = 4
DIFF_V_DIM = 2 * HEAD_DIM
N_DIL_HEADS = 8
DIL_PATTERNS = ((128, 1), (512, 4), (2048, 16))
Q_BLOCK = 128
D_FF = 2816
RMS_EPS = 1e-6

DIFF_QK_WIDTH = N_DIFF_HEADS * 2 * HEAD_DIM
DIFF_V_WIDTH = N_DIFF_HEADS * DIFF_V_DIM
DIL_WIDTH = N_DIL_HEADS * HEAD_DIM
IN_PROJ_WIDTH = 2 * DIFF_QK_WIDTH + DIFF_V_WIDTH + 3 * DIL_WIDTH
DIL_OFFSET = 2 * DIFF_QK_WIDTH + DIFF_V_WIDTH
N_ALIBI_HEADS = N_DIFF_HEADS + N_DIL_HEADS
LAM_INIT = 0.8 - 0.6 * math.exp(-0.3 * 0)

LANES = 128
NEG_BIG = -1e30
VMEM_LIMIT_FFN = 56 * 1024 * 1024
VMEM_LIMIT_ATTN = 52 * 1024 * 1024

TM_IN = 512
TM_OUT = 1024
TM_SUB = 256
TQ = 256
TK = 256


def _rms(x, g):
    ms = jnp.mean(x * x, axis=-1, keepdims=True)
    return x * lax.rsqrt(ms + RMS_EPS) * g


def _row_groups(tm):
    return [slice(r * TM_SUB, (r + 1) * TM_SUB) for r in range(tm // TM_SUB)]


def _swiglu_half_step(xs, pre_g, wg_ref, wu_ref, wd_ref, post_g):
    xb = [_rms(x, pre_g).astype(BF16) for x in xs]
    gu = [(jnp.dot(b, wg_ref[...], preferred_element_type=F32),
           jnp.dot(b, wu_ref[...], preferred_element_type=F32)) for b in xb]
    ys = []
    for g, u in gu:
        h = (g * (1.0 / (1.0 + jnp.exp(-g))) * u).astype(BF16)
        ys.append(jnp.dot(h, wd_ref[...], preferred_element_type=F32))
    return [x + 0.5 * _rms(y, post_g) for x, y in zip(xs, ys)]


def _ffn1_inproj_kernel(x_ref, pre_g, wg, wu, wd, post_g, mix_g, w_in, x1_ref, qk_ref, vt_ref, dil_ref):
    groups = _row_groups(TM_IN)
    x1s = _swiglu_half_step([x_ref[rows, :] for rows in groups], pre_g[...], wg, wu, wd, post_g[...])
    for rows, x1 in zip(groups, x1s):
        x1_ref[rows, :] = x1
        hb = _rms(x1, mix_g[...]).astype(BF16)
        proj = jnp.dot(hb, w_in[...], preferred_element_type=F32)
        qk_ref[rows, :] = proj[:, :2 * DIFF_QK_WIDTH].astype(BF16)
        v_d = proj[:, 2 * DIFF_QK_WIDTH:DIL_OFFSET]
        vt_ref[:, rows] = v_d.T.astype(BF16)
        dil_ref[rows, :] = proj[:, DIL_OFFSET:]


def _const_spec(shape):
    nd = len(shape)
    return pl.BlockSpec(shape, lambda *_: (0,) * nd, pipeline_mode=pl.Buffered(1))


def _ffn1_inproj(x2d, pre_g, wg, wu, wd, post_g, mix_g, w_in, batch):
    t = x2d.shape[0]
    tm = TM_IN
    tiles_per_seq = SEQ // tm
    row = lambda i: (i, 0)
    return pl.pallas_call(
        _ffn1_inproj_kernel,
        grid=(t // tm,),
        in_specs=[
            pl.BlockSpec((tm, D_MODEL), row),
            _const_spec((1, D_MODEL)),
            _const_spec((D_MODEL, D_FF)),
            _const_spec((D_MODEL, D_FF)),
            _const_spec((D_FF, D_MODEL)),
            _const_spec((1, D_MODEL)),
            _const_spec((1, D_MODEL)),
            _const_spec((D_MODEL, IN_PROJ_WIDTH)),
        ],
        out_specs=[
            pl.BlockSpec((tm, D_MODEL), row),
            pl.BlockSpec((tm, 2 * DIFF_QK_WIDTH), row),
            pl.BlockSpec((None, DIFF_V_WIDTH, tm), lambda i: (i // tiles_per_seq, 0, i % tiles_per_seq)),
            pl.BlockSpec((tm, 3 * DIL_WIDTH), row),
        ],
        out_shape=[
            jax.ShapeDtypeStruct((t, D_MODEL), F32),
            jax.ShapeDtypeStruct((t, 2 * DIFF_QK_WIDTH), BF16),
            jax.ShapeDtypeStruct((batch, DIFF_V_WIDTH, SEQ), BF16),
            jax.ShapeDtypeStruct((t, 3 * DIL_WIDTH), F32),
        ],
        compiler_params=pltpu.CompilerParams(
            dimension_semantics=("arbitrary",), vmem_limit_bytes=VMEM_LIMIT_FFN),
        name="ffn1_inproj",
    )(x2d, pre_g, wg, wu, wd, post_g, mix_g, w_in)


def _outproj_ffn2_kernel(od_ref, ol_ref, x1_ref, wo_d, wo_l, mix_post_g, pre_g, wg, wu, wd, post_g, out_ref):
    groups = _row_groups(TM_OUT)
    x2s = []
    for rows in groups:
        h = (jnp.dot(od_ref[rows, :], wo_d[...], preferred_element_type=F32)
             + jnp.dot(ol_ref[rows, :], wo_l[...], preferred_element_type=F32))
        x2s.append(x1_ref[rows, :] + _rms(h, mix_post_g[...]))
    outs = _swiglu_half_step(x2s, pre_g[...], wg, wu, wd, post_g[...])
    for rows, out in zip(groups, outs):
        out_ref[rows, :] = out


def _outproj_ffn2(od, ol, x1, wo_d, wo_l, mix_post_g, pre_g, wg, wu, wd, post_g):
    t = x1.shape[0]
    tm = TM_OUT
    row = lambda i: (i, 0)
    return pl.pallas_call(
        _outproj_ffn2_kernel,
        grid=(t // tm,),
        in_specs=[
            pl.BlockSpec((tm, DIFF_V_WIDTH), row),
            pl.BlockSpec((tm, DIL_WIDTH), row),
            pl.BlockSpec((tm, D_MODEL), row),
            _const_spec((DIFF_V_WIDTH, D_MODEL)),
            _const_spec((DIL_WIDTH, D_MODEL)),
            _const_spec((1, D_MODEL)),
            _const_spec((1, D_MODEL)),
            _const_spec((D_MODEL, D_FF)),
            _const_spec((D_MODEL, D_FF)),
            _const_spec((D_FF, D_MODEL)),
            _const_spec((1, D_MODEL)),
        ],
        out_specs=pl.BlockSpec((tm, D_MODEL), row),
        out_shape=jax.ShapeDtypeStruct((t, D_MODEL), F32),
        compiler_params=pltpu.CompilerParams(
            dimension_semantics=("arbitrary",), vmem_limit_bytes=VMEM_LIMIT_FFN),
        name="outproj_ffn2",
    )(od, ol, x1, wo_d, wo_l, mix_post_g, pre_g, wg, wu, wd, post_g)


_NT = (((1,), (1,)), ((), ()))
N_MAPS = 2 * N_DIFF_HEADS
ONES_ROWS = 16
SLOPE_TERMS = 3
N_QT = SEQ // TQ
N_TILES = N_QT * (N_QT + 1) // 2
TILES_PER_TRIP = 2
LOG2E = math.log2(math.e)


def _bf16_terms(x, n_terms):
    terms, rest = [], np.float32(x)
    for _ in range(n_terms):
        t = np.float32(np.asarray(rest, dtype=BF16))
        terms.append(float(t))
        rest = np.float32(rest - t)
    return terms


def _diff_attn_kernel(q_ref, k_ref, vt_ref, lq1, lk1, lq2, lk2, g_ref, o_ref,
                      kp, qp, vta, masks, s_buf, p_buf, acc, *, slopes):
    lam = (jnp.exp(jnp.sum(lq1[...] * lk1[...], axis=-1, keepdims=True))
           - jnp.exp(jnp.sum(lq2[...] * lk2[...], axis=-1, keepdims=True)) + LAM_INIT)
    lane = lax.broadcasted_iota(jnp.int32, (1, LANES), 1)
    first_map = lane < HEAD_DIM

    def in_lanes(base, values):
        row = jnp.zeros((1, LANES), F32)
        for i, v in enumerate(values):
            row = jnp.where(lane == base + i, v, row)
        return row.astype(BF16)

    slope_rows = [(in_lanes(HEAD_DIM, _bf16_terms(s * LOG2E, SLOPE_TERMS)),
                   in_lanes(0, _bf16_terms(s * LOG2E, SLOPE_TERMS))) for s in slopes]
    key_idx = lax.broadcasted_iota(jnp.int32, (TK, LANES), 0).astype(F32)
    idx_first = jnp.where((lane >= HEAD_DIM) & (lane < HEAD_DIM + SLOPE_TERMS), key_idx, 0.0).astype(BF16)
    idx_second = jnp.where(lane < SLOPE_TERMS, key_idx, 0.0).astype(BF16)

    def augment(i, carry):
        rows = pl.ds(pl.multiple_of(i * TK, TK), TK)
        for h in range(N_DIFF_HEADS):
            k = k_ref[rows, h * LANES:(h + 1) * LANES]
            kp[2 * h, rows, :] = jnp.where(first_map, k, idx_first)
            kp[2 * h + 1, rows, :] = jnp.where(first_map, idx_second, k)
            q = q_ref[rows, h * LANES:(h + 1) * LANES]
            qp[2 * h, rows, :] = jnp.where(first_map, q, slope_rows[h][0])
            qp[2 * h + 1, rows, :] = jnp.where(first_map, slope_rows[h][1], q)
        return carry

    lax.fori_loop(0, SEQ // TK, augment, 0)
    for h in range(N_DIFF_HEADS):
        vta[h, 0:DIFF_V_DIM, :] = vt_ref[h * DIFF_V_DIM:(h + 1) * DIFF_V_DIM, :]
        vta[h, DIFF_V_DIM:, :] = jnp.ones((ONES_ROWS, SEQ), BF16)
    causal = (lax.broadcasted_iota(jnp.int32, (TK, TQ), 1)
              >= lax.broadcasted_iota(jnp.int32, (TK, TQ), 0))
    masks[0] = jnp.zeros((TK, TQ), F32)
    masks[1] = jnp.where(causal, 0.0, NEG_BIG)

    def clear(qi, carry):
        acc[qi] = jnp.zeros((N_MAPS, DIFF_V_DIM + ONES_ROWS, TQ), F32)
        return carry

    lax.fori_loop(0, N_QT, clear, 0)

    def score_matmuls(qi, t, slot):
        qs = pl.multiple_of(qi * TQ, TQ)
        ks = pl.multiple_of((qi - t) * TK, TK)
        mask = masks[(t == 0).astype(jnp.int32)]
        tile_max = []
        for c in range(N_MAPS):
            s = lax.dot_general(kp[c, pl.ds(ks, TK), :], qp[c, pl.ds(qs, TQ), :], _NT,
                                preferred_element_type=F32) + mask
            s_buf[slot, c] = s
            tile_max.append(jnp.max(s, axis=0, keepdims=True))
        return tuple(tile_max)

    def softmax_step(t, slot, tile_max, maxima):
        on_diagonal = t == 0
        dist = jnp.asarray(t * TK, F32)
        new_m, alphas = [], []
        for c in range(N_MAPS):
            coff = (slopes[c // 2] * LOG2E) * dist
            m = jnp.where(on_diagonal, NEG_BIG, maxima[c])
            n = jnp.maximum(m, tile_max[c] - coff)
            new_m.append(n)
            alphas.append(jnp.exp2(m - n))
            p_buf[slot, c] = jnp.exp2(s_buf[slot, c] - (n + coff)).astype(BF16)
        return tuple(new_m), tuple(alphas)

    def value_matmuls(qi, t, slot, alphas):
        ks = pl.multiple_of((qi - t) * TK, TK)
        for c in range(N_MAPS):
            pv = jnp.dot(vta[c // 2, :, pl.ds(ks, TK)], p_buf[slot, c], preferred_element_type=F32)
            acc[qi, c] = alphas[c] * acc[qi, c] + pv

    def next_tile(qi, t):
        last = t == qi
        nq = jnp.where(last, qi + 1, qi)
        nt = jnp.where(last, 0, t + 1)
        return jnp.minimum(nq, N_QT - 1), nt

    zero = jnp.int32(0)
    first_max = score_matmuls(zero, zero, 0)
    p_buf[1] = jnp.zeros((N_MAPS, TK, TQ), BF16)
    row = lambda v: tuple(jnp.full((1, TQ), v, F32) for _ in range(N_MAPS))

    def several_tiles(i, carry):
        qi, t, qi_prev, t_prev, tile_max, maxima, alphas = carry
        for u in range(TILES_PER_TRIP):
            slot = u % 2
            q_next, t_next = next_tile(qi, t)
            next_max = score_matmuls(q_next, t_next, 1 - slot)
            maxima, new_alphas = softmax_step(t, slot, tile_max, maxima)
            value_matmuls(qi_prev, t_prev, 1 - slot, alphas)
            qi_prev, t_prev, alphas = qi, t, new_alphas
            qi, t, tile_max = q_next, t_next, next_max
        return qi, t, qi_prev, t_prev, tile_max, maxima, alphas

    assert TILES_PER_TRIP % 2 == 0 and N_TILES % TILES_PER_TRIP == 0
    carry = lax.fori_loop(0, N_TILES // TILES_PER_TRIP, several_tiles,
                          (zero, zero, zero, zero, first_max, row(0.0), row(1.0)))
    _, _, qi_prev, t_prev, _, _, alphas = carry
    value_matmuls(qi_prev, t_prev, 1, alphas)

    def finish(qi, carry):
        qs = pl.multiple_of(qi * TQ, TQ)
        for h in range(N_DIFF_HEADS):
            a1, a2 = acc[qi, 2 * h], acc[qi, 2 * h + 1]
            l1, l2 = a1[DIFF_V_DIM:DIFF_V_DIM + 1, :], a2[DIFF_V_DIM:DIFF_V_DIM + 1, :]
            o = a1[:DIFF_V_DIM, :] * (1.0 / l1) - lam * (a2[:DIFF_V_DIM, :] * (1.0 / l2))
            ms = jnp.mean(o * o, axis=0, keepdims=True)
            y = o * lax.rsqrt(ms + RMS_EPS) * g_ref[...] * (1.0 - LAM_INIT)
            o_ref[pl.ds(qs, TQ), h * LANES:(h + 1) * LANES] = y.T.astype(BF16)
        return carry

    lax.fori_loop(0, N_QT, finish, 0)


def _diff_attention(slopes, qk3, vt, lq1, lk1, lq2, lk2, g_col):
    batch = qk3.shape[0]
    vec = pl.BlockSpec((1, HEAD_DIM), lambda b: (0, 0))
    return pl.pallas_call(
        functools.partial(_diff_attn_kernel, slopes=slopes),
        grid=(batch,),
        in_specs=[
            pl.BlockSpec((None, SEQ, DIFF_QK_WIDTH), lambda b: (b, 0, 0)),
            pl.BlockSpec((None, SEQ, DIFF_QK_WIDTH), lambda b: (b, 0, 1)),
            pl.BlockSpec((None, DIFF_V_WIDTH, SEQ), lambda b: (b, 0, 0)),
            vec, vec, vec, vec,
            pl.BlockSpec((DIFF_V_DIM, 1), lambda b: (0, 0)),
        ],
        out_specs=pl.BlockSpec((None, SEQ, DIFF_V_WIDTH), lambda b: (b, 0, 0)),
        out_shape=jax.ShapeDtypeStruct((batch, SEQ, DIFF_V_WIDTH), BF16),
        scratch_shapes=[
            pltpu.VMEM((N_MAPS, SEQ, LANES), BF16),
            pltpu.VMEM((N_MAPS, SEQ, LANES), BF16),
            pltpu.VMEM((N_DIFF_HEADS, DIFF_V_DIM + ONES_ROWS, SEQ), BF16),
            pltpu.VMEM((2, TK, TQ), F32),
            pltpu.VMEM((2, N_MAPS, TK, TQ), F32),
            pltpu.VMEM((2, N_MAPS, TK, TQ), BF16),
            pltpu.VMEM((N_QT, N_MAPS, DIFF_V_DIM + ONES_ROWS, TQ), F32),
        ],
        compiler_params=pltpu.CompilerParams(
            dimension_semantics=("arbitrary",), vmem_limit_bytes=VMEM_LIMIT_ATTN),
        name="diff_attention",
    )(qk3, qk3, vt, lq1, lk1, lq2, lk2, g_col)


N_PATTERNS = len(DIL_PATTERNS)
N_WIN = Q_BLOCK
N_BLOCKS = SEQ // Q_BLOCK
KPAD = Q_BLOCK
PREP_UNROLL = 4


def _dil_attn_kernel(slopes_ref, q_ref, k_ref, v_ref, o_ref,
                     q4, k4, v4, qq, kt, vv, bias, o_acc, l_acc, m_acc, m_swapped):
    hp = pl.program_id(1)
    lane = lax.broadcasted_iota(jnp.int32, (1, LANES), 1)
    first_head = lane < HEAD_DIM

    kt[0, :, 0:KPAD] = jnp.zeros((LANES, KPAD), BF16)
    vv[0, 0:KPAD, :] = jnp.zeros((KPAD, LANES), BF16)

    col = lax.broadcasted_iota(jnp.int32, (Q_BLOCK, 2 * Q_BLOCK), 1)
    dsub = Q_BLOCK + lax.broadcasted_iota(jnp.int32, (Q_BLOCK, 2 * Q_BLOCK), 0) - col
    valid = (dsub >= 0) & (dsub <= N_WIN)
    dsub_f = dsub.astype(F32)
    for p, (_, dil) in enumerate(DIL_PATTERNS):
        for hh in range(2):
            slope = slopes_ref[2 * hp + hh]
            b = jnp.where(valid, (-slope * (LOG2E * dil)) * dsub_f, NEG_BIG)
            bias[2 * p + hh] = b
            if p == 0:
                bias[2 * N_PATTERNS + hh] = jnp.where(col >= Q_BLOCK, b, NEG_BIG)

    def emit(p, block, q, k, v):
        dst = pl.multiple_of(block * Q_BLOCK, Q_BLOCK)
        qq[p, pl.ds(dst, Q_BLOCK), :] = q.astype(BF16)
        kt[p, :, pl.ds(KPAD + dst, Q_BLOCK)] = k.T.astype(BF16)
        vv[p, pl.ds(KPAD + dst, Q_BLOCK), :] = v.astype(BF16)

    step = DIL_PATTERNS[1][1]
    assert DIL_PATTERNS[2][1] == step * step
    run4 = SEQ // step

    def first_level(i, carry):
        for u in range(PREP_UNROLL):
            block = i * PREP_UNROLL + u
            r4 = block >> (step.bit_length() - 1)
            n = block & (step - 1)
            rows = pl.ds(pl.multiple_of(block * Q_BLOCK, Q_BLOCK), Q_BLOCK)
            emit(0, block, q_ref[rows, :], k_ref[rows, :], v_ref[rows, :])
            src = pl.ds(n * (Q_BLOCK * step) + r4, Q_BLOCK, stride=step)
            q, k, v = q_ref[src, :], k_ref[src, :], v_ref[src, :]
            q4[rows, :] = q
            k4[rows, :] = k
            v4[rows, :] = v
            emit(1, block, q, k, v)
        return carry

    def second_level(a, carry):
        for r4 in range(step):
            src = pl.ds(r4 * run4 + a, Q_BLOCK, stride=step)
            emit(2, a * step + r4, q4[src, :], k4[src, :], v4[src, :])
        return carry

    assert N_BLOCKS == step * step
    lax.fori_loop(0, N_BLOCKS // PREP_UNROLL, first_level, 0)
    lax.fori_loop(0, step, second_level, 0)

    def attend(p, dil, q_row, with_prev, nat_start, bias_idx):
        n_keys = 2 * Q_BLOCK if with_prev else Q_BLOCK
        key_row = q_row + (KPAD - Q_BLOCK if with_prev else KPAD)
        if not isinstance(q_row, int):
            q_row = pl.multiple_of(q_row, Q_BLOCK)
            key_row = pl.multiple_of(key_row, Q_BLOCK)
        k_t = kt[p, :, pl.ds(key_row, n_keys)]
        q = qq[p, pl.ds(q_row, Q_BLOCK), :]
        v = vv[p, pl.ds(key_row, n_keys), :]
        zero, one = jnp.zeros((), BF16), jnp.ones((), BF16)
        outs = []
        for hh in range(2):
            qh = jnp.where(first_head, q, zero) if hh == 0 else jnp.where(first_head, zero, q)
            s = jnp.dot(qh, k_t, preferred_element_type=F32)
            if with_prev:
                s = s + bias[bias_idx[hh]]
            else:
                s = s + bias[bias_idx[hh], :, Q_BLOCK:]
            m = jnp.max(s, axis=1, keepdims=True)
            e = jnp.exp2(s - m).astype(BF16)
            vh = jnp.where(first_head, v, one) if hh == 0 else jnp.where(first_head, one, v)
            ov = jnp.dot(e, vh, preferred_element_type=F32)
            outs.append((ov, m))
        if dil == 1:
            nat = pl.ds(nat_start, Q_BLOCK)
        else:
            nat = pl.ds(nat_start, Q_BLOCK, stride=dil)
        o_acc[p, nat, :] = jnp.where(first_head, outs[0][0], outs[1][0])
        l_acc[p, nat, :] = jnp.where(first_head, outs[1][0], outs[0][0])
        m_acc[p, nat, :] = jnp.where(first_head, outs[0][1], outs[1][1])
        m_swapped[p, nat, :] = jnp.where(first_head, outs[1][1], outs[0][1])

    for n in range(N_BLOCKS):
        first = 2 * N_PATTERNS if n == 0 else 0
        attend(0, 1, n * Q_BLOCK, True, n * Q_BLOCK, (first, first + 1))

    dil4 = DIL_PATTERNS[1][1]
    run = SEQ // dil4
    for n in range(run // Q_BLOCK):
        for r in range(dil4):
            attend(1, dil4, r * run + n * Q_BLOCK, n > 0, n * (Q_BLOCK * dil4) + r, (2, 3))

    dil16 = DIL_PATTERNS[2][1]
    for r in range(dil16):
        attend(2, dil16, r * Q_BLOCK, False, r, (4, 5))

    rows_per_step = 256

    def blend(i, carry):
        sl = pl.ds(pl.multiple_of(i * rows_per_step, rows_per_step), rows_per_step)
        add = lambda a, b: a + b

        def weighted(m_ref, x_ref):
            ms = [m_ref[p, sl, :] for p in range(N_PATTERNS)]
            mx = functools.reduce(jnp.maximum, ms)
            return functools.reduce(add, [jnp.exp2(ms[p] - mx) * x_ref[p, sl, :] for p in range(N_PATTERNS)])

        num = weighted(m_acc, o_acc)
        den = pltpu.roll(weighted(m_swapped, l_acc), HEAD_DIM, axis=1)
        o_ref[sl, :] = (num * (1.0 / den)).astype(BF16)
        return carry

    lax.fori_loop(0, SEQ // rows_per_step, blend, 0)


def _dilated_attention(slopes, dil3):
    batch = dil3.shape[0]
    n_pairs = DIL_WIDTH // LANES
    q_off, k_off, v_off = 0, n_pairs, 2 * n_pairs
    return pl.pallas_call(
        _dil_attn_kernel,
        grid=(batch, n_pairs),
        in_specs=[
            pl.BlockSpec(memory_space=pltpu.SMEM),
            pl.BlockSpec((None, SEQ, LANES), lambda b, h: (b, 0, q_off + h)),
            pl.BlockSpec((None, SEQ, LANES), lambda b, h: (b, 0, k_off + h)),
            pl.BlockSpec((None, SEQ, LANES), lambda b, h: (b, 0, v_off + h)),
        ],
        out_specs=pl.BlockSpec((None, SEQ, LANES), lambda b, h: (b, 0, h)),
        out_shape=jax.ShapeDtypeStruct((batch, SEQ, DIL_WIDTH), BF16),
        scratch_shapes=[
            pltpu.VMEM((SEQ, LANES), F32),
            pltpu.VMEM((SEQ, LANES), F32),
            pltpu.VMEM((SEQ, LANES), F32),
            pltpu.VMEM((N_PATTERNS, SEQ, LANES), BF16),
            pltpu.VMEM((N_PATTERNS, LANES, KPAD + SEQ), BF16),
            pltpu.VMEM((N_PATTERNS, KPAD + SEQ, LANES), BF16),
            pltpu.VMEM((2 * N_PATTERNS + 2, Q_BLOCK, 2 * Q_BLOCK), F32),
            pltpu.VMEM((N_PATTERNS, SEQ, LANES), F32),
            pltpu.VMEM((N_PATTERNS, SEQ, LANES), F32),
            pltpu.VMEM((N_PATTERNS, SEQ, LANES), F32),
            pltpu.VMEM((N_PATTERNS, SEQ, LANES), F32),
        ],
        compiler_params=pltpu.CompilerParams(
            dimension_semantics=("arbitrary", "arbitrary"), vmem_limit_bytes=VMEM_LIMIT_ATTN),
        name="dilated_attention",
    )(slopes, dil3, dil3, dil3)


def _alibi_slopes():
    all_s = 2.0 ** (-8.0 * np.arange(1, N_ALIBI_HEADS + 1, dtype=np.float32) / N_ALIBI_HEADS)
    diff_idx = np.arange(0, N_ALIBI_HEADS, N_ALIBI_HEADS // N_DIFF_HEADS)
    dil_idx = np.setdiff1d(np.arange(N_ALIBI_HEADS), diff_idx)
    return tuple(float(s) for s in all_s[diff_idx]), jnp.asarray(all_s[dil_idx], F32)


def kernel(x, ffn1_pre_g, ffn1_w_gate, ffn1_w_up, ffn1_w_down, ffn1_post_g, mix_pre_g, w_in, lambda_q1, lambda_k1, lambda_q2, lambda_k2, diff_subln_g, w_out, mix_post_g, ffn2_pre_g, ffn2_w_gate, ffn2_w_up, ffn2_w_down, ffn2_post_g):
    batch, seq, d_model = x.shape
    assert (seq, d_model) == (SEQ, D_MODEL) and ffn1_pre_g.shape[0] == 1
    slopes_d, slopes_l = _alibi_slopes()

    scale = HEAD_DIM ** -0.5 * LOG2E
    col_scale = np.ones((IN_PROJ_WIDTH,), np.float32)
    col_scale[:DIFF_QK_WIDTH] = scale
    q_l0 = 2 * DIFF_QK_WIDTH + DIFF_V_WIDTH
    col_scale[q_l0:q_l0 + DIL_WIDTH] = scale

    l = 0
    row = lambda g: g[l].reshape(1, -1)
    x2d = x.reshape(batch * seq, d_model)
    x1, qk, vt, dil = _ffn1_inproj(
        x2d, row(ffn1_pre_g), ffn1_w_gate[l].astype(BF16), ffn1_w_up[l].astype(BF16),
        ffn1_w_down[l].astype(BF16), row(ffn1_post_g), row(mix_pre_g),
        (w_in[l] * col_scale).astype(BF16), batch)
    o_d = _diff_attention(slopes_d, qk.reshape(batch, seq, 2 * DIFF_QK_WIDTH), vt, row(lambda_q1),
                          row(lambda_k1), row(lambda_q2), row(lambda_k2),
                          diff_subln_g[l].reshape(DIFF_V_DIM, 1))
    o_l = _dilated_attention(slopes_l, dil.reshape(batch, seq, 3 * DIL_WIDTH))
    w_o = w_out[l].astype(BF16)
    out = _outproj_ffn2(
        o_d.reshape(batch * seq, DIFF_V_WIDTH), o_l.reshape(batch * seq, DIL_WIDTH), x1,
        w_o[:DIFF_V_WIDTH], w_o[DIFF_V_WIDTH:], row(mix_post_g), row(ffn2_pre_g),
        ffn2_w_gate[l].astype(BF16), ffn2_w_up[l].astype(BF16), ffn2_w_down[l].astype(BF16),
        row(ffn2_post_g))
    return out.reshape(batch, seq, d_model)
```

```python
import functools
import math

import numpy as np
import jax
import jax.numpy as jnp
from jax import lax
from jax.experimental import pallas as pl
from jax.experimental.pallas import tpu as pltpu

F32 = jnp.float32
BF16 = jnp.bfloat16

D_MODEL = 1024
SEQ = 2048
HEAD_DIM = 64
N_DIFF_HEADS = 4
DIFF_V_DIM = 2 * HEAD_DIM
N_DIL_HEADS = 8
DIL_PATTERNS = ((128, 1), (512, 4), (2048, 16))
Q_BLOCK = 128
D_FF = 2816
RMS_EPS = 1e-6

DIFF_QK_WIDTH = N_DIFF_HEADS * 2 * HEAD_DIM
DIFF_V_WIDTH = N_DIFF_HEADS * DIFF_V_DIM
DIL_WIDTH = N_DIL_HEADS * HEAD_DIM
IN_PROJ_WIDTH = 2 * DIFF_QK_WIDTH + DIFF_V_WIDTH + 3 * DIL_WIDTH
DIL_OFFSET = 2 * DIFF_QK_WIDTH + DIFF_V_WIDTH
N_ALIBI_HEADS = N_DIFF_HEADS + N_DIL_HEADS
LAM_INIT = 0.8 - 0.6 * math.exp(-0.3 * 0)

LANES = 128
NEG_BIG = -1e30
VMEM_LIMIT_FFN = 56 * 1024 * 1024
VMEM_LIMIT_ATTN = 52 * 1024 * 1024

TM_IN = 512
TM_OUT = 1024
TM_SUB = 256
TQ = 256
TK = 256


def _rms(x, g):
    ms = jnp.mean(x * x, axis=-1, keepdims=True)
    return x * lax.rsqrt(ms + RMS_EPS) * g


def _row_groups(tm):
    return [slice(r * TM_SUB, (r + 1) * TM_SUB) for r in range(tm // TM_SUB)]


def _swiglu_half_step(xs, pre_g, wg_ref, wu_ref, wd_ref, post_g):
    xb = [_rms(x, pre_g).astype(BF16) for x in xs]
    gu = [(jnp.dot(b, wg_ref[...], preferred_element_type=F32),
           jnp.dot(b, wu_ref[...], preferred_element_type=F32)) for b in xb]
    ys = []
    for g, u in gu:
        h = (g * (1.0 / (1.0 + jnp.exp(-g))) * u).astype(BF16)
        ys.append(jnp.dot(h, wd_ref[...], preferred_element_type=F32))
    return [x + 0.5 * _rms(y, post_g) for x, y in zip(xs, ys)]


def _ffn1_inproj_kernel(x_ref, pre_g, wg, wu, wd, post_g, mix_g, w_in, x1_ref, qk_ref, vt_ref, dil_ref):
    groups = _row_groups(TM_IN)
    x1s = _swiglu_half_step([x_ref[rows, :] for rows in groups], pre_g[...], wg, wu, wd, post_g[...])
    for rows, x1 in zip(groups, x1s):
        x1_ref[rows, :] = x1
        hb = _rms(x1, mix_g[...]).astype(BF16)
        proj = jnp.dot(hb, w_in[...], preferred_element_type=F32)
        qk_ref[rows, :] = proj[:, :2 * DIFF_QK_WIDTH].astype(BF16)
        v_d = proj[:, 2 * DIFF_QK_WIDTH:DIL_OFFSET]
        vt_ref[:, rows] = v_d.T.astype(BF16)
        dil_ref[rows, :] = proj[:, DIL_OFFSET:]


def _const_spec(shape):
    nd = len(shape)
    return pl.BlockSpec(shape, lambda *_: (0,) * nd, pipeline_mode=pl.Buffered(1))


def _ffn1_inproj(x2d, pre_g, wg, wu, wd, post_g, mix_g, w_in, batch):
    t = x2d.shape[0]
    tm = TM_IN
    tiles_per_seq = SEQ // tm
    row = lambda i: (i, 0)
    return pl.pallas_call(
        _ffn1_inproj_kernel,
        grid=(t // tm,),
        in_specs=[
            pl.BlockSpec((tm, D_MODEL), row),
            _const_spec((1, D_MODEL)),
            _const_spec((D_MODEL, D_FF)),
            _const_spec((D_MODEL, D_FF)),
            _const_spec((D_FF, D_MODEL)),
            _const_spec((1, D_MODEL)),
            _const_spec((1, D_MODEL)),
            _const_spec((D_MODEL, IN_PROJ_WIDTH)),
        ],
        out_specs=[
            pl.BlockSpec((tm, D_MODEL), row),
            pl.BlockSpec((tm, 2 * DIFF_QK_WIDTH), row),
            pl.BlockSpec((None, DIFF_V_WIDTH, tm), lambda i: (i // tiles_per_seq, 0, i % tiles_per_seq)),
            pl.BlockSpec((tm, 3 * DIL_WIDTH), row),
        ],
        out_shape=[
            jax.ShapeDtypeStruct((t, D_MODEL), F32),
            jax.ShapeDtypeStruct((t, 2 * DIFF_QK_WIDTH), BF16),
            jax.ShapeDtypeStruct((batch, DIFF_V_WIDTH, SEQ), BF16),
            jax.ShapeDtypeStruct((t, 3 * DIL_WIDTH), F32),
        ],
        compiler_params=pltpu.CompilerParams(
            dimension_semantics=("arbitrary",), vmem_limit_bytes=VMEM_LIMIT_FFN),
        name="ffn1_inproj",
    )(x2d, pre_g, wg, wu, wd, post_g, mix_g, w_in)


def _outproj_ffn2_kernel(od_ref, ol_ref, x1_ref, wo_d, wo_l, mix_post_g, pre_g, wg, wu, wd, post_g, out_ref):
    groups = _row_groups(TM_OUT)
    x2s = []
    for rows in groups:
        h = (jnp.dot(od_ref[rows, :], wo_d[...], preferred_element_type=F32)
             + jnp.dot(ol_ref[rows, :], wo_l[...], preferred_element_type=F32))
        x2s.append(x1_ref[rows, :] + _rms(h, mix_post_g[...]))
    outs = _swiglu_half_step(x2s, pre_g[...], wg, wu, wd, post_g[...])
    for rows, out in zip(groups, outs):
        out_ref[rows, :] = out


def _outproj_ffn2(od, ol, x1, wo_d, wo_l, mix_post_g, pre_g, wg, wu, wd, post_g):
    t = x1.shape[0]
    tm = TM_OUT
    row = lambda i: (i, 0)
    return pl.pallas_call(
        _outproj_ffn2_kernel,
        grid=(t // tm,),
        in_specs=[
            pl.BlockSpec((tm, DIFF_V_WIDTH), row),
            pl.BlockSpec((tm, DIL_WIDTH), row),
            pl.BlockSpec((tm, D_MODEL), row),
            _const_spec((DIFF_V_WIDTH, D_MODEL)),
            _const_spec((DIL_WIDTH, D_MODEL)),
            _const_spec((1, D_MODEL)),
            _const_spec((1, D_MODEL)),
            _const_spec((D_MODEL, D_FF)),
            _const_spec((D_MODEL, D_FF)),
            _const_spec((D_FF, D_MODEL)),
            _const_spec((1, D_MODEL)),
        ],
        out_specs=pl.BlockSpec((tm, D_MODEL), row),
        out_shape=jax.ShapeDtypeStruct((t, D_MODEL), F32),
        compiler_params=pltpu.CompilerParams(
            dimension_semantics=("arbitrary",), vmem_limit_bytes=VMEM_LIMIT_FFN),
        name="outproj_ffn2",
    )(od, ol, x1, wo_d, wo_l, mix_post_g, pre_g, wg, wu, wd, post_g)


_NT = (((1,), (1,)), ((), ()))
N_MAPS = 2 * N_DIFF_HEADS
ONES_ROWS = 16
SLOPE_TERMS = 3
N_QT = SEQ // TQ
N_TILES = N_QT * (N_QT + 1) // 2
TILES_PER_TRIP = 2
LOG2E = math.log2(math.e)


def _bf16_terms(x, n_terms):
    terms, rest = [], np.float32(x)
    for _ in range(n_terms):
        t = np.float32(np.asarray(rest, dtype=BF16))
        terms.append(float(t))
        rest = np.float32(rest - t)
    return terms


def _diff_attn_kernel(q_ref, k_ref, vt_ref, lq1, lk1, lq2, lk2, g_ref, o_ref,
                      kp, qp, vta, masks, s_buf, p_buf, acc, *, slopes):
    lam = (jnp.exp(jnp.sum(lq1[...] * lk1[...], axis=-1, keepdims=True))
           - jnp.exp(jnp.sum(lq2[...] * lk2[...], axis=-1, keepdims=True)) + LAM_INIT)
    lane = lax.broadcasted_iota(jnp.int32, (1, LANES), 1)
    first_map = lane < HEAD_DIM

    def in_lanes(base, values):
        row = jnp.zeros((1, LANES), F32)
        for i, v in enumerate(values):
            row = jnp.where(lane == base + i, v, row)
        return row.astype(BF16)

    slope_rows = [(in_lanes(HEAD_DIM, _bf16_terms(s * LOG2E, SLOPE_TERMS)),
                   in_lanes(0, _bf16_terms(s * LOG2E, SLOPE_TERMS))) for s in slopes]
    key_idx = lax.broadcasted_iota(jnp.int32, (TK, LANES), 0).astype(F32)
    idx_first = jnp.where((lane >= HEAD_DIM) & (lane < HEAD_DIM + SLOPE_TERMS), key_idx, 0.0).astype(BF16)
    idx_second = jnp.where(lane < SLOPE_TERMS, key_idx, 0.0).astype(BF16)

    def augment(i, carry):
        rows = pl.ds(pl.multiple_of(i * TK, TK), TK)
        for h in range(N_DIFF_HEADS):
            k = k_ref[rows, h * LANES:(h + 1) * LANES]
            kp[2 * h, rows, :] = jnp.where(first_map, k, idx_first)
            kp[2 * h + 1, rows, :] = jnp.where(first_map, idx_second, k)
            q = q_ref[rows, h * LANES:(h + 1) * LANES]
            qp[2 * h, rows, :] = jnp.where(first_map, q, slope_rows[h][0])
            qp[2 * h + 1, rows, :] = jnp.where(first_map, slope_rows[h][1], q)
        return carry

    lax.fori_loop(0, SEQ // TK, augment, 0)
    for h in range(N_DIFF_HEADS):
        vta[h, 0:DIFF_V_DIM, :] = vt_ref[h * DIFF_V_DIM:(h + 1) * DIFF_V_DIM, :]
        vta[h, DIFF_V_DIM:, :] = jnp.ones((ONES_ROWS, SEQ), BF16)
    causal = (lax.broadcasted_iota(jnp.int32, (TK, TQ), 1)
              >= lax.broadcasted_iota(jnp.int32, (TK, TQ), 0))
    masks[0] = jnp.zeros((TK, TQ), F32)
    masks[1] = jnp.where(causal, 0.0, NEG_BIG)

    def clear(qi, carry):
        acc[qi] = jnp.zeros((N_MAPS, DIFF_V_DIM + ONES_ROWS, TQ), F32)
        return carry

    lax.fori_loop(0, N_QT, clear, 0)

    def score_matmuls(qi, t, slot):
        qs = pl.multiple_of(qi * TQ, TQ)
        ks = pl.multiple_of((qi - t) * TK, TK)
        mask = masks[(t == 0).astype(jnp.int32)]
        tile_max = []
        for c in range(N_MAPS):
            s = lax.dot_general(kp[c, pl.ds(ks, TK), :], qp[c, pl.ds(qs, TQ), :], _NT,
                                preferred_element_type=F32) + mask
            s_buf[slot, c] = s
            tile_max.append(jnp.max(s, axis=0, keepdims=True))
        return tuple(tile_max)

    def softmax_step(t, slot, tile_max, maxima):
        on_diagonal = t == 0
        dist = jnp.asarray(t * TK, F32)
        new_m, alphas = [], []
        for c in range(N_MAPS):
            coff = (slopes[c // 2] * LOG2E) * dist
            m = jnp.where(on_diagonal, NEG_BIG, maxima[c])
            n = jnp.maximum(m, tile_max[c] - coff)
            new_m.append(n)
            alphas.append(jnp.exp2(m - n))
            p_buf[slot, c] = jnp.exp2(s_buf[slot, c] - (n + coff)).astype(BF16)
        return tuple(new_m), tuple(alphas)

    def value_matmuls(qi, t, slot, alphas):
        ks = pl.multiple_of((qi - t) * TK, TK)
        for c in range(N_MAPS):
            pv = jnp.dot(vta[c // 2, :, pl.ds(ks, TK)], p_buf[slot, c], preferred_element_type=F32)
            acc[qi, c] = alphas[c] * acc[qi, c] + pv

    def next_tile(qi, t):
        last = t == qi
        nq = jnp.where(last, qi + 1, qi)
        nt = jnp.where(last, 0, t + 1)
        return jnp.minimum(nq, N_QT - 1), nt

    zero = jnp.int32(0)
    first_max = score_matmuls(zero, zero, 0)
    p_buf[1] = jnp.zeros((N_MAPS, TK, TQ), BF16)
    row = lambda v: tuple(jnp.full((1, TQ), v, F32) for _ in range(N_MAPS))

    def several_tiles(i, carry):
        qi, t, qi_prev, t_prev, tile_max, maxima, alphas = carry
        for u in range(TILES_PER_TRIP):
            slot = u % 2
            q_next, t_next = next_tile(qi, t)
            next_max = score_matmuls(q_next, t_next, 1 - slot)
            maxima, new_alphas = softmax_step(t, slot, tile_max, maxima)
            value_matmuls(qi_prev, t_prev, 1 - slot, alphas)
            qi_prev, t_prev, alphas = qi, t, new_alphas
            qi, t, tile_max = q_next, t_next, next_max
        return qi, t, qi_prev, t_prev, tile_max, maxima, alphas

    assert TILES_PER_TRIP % 2 == 0 and N_TILES % TILES_PER_TRIP == 0
    carry = lax.fori_loop(0, N_TILES // TILES_PER_TRIP, several_tiles,
                          (zero, zero, zero, zero, first_max, row(0.0), row(1.0)))
    _, _, qi_prev, t_prev, _, _, alphas = carry
    value_matmuls(qi_prev, t_prev, 1, alphas)

    def finish(qi, carry):
        qs = pl.multiple_of(qi * TQ, TQ)
        for h in range(N_DIFF_HEADS):
            a1, a2 = acc[qi, 2 * h], acc[qi, 2 * h + 1]
            l1, l2 = a1[DIFF_V_DIM:DIFF_V_DIM + 1, :], a2[DIFF_V_DIM:DIFF_V_DIM + 1, :]
            o = a1[:DIFF_V_DIM, :] * (1.0 / l1) - lam * (a2[:DIFF_V_DIM, :] * (1.0 / l2))
            ms = jnp.mean(o * o, axis=0, keepdims=True)
            y = o * lax.rsqrt(ms + RMS_EPS) * g_ref[...] * (1.0 - LAM_INIT)
            o_ref[pl.ds(qs, TQ), h * LANES:(h + 1) * LANES] = y.T.astype(BF16)
        return carry

    lax.fori_loop(0, N_QT, finish, 0)


def _diff_attention(slopes, qk3, vt, lq1, lk1, lq2, lk2, g_col):
    batch = qk3.shape[0]
    vec = pl.BlockSpec((1, HEAD_DIM), lambda b: (0, 0))
    return pl.pallas_call(
        functools.partial(_diff_attn_kernel, slopes=slopes),
        grid=(batch,),
        in_specs=[
            pl.BlockSpec((None, SEQ, DIFF_QK_WIDTH), lambda b: (b, 0, 0)),
            pl.BlockSpec((None, SEQ, DIFF_QK_WIDTH), lambda b: (b, 0, 1)),
            pl.BlockSpec((None, DIFF_V_WIDTH, SEQ), lambda b: (b, 0, 0)),
            vec, vec, vec, vec,
            pl.BlockSpec((DIFF_V_DIM, 1), lambda b: (0, 0)),
        ],
        out_specs=pl.BlockSpec((None, SEQ, DIFF_V_WIDTH), lambda b: (b, 0, 0)),
        out_shape=jax.ShapeDtypeStruct((batch, SEQ, DIFF_V_WIDTH), BF16),
        scratch_shapes=[
            pltpu.VMEM((N_MAPS, SEQ, LANES), BF16),
            pltpu.VMEM((N_MAPS, SEQ, LANES), BF16),
            pltpu.VMEM((N_DIFF_HEADS, DIFF_V_DIM + ONES_ROWS, SEQ), BF16),
            pltpu.VMEM((2, TK, TQ), F32),
            pltpu.VMEM((2, N_MAPS, TK, TQ), F32),
            pltpu.VMEM((2, N_MAPS, TK, TQ), BF16),
            pltpu.VMEM((N_QT, N_MAPS, DIFF_V_DIM + ONES_ROWS, TQ), F32),
        ],
        compiler_params=pltpu.CompilerParams(
            dimension_semantics=("arbitrary",), vmem_limit_bytes=VMEM_LIMIT_ATTN),
        name="diff_attention",
    )(qk3, qk3, vt, lq1, lk1, lq2, lk2, g_col)


N_PATTERNS = len(DIL_PATTERNS)
N_WIN = Q_BLOCK
N_BLOCKS = SEQ // Q_BLOCK
KPAD = Q_BLOCK


def _dil_attn_kernel(slopes_ref, q_ref, k_ref, v_ref, o_ref,
                     q4, k4, v4, qq, kt, vv, bias, o_acc, l_acc, m_acc, m_swapped):
    hp = pl.program_id(1)
    lane = lax.broadcasted_iota(jnp.int32, (1, LANES), 1)
    first_head = lane < HEAD_DIM

    kt[0, :, 0:KPAD] = jnp.zeros((LANES, KPAD), BF16)
    vv[0, 0:KPAD, :] = jnp.zeros((KPAD, LANES), BF16)

    col = lax.broadcasted_iota(jnp.int32, (Q_BLOCK, 2 * Q_BLOCK), 1)
    dsub = Q_BLOCK + lax.broadcasted_iota(jnp.int32, (Q_BLOCK, 2 * Q_BLOCK), 0) - col
    valid = (dsub >= 0) & (dsub <= N_WIN)
    dsub_f = dsub.astype(F32)
    for p, (_, dil) in enumerate(DIL_PATTERNS):
        for hh in range(2):
            slope = slopes_ref[2 * hp + hh]
            b = jnp.where(valid, (-slope * (LOG2E * dil)) * dsub_f, NEG_BIG)
            bias[2 * p + hh] = b
            if p == 0:
                bias[2 * N_PATTERNS + hh] = jnp.where(col >= Q_BLOCK, b, NEG_BIG)

    def emit(p, block, q, k, v):
        dst = block * Q_BLOCK
        qq[p, pl.ds(dst, Q_BLOCK), :] = q.astype(BF16)
        kt[p, :, pl.ds(KPAD + dst, Q_BLOCK)] = k.T.astype(BF16)
        vv[p, pl.ds(KPAD + dst, Q_BLOCK), :] = v.astype(BF16)

    step = DIL_PATTERNS[1][1]
    assert DIL_PATTERNS[2][1] == step * step and N_BLOCKS == step * step
    run4 = SEQ // step

    def prepare_dense(block):
        rows = pl.ds(block * Q_BLOCK, Q_BLOCK)
        emit(0, block, q_ref[rows, :], k_ref[rows, :], v_ref[rows, :])

    def prepare_first_level(block):
        r4, n = divmod(block, step)
        rows = pl.ds(block * Q_BLOCK, Q_BLOCK)
        src = pl.ds(n * (Q_BLOCK * step) + r4, Q_BLOCK, stride=step)
        q, k, v = q_ref[src, :], k_ref[src, :], v_ref[src, :]
        q4[rows, :] = q
        k4[rows, :] = k
        v4[rows, :] = v
        emit(1, block, q, k, v)

    def prepare_second_level(r16):
        a, r4 = divmod(r16, step)
        src = pl.ds(r4 * run4 + a, Q_BLOCK, stride=step)
        emit(2, r16, q4[src, :], k4[src, :], v4[src, :])

    def attend(p, dil, q_row, with_prev, nat_start, bias_idx):
        n_keys = 2 * Q_BLOCK if with_prev else Q_BLOCK
        key_row = q_row + (KPAD - Q_BLOCK if with_prev else KPAD)
        if not isinstance(q_row, int):
            q_row = pl.multiple_of(q_row, Q_BLOCK)
            key_row = pl.multiple_of(key_row, Q_BLOCK)
        k_t = kt[p, :, pl.ds(key_row, n_keys)]
        q = qq[p, pl.ds(q_row, Q_BLOCK), :]
        v = vv[p, pl.ds(key_row, n_keys), :]
        zero, one = jnp.zeros((), BF16), jnp.ones((), BF16)
        outs = []
        for hh in range(2):
            qh = jnp.where(first_head, q, zero) if hh == 0 else jnp.where(first_head, zero, q)
            s = jnp.dot(qh, k_t, preferred_element_type=F32)
            if with_prev:
                s = s + bias[bias_idx[hh]]
            else:
                s = s + bias[bias_idx[hh], :, Q_BLOCK:]
            m = jnp.max(s, axis=1, keepdims=True)
            e = jnp.exp2(s - m).astype(BF16)
            vh = jnp.where(first_head, v, one) if hh == 0 else jnp.where(first_head, one, v)
            ov = jnp.dot(e, vh, preferred_element_type=F32)
            outs.append((ov, m))
        if dil == 1:
            nat = pl.ds(nat_start, Q_BLOCK)
        else:
            nat = pl.ds(nat_start, Q_BLOCK, stride=dil)
        o_acc[p, nat, :] = jnp.where(first_head, outs[0][0], outs[1][0])
        l_acc[p, nat, :] = jnp.where(first_head, outs[1][0], outs[0][0])
        m_acc[p, nat, :] = jnp.where(first_head, outs[0][1], outs[1][1])
        m_swapped[p, nat, :] = jnp.where(first_head, outs[1][1], outs[0][1])

    for block in range(N_BLOCKS):
        prepare_dense(block)

    for n in range(N_BLOCKS):
        first = 2 * N_PATTERNS if n == 0 else 0
        attend(0, 1, n * Q_BLOCK, True, n * Q_BLOCK, (first, first + 1))
        prepare_first_level(n)

    dil4 = DIL_PATTERNS[1][1]
    run = SEQ // dil4
    for n in range(run // Q_BLOCK):
        for r in range(dil4):
            attend(1, dil4, r * run + n * Q_BLOCK, n > 0, n * (Q_BLOCK * dil4) + r, (2, 3))
            prepare_second_level(n * dil4 + r)

    dil16 = DIL_PATTERNS[2][1]
    for r in range(dil16):
        attend(2, dil16, r * Q_BLOCK, False, r, (4, 5))

    rows_per_step = 256

    def blend(i, carry):
        sl = pl.ds(pl.multiple_of(i * rows_per_step, rows_per_step), rows_per_step)
        add = lambda a, b: a + b

        def weighted(m_ref, x_ref):
            ms = [m_ref[p, sl, :] for p in range(N_PATTERNS)]
            mx = functools.reduce(jnp.maximum, ms)
            return functools.reduce(add, [jnp.exp2(ms[p] - mx) * x_ref[p, sl, :] for p in range(N_PATTERNS)])

        num = weighted(m_acc, o_acc)
        den = pltpu.roll(weighted(m_swapped, l_acc), HEAD_DIM, axis=1)
        o_ref[sl, :] = (num * (1.0 / den)).astype(BF16)
        return carry

    lax.fori_loop(0, SEQ // rows_per_step, blend, 0)


def _dilated_attention(slopes, dil3):
    batch = dil3.shape[0]
    n_pairs = DIL_WIDTH // LANES
    q_off, k_off, v_off = 0, n_pairs, 2 * n_pairs
    return pl.pallas_call(
        _dil_attn_kernel,
        grid=(batch, n_pairs),
        in_specs=[
            pl.BlockSpec(memory_space=pltpu.SMEM),
            pl.BlockSpec((None, SEQ, LANES), lambda b, h: (b, 0, q_off + h)),
            pl.BlockSpec((None, SEQ, LANES), lambda b, h: (b, 0, k_off + h)),
            pl.BlockSpec((None, SEQ, LANES), lambda b, h: (b, 0, v_off + h)),
        ],
        out_specs=pl.BlockSpec((None, SEQ, LANES), lambda b, h: (b, 0, h)),
        out_shape=jax.ShapeDtypeStruct((batch, SEQ, DIL_WIDTH), BF16),
        scratch_shapes=[
            pltpu.VMEM((SEQ, LANES), F32),
            pltpu.VMEM((SEQ, LANES), F32),
            pltpu.VMEM((SEQ, LANES), F32),
            pltpu.VMEM((N_PATTERNS, SEQ, LANES), BF16),
            pltpu.VMEM((N_PATTERNS, LANES, KPAD + SEQ), BF16),
            pltpu.VMEM((N_PATTERNS, KPAD + SEQ, LANES), BF16),
            pltpu.VMEM((2 * N_PATTERNS + 2, Q_BLOCK, 2 * Q_BLOCK), F32),
            pltpu.VMEM((N_PATTERNS, SEQ, LANES), F32),
            pltpu.VMEM((N_PATTERNS, SEQ, LANES), F32),
            pltpu.VMEM((N_PATTERNS, SEQ, LANES), F32),
            pltpu.VMEM((N_PATTERNS, SEQ, LANES), F32),
        ],
        compiler_params=pltpu.CompilerParams(
            dimension_semantics=("arbitrary", "arbitrary"), vmem_limit_bytes=VMEM_LIMIT_ATTN),
        name="dilated_attention",
    )(slopes, dil3, dil3, dil3)


def _alibi_slopes():
    all_s = 2.0 ** (-8.0 * np.arange(1, N_ALIBI_HEADS + 1, dtype=np.float32) / N_ALIBI_HEADS)
    diff_idx = np.arange(0, N_ALIBI_HEADS, N_ALIBI_HEADS // N_DIFF_HEADS)
    dil_idx = np.setdiff1d(np.arange(N_ALIBI_HEADS), diff_idx)
    return tuple(float(s) for s in all_s[diff_idx]), jnp.asarray(all_s[dil_idx], F32)


def kernel(x, ffn1_pre_g, ffn1_w_gate, ffn1_w_up, ffn1_w_down, ffn1_post_g, mix_pre_g, w_in, lambda_q1, lambda_k1, lambda_q2, lambda_k2, diff_subln_g, w_out, mix_post_g, ffn2_pre_g, ffn2_w_gate, ffn2_w_up, ffn2_w_down, ffn2_post_g):
    batch, seq, d_model = x.shape
    assert (seq, d_model) == (SEQ, D_MODEL) and ffn1_pre_g.shape[0] == 1
    slopes_d, slopes_l = _alibi_slopes()

    scale = HEAD_DIM ** -0.5 * LOG2E
    col_scale = np.ones((IN_PROJ_WIDTH,), np.float32)
    col_scale[:DIFF_QK_WIDTH] = scale
    q_l0 = 2 * DIFF_QK_WIDTH + DIFF_V_WIDTH
    col_scale[q_l0:q_l0 + DIL_WIDTH] = scale

    l = 0
    row = lambda g: g[l].reshape(1, -1)
    x2d = x.reshape(batch * seq, d_model)
    x1, qk, vt, dil = _ffn1_inproj(
        x2d, row(ffn1_pre_g), ffn1_w_gate[l].astype(BF16), ffn1_w_up[l].astype(BF16),
        ffn1_w_down[l].astype(BF16), row(ffn1_post_g), row(mix_pre_g),
        (w_in[l] * col_scale).astype(BF16), batch)
    o_d = _diff_attention(slopes_d, qk.reshape(batch, seq, 2 * DIFF_QK_WIDTH), vt, row(lambda_q1),
                          row(lambda_k1), row(lambda_q2), row(lambda_k2),
                          diff_subln_g[l].reshape(DIFF_V_DIM, 1))
    o_l = _dilated_attention(slopes_l, dil.reshape(batch, seq, 3 * DIL_WIDTH))
    w_o = w_out[l].astype(BF16)
    out = _outproj_ffn2(
        o_d.reshape(batch * seq, DIFF_V_WIDTH), o_l.reshape(batch * seq, DIL_WIDTH), x1,
        w_o[:DIFF_V_WIDTH], w_o[DIFF_V_WIDTH:], row(mix_post_g), row(ffn2_pre_g),
        ffn2_w_gate[l].astype(BF16), ffn2_w_up[l].astype(BF16), ffn2_w_down[l].astype(BF16),
        row(ffn2_post_g))
    return out.reshape(batch, seq, d_model)
```

```python
import functools
import math

import numpy as np
import jax
import jax.numpy as jnp
from jax import lax
from jax.experimental import pallas as pl
from jax.experimental.pallas import tpu as pltpu

F32 = jnp.float32
BF16 = jnp.bfloat16

D_MODEL = 1024
SEQ = 2048
HEAD_DIM = 64
N_DIFF_HEADS = 4
DIFF_V_DIM = 2 * HEAD_DIM
N_DIL_HEADS = 8
DIL_PATTERNS = ((128, 1), (512, 4), (2048, 16))
Q_BLOCK = 128
D_FF = 2816
RMS_EPS = 1e-6

DIFF_QK_WIDTH = N_DIFF_HEADS * 2 * HEAD_DIM
DIFF_V_WIDTH = N_DIFF_HEADS * DIFF_V_DIM
DIL_WIDTH = N_DIL_HEADS * HEAD_DIM
IN_PROJ_WIDTH = 2 * DIFF_QK_WIDTH + DIFF_V_WIDTH + 3 * DIL_WIDTH
DIL_OFFSET = 2 * DIFF_QK_WIDTH + DIFF_V_WIDTH
N_ALIBI_HEADS = N_DIFF_HEADS + N_DIL_HEADS
LAM_INIT = 0.8 - 0.6 * math.exp(-0.3 * 0)

LANES = 128
NEG_BIG = -1e30
VMEM_LIMIT_FFN = 56 * 1024 * 1024
VMEM_LIMIT_ATTN = 52 * 1024 * 1024

TM_IN = 512
TM_OUT = 1024
TM_SUB = 256
TQ = 256
TK = 256


def _rms(x, g):
    ms = jnp.mean(x * x, axis=-1, keepdims=True)
    return x * lax.rsqrt(ms + RMS_EPS) * g


def _row_groups(tm):
    return [slice(r * TM_SUB, (r + 1) * TM_SUB) for r in range(tm // TM_SUB)]


def _swiglu_half_step(xs, pre_g, wg_ref, wu_ref, wd_ref, post_g):
    xb = [_rms(x, pre_g).astype(BF16) for x in xs]
    gu = [(jnp.dot(b, wg_ref[...], preferred_element_type=F32),
           jnp.dot(b, wu_ref[...], preferred_element_type=F32)) for b in xb]
    ys = []
    for g, u in gu:
        h = (g * (1.0 / (1.0 + jnp.exp(-g))) * u).astype(BF16)
        ys.append(jnp.dot(h, wd_ref[...], preferred_element_type=F32))
    return [x + 0.5 * _rms(y, post_g) for x, y in zip(xs, ys)]


def _ffn1_inproj_kernel(x_ref, pre_g, wg, wu, wd, post_g, mix_g, w_in, x1_ref, qk_ref, vt_ref, dil_ref):
    groups = _row_groups(TM_IN)
    x1s = _swiglu_half_step([x_ref[rows, :] for rows in groups], pre_g[...], wg, wu, wd, post_g[...])
    for rows, x1 in zip(groups, x1s):
        x1_ref[rows, :] = x1
        hb = _rms(x1, mix_g[...]).astype(BF16)
        proj = jnp.dot(hb, w_in[...], preferred_element_type=F32)
        qk_ref[rows, :] = proj[:, :2 * DIFF_QK_WIDTH].astype(BF16)
        v_d = proj[:, 2 * DIFF_QK_WIDTH:DIL_OFFSET]
        vt_ref[:, rows] = v_d.T.astype(BF16)
        dil_ref[rows, :] = proj[:, DIL_OFFSET:]


def _const_spec(shape):
    nd = len(shape)
    return pl.BlockSpec(shape, lambda *_: (0,) * nd, pipeline_mode=pl.Buffered(1))


def _ffn1_inproj(x2d, pre_g, wg, wu, wd, post_g, mix_g, w_in, batch):
    t = x2d.shape[0]
    tm = TM_IN
    tiles_per_seq = SEQ // tm
    row = lambda i: (i, 0)
    return pl.pallas_call(
        _ffn1_inproj_kernel,
        grid=(t // tm,),
        in_specs=[
            pl.BlockSpec((tm, D_MODEL), row),
            _const_spec((1, D_MODEL)),
            _const_spec((D_MODEL, D_FF)),
            _const_spec((D_MODEL, D_FF)),
            _const_spec((D_FF, D_MODEL)),
            _const_spec((1, D_MODEL)),
            _const_spec((1, D_MODEL)),
            _const_spec((D_MODEL, IN_PROJ_WIDTH)),
        ],
        out_specs=[
            pl.BlockSpec((tm, D_MODEL), row),
            pl.BlockSpec((tm, 2 * DIFF_QK_WIDTH), row),
            pl.BlockSpec((None, DIFF_V_WIDTH, tm), lambda i: (i // tiles_per_seq, 0, i % tiles_per_seq)),
            pl.BlockSpec((tm, 3 * DIL_WIDTH), row),
        ],
        out_shape=[
            jax.ShapeDtypeStruct((t, D_MODEL), F32),
            jax.ShapeDtypeStruct((t, 2 * DIFF_QK_WIDTH), BF16),
            jax.ShapeDtypeStruct((batch, DIFF_V_WIDTH, SEQ), BF16),
            jax.ShapeDtypeStruct((t, 3 * DIL_WIDTH), F32),
        ],
        compiler_params=pltpu.CompilerParams(
            dimension_semantics=("arbitrary",), vmem_limit_bytes=VMEM_LIMIT_FFN),
        name="ffn1_inproj",
    )(x2d, pre_g, wg, wu, wd, post_g, mix_g, w_in)


def _outproj_ffn2_kernel(od_ref, ol_ref, x1_ref, wo_d, wo_l, mix_post_g, pre_g, wg, wu, wd, post_g, out_ref):
    groups = _row_groups(TM_OUT)
    x2s = []
    for rows in groups:
        h = (jnp.dot(od_ref[rows, :], wo_d[...], preferred_element_type=F32)
             + jnp.dot(ol_ref[rows, :], wo_l[...], preferred_element_type=F32))
        x2s.append(x1_ref[rows, :] + _rms(h, mix_post_g[...]))
    outs = _swiglu_half_step(x2s, pre_g[...], wg, wu, wd, post_g[...])
    for rows, out in zip(groups, outs):
        out_ref[rows, :] = out


def _outproj_ffn2(od, ol, x1, wo_d, wo_l, mix_post_g, pre_g, wg, wu, wd, post_g):
    t = x1.shape[0]
    tm = TM_OUT
    row = lambda i: (i, 0)
    return pl.pallas_call(
        _outproj_ffn2_kernel,
        grid=(t // tm,),
        in_specs=[
            pl.BlockSpec((tm, DIFF_V_WIDTH), row),
            pl.BlockSpec((tm, DIL_WIDTH), row),
            pl.BlockSpec((tm, D_MODEL), row),
            _const_spec((DIFF_V_WIDTH, D_MODEL)),
            _const_spec((DIL_WIDTH, D_MODEL)),
            _const_spec((1, D_MODEL)),
            _const_spec((1, D_MODEL)),
            _const_spec((D_MODEL, D_FF)),
            _const_spec((D_MODEL, D_FF)),
            _const_spec((D_FF, D_MODEL)),
            _const_spec((1, D_MODEL)),
        ],
        out_specs=pl.BlockSpec((tm, D_MODEL), row),
        out_shape=jax.ShapeDtypeStruct((t, D_MODEL), F32),
        compiler_params=pltpu.CompilerParams(
            dimension_semantics=("arbitrary",), vmem_limit_bytes=VMEM_LIMIT_FFN),
        name="outproj_ffn2",
    )(od, ol, x1, wo_d, wo_l, mix_post_g, pre_g, wg, wu, wd, post_g)


_NT = (((1,), (1,)), ((), ()))
N_MAPS = 2 * N_DIFF_HEADS
ONES_ROWS = 16
SLOPE_TERMS = 3
N_QT = SEQ // TQ
N_TILES = N_QT * (N_QT + 1) // 2
TILES_PER_TRIP = 2
LOG2E = math.log2(math.e)


def _bf16_terms(x, n_terms):
    terms, rest = [], np.float32(x)
    for _ in range(n_terms):
        t = np.float32(np.asarray(rest, dtype=BF16))
        terms.append(float(t))
        rest = np.float32(rest - t)
    return terms


def _diff_attn_kernel(q_ref, k_ref, vt_ref, lq1, lk1, lq2, lk2, g_ref, o_ref,
                      kp, qp, vta, masks, s_buf, p_buf, acc, *, slopes):
    lam = (jnp.exp(jnp.sum(lq1[...] * lk1[...], axis=-1, keepdims=True))
           - jnp.exp(jnp.sum(lq2[...] * lk2[...], axis=-1, keepdims=True)) + LAM_INIT)
    lane = lax.broadcasted_iota(jnp.int32, (1, LANES), 1)
    first_map = lane < HEAD_DIM

    def in_lanes(base, values):
        row = jnp.zeros((1, LANES), F32)
        for i, v in enumerate(values):
            row = jnp.where(lane == base + i, v, row)
        return row.astype(BF16)

    slope_rows = [(in_lanes(HEAD_DIM, _bf16_terms(s * LOG2E, SLOPE_TERMS)),
                   in_lanes(0, _bf16_terms(s * LOG2E, SLOPE_TERMS))) for s in slopes]
    key_idx = lax.broadcasted_iota(jnp.int32, (TK, LANES), 0).astype(F32)
    idx_first = jnp.where((lane >= HEAD_DIM) & (lane < HEAD_DIM + SLOPE_TERMS), key_idx, 0.0).astype(BF16)
    idx_second = jnp.where(lane < SLOPE_TERMS, key_idx, 0.0).astype(BF16)

    def augment(i, carry):
        rows = pl.ds(pl.multiple_of(i * TK, TK), TK)
        for h in range(N_DIFF_HEADS):
            k = k_ref[rows, h * LANES:(h + 1) * LANES]
            kp[2 * h, rows, :] = jnp.where(first_map, k, idx_first)
            kp[2 * h + 1, rows, :] = jnp.where(first_map, idx_second, k)
            q = q_ref[rows, h * LANES:(h + 1) * LANES]
            qp[2 * h, rows, :] = jnp.where(first_map, q, slope_rows[h][0])
            qp[2 * h + 1, rows, :] = jnp.where(first_map, slope_rows[h][1], q)
        return carry

    lax.fori_loop(0, SEQ // TK, augment, 0)
    for h in range(N_DIFF_HEADS):
        vta[h, 0:DIFF_V_DIM, :] = vt_ref[h * DIFF_V_DIM:(h + 1) * DIFF_V_DIM, :]
        vta[h, DIFF_V_DIM:, :] = jnp.ones((ONES_ROWS, SEQ), BF16)
    causal = (lax.broadcasted_iota(jnp.int32, (TK, TQ), 1)
              >= lax.broadcasted_iota(jnp.int32, (TK, TQ), 0))
    masks[0] = jnp.zeros((TK, TQ), F32)
    masks[1] = jnp.where(causal, 0.0, NEG_BIG)

    def clear(qi, carry):
        acc[qi] = jnp.zeros((N_MAPS, DIFF_V_DIM + ONES_ROWS, TQ), F32)
        return carry

    lax.fori_loop(0, N_QT, clear, 0)

    def score_matmuls(qi, t, slot):
        qs = pl.multiple_of(qi * TQ, TQ)
        ks = pl.multiple_of((qi - t) * TK, TK)
        mask = masks[(t == 0).astype(jnp.int32)]
        tile_max = []
        for c in range(N_MAPS):
            s = lax.dot_general(kp[c, pl.ds(ks, TK), :], qp[c, pl.ds(qs, TQ), :], _NT,
                                preferred_element_type=F32) + mask
            s_buf[slot, c] = s
            tile_max.append(jnp.max(s, axis=0, keepdims=True))
        return tuple(tile_max)

    def softmax_step(t, slot, tile_max, maxima):
        on_diagonal = t == 0
        dist = jnp.asarray(t * TK, F32)
        new_m, alphas = [], []
        for c in range(N_MAPS):
            coff = (slopes[c // 2] * LOG2E) * dist
            m = jnp.where(on_diagonal, NEG_BIG, maxima[c])
            n = jnp.maximum(m, tile_max[c] - coff)
            new_m.append(n)
            alphas.append(jnp.exp2(m - n))
            p_buf[slot, c] = jnp.exp2(s_buf[slot, c] - (n + coff)).astype(BF16)
        return tuple(new_m), tuple(alphas)

    def value_matmuls(qi, t, slot, alphas):
        ks = pl.multiple_of((qi - t) * TK, TK)
        for c in range(N_MAPS):
            pv = jnp.dot(vta[c // 2, :, pl.ds(ks, TK)], p_buf[slot, c], preferred_element_type=F32)
            acc[qi, c] = alphas[c] * acc[qi, c] + pv

    def next_tile(qi, t):
        last = t == qi
        nq = jnp.where(last, qi + 1, qi)
        nt = jnp.where(last, 0, t + 1)
        return jnp.minimum(nq, N_QT - 1), nt

    zero = jnp.int32(0)
    first_max = score_matmuls(zero, zero, 0)
    p_buf[1] = jnp.zeros((N_MAPS, TK, TQ), BF16)
    row = lambda v: tuple(jnp.full((1, TQ), v, F32) for _ in range(N_MAPS))

    def several_tiles(i, carry):
        qi, t, qi_prev, t_prev, tile_max, maxima, alphas = carry
        for u in range(TILES_PER_TRIP):
            slot = u % 2
            q_next, t_next = next_tile(qi, t)
            next_max = score_matmuls(q_next, t_next, 1 - slot)
            maxima, new_alphas = softmax_step(t, slot, tile_max, maxima)
            value_matmuls(qi_prev, t_prev, 1 - slot, alphas)
            qi_prev, t_prev, alphas = qi, t, new_alphas
            qi, t, tile_max = q_next, t_next, next_max
        return qi, t, qi_prev, t_prev, tile_max, maxima, alphas

    assert TILES_PER_TRIP % 2 == 0 and N_TILES % TILES_PER_TRIP == 0
    carry = lax.fori_loop(0, N_TILES // TILES_PER_TRIP, several_tiles,
                          (zero, zero, zero, zero, first_max, row(0.0), row(1.0)))
    _, _, qi_prev, t_prev, _, _, alphas = carry
    value_matmuls(qi_prev, t_prev, 1, alphas)

    def finish(qi, carry):
        qs = pl.multiple_of(qi * TQ, TQ)
        for h in range(N_DIFF_HEADS):
            a1, a2 = acc[qi, 2 * h], acc[qi, 2 * h + 1]
            l1, l2 = a1[DIFF_V_DIM:DIFF_V_DIM + 1, :], a2[DIFF_V_DIM:DIFF_V_DIM + 1, :]
            o = a1[:DIFF_V_DIM, :] * (1.0 / l1) - lam * (a2[:DIFF_V_DIM, :] * (1.0 / l2))
            ms = jnp.mean(o * o, axis=0, keepdims=True)
            y = o * lax.rsqrt(ms + RMS_EPS) * g_ref[...] * (1.0 - LAM_INIT)
            o_ref[pl.ds(qs, TQ), h * LANES:(h + 1) * LANES] = y.T.astype(BF16)
        return carry

    lax.fori_loop(0, N_QT, finish, 0)


def _diff_attention(slopes, qk3, vt, lq1, lk1, lq2, lk2, g_col):
    batch = qk3.shape[0]
    vec = pl.BlockSpec((1, HEAD_DIM), lambda b: (0, 0))
    return pl.pallas_call(
        functools.partial(_diff_attn_kernel, slopes=slopes),
        grid=(batch,),
        in_specs=[
            pl.BlockSpec((None, SEQ, DIFF_QK_WIDTH), lambda b: (b, 0, 0)),
            pl.BlockSpec((None, SEQ, DIFF_QK_WIDTH), lambda b: (b, 0, 1)),
            pl.BlockSpec((None, DIFF_V_WIDTH, SEQ), lambda b: (b, 0, 0)),
            vec, vec, vec, vec,
            pl.BlockSpec((DIFF_V_DIM, 1), lambda b: (0, 0)),
        ],
        out_specs=pl.BlockSpec((None, SEQ, DIFF_V_WIDTH), lambda b: (b, 0, 0)),
        out_shape=jax.ShapeDtypeStruct((batch, SEQ, DIFF_V_WIDTH), BF16),
        scratch_shapes=[
            pltpu.VMEM((N_MAPS, SEQ, LANES), BF16),
            pltpu.VMEM((N_MAPS, SEQ, LANES), BF16),
            pltpu.VMEM((N_DIFF_HEADS, DIFF_V_DIM + ONES_ROWS, SEQ), BF16),
            pltpu.VMEM((2, TK, TQ), F32),
            pltpu.VMEM((2, N_MAPS, TK, TQ), F32),
            pltpu.VMEM((2, N_MAPS, TK, TQ), BF16),
            pltpu.VMEM((N_QT, N_MAPS, DIFF_V_DIM + ONES_ROWS, TQ), F32),
        ],
        compiler_params=pltpu.CompilerParams(
            dimension_semantics=("arbitrary",), vmem_limit_bytes=VMEM_LIMIT_ATTN),
        name="diff_attention",
    )(qk3, qk3, vt, lq1, lk1, lq2, lk2, g_col)


N_PATTERNS = len(DIL_PATTERNS)
N_WIN = Q_BLOCK
N_BLOCKS = SEQ // Q_BLOCK
KPAD = Q_BLOCK


def _dil_attn_kernel(slopes_ref, q_ref, k_ref, v_ref, o_ref,
                     q4, k4, v4, qq, kt, vv, bias, o_acc, l_acc, m_acc, m_swapped, staging):
    hp = pl.program_id(1)
    lane = lax.broadcasted_iota(jnp.int32, (1, LANES), 1)
    first_head = lane < HEAD_DIM

    kt[0, :, 0:KPAD] = jnp.zeros((LANES, KPAD), BF16)
    vv[0, 0:KPAD, :] = jnp.zeros((KPAD, LANES), BF16)

    col = lax.broadcasted_iota(jnp.int32, (Q_BLOCK, 2 * Q_BLOCK), 1)
    dsub = Q_BLOCK + lax.broadcasted_iota(jnp.int32, (Q_BLOCK, 2 * Q_BLOCK), 0) - col
    valid = (dsub >= 0) & (dsub <= N_WIN)
    dsub_f = dsub.astype(F32)
    for p, (_, dil) in enumerate(DIL_PATTERNS):
        for hh in range(2):
            slope = slopes_ref[2 * hp + hh]
            b = jnp.where(valid, (-slope * (LOG2E * dil)) * dsub_f, NEG_BIG)
            bias[2 * p + hh] = b
            if p == 0:
                bias[2 * N_PATTERNS + hh] = jnp.where(col >= Q_BLOCK, b, NEG_BIG)

    def emit(p, block, q, k, v):
        dst = block * Q_BLOCK
        qq[p, pl.ds(dst, Q_BLOCK), :] = q.astype(BF16)
        kt[p, :, pl.ds(KPAD + dst, Q_BLOCK)] = k.T.astype(BF16)
        vv[p, pl.ds(KPAD + dst, Q_BLOCK), :] = v.astype(BF16)

    step = DIL_PATTERNS[1][1]
    assert DIL_PATTERNS[2][1] == step * step and N_BLOCKS == step * step
    run4 = SEQ // step

    def prepare_dense(block):
        rows = pl.ds(block * Q_BLOCK, Q_BLOCK)
        emit(0, block, q_ref[rows, :], k_ref[rows, :], v_ref[rows, :])

    def prepare_first_level(block):
        r4, n = divmod(block, step)
        rows = pl.ds(block * Q_BLOCK, Q_BLOCK)
        src = pl.ds(n * (Q_BLOCK * step) + r4, Q_BLOCK, stride=step)
        q, k, v = q_ref[src, :], k_ref[src, :], v_ref[src, :]
        q4[rows, :] = q
        k4[rows, :] = k
        v4[rows, :] = v
        emit(1, block, q, k, v)

    def prepare_second_level(r16):
        a, r4 = divmod(r16, step)
        src = pl.ds(r4 * run4 + a, Q_BLOCK, stride=step)
        emit(2, r16, q4[src, :], k4[src, :], v4[src, :])

    natural_order = (o_acc, l_acc, m_acc, m_swapped)

    def attend(p, dil, q_row, with_prev, nat_start, bias_idx):
        n_keys = 2 * Q_BLOCK if with_prev else Q_BLOCK
        key_row = q_row + (KPAD - Q_BLOCK if with_prev else KPAD)
        if not isinstance(q_row, int):
            q_row = pl.multiple_of(q_row, Q_BLOCK)
            key_row = pl.multiple_of(key_row, Q_BLOCK)
        k_t = kt[p, :, pl.ds(key_row, n_keys)]
        q = qq[p, pl.ds(q_row, Q_BLOCK), :]
        v = vv[p, pl.ds(key_row, n_keys), :]
        zero, one = jnp.zeros((), BF16), jnp.ones((), BF16)
        outs = []
        for hh in range(2):
            qh = jnp.where(first_head, q, zero) if hh == 0 else jnp.where(first_head, zero, q)
            s = jnp.dot(qh, k_t, preferred_element_type=F32)
            if with_prev:
                s = s + bias[bias_idx[hh]]
            else:
                s = s + bias[bias_idx[hh], :, Q_BLOCK:]
            m = jnp.max(s, axis=1, keepdims=True)
            e = jnp.exp2(s - m).astype(BF16)
            vh = jnp.where(first_head, v, one) if hh == 0 else jnp.where(first_head, one, v)
            ov = jnp.dot(e, vh, preferred_element_type=F32)
            outs.append((ov, m))
        results = (jnp.where(first_head, outs[0][0], outs[1][0]), jnp.where(first_head, outs[1][0], outs[0][0]),
                   jnp.where(first_head, outs[0][1], outs[1][1]), jnp.where(first_head, outs[1][1], outs[0][1]))
        if dil == step * step:
            a, r4 = divmod(nat_start, step)
            staged = pl.ds(r4 * run4 + a, Q_BLOCK, stride=step)
            for j, value in enumerate(results):
                staging[j, staged, :] = value
        else:
            nat = pl.ds(nat_start, Q_BLOCK) if dil == 1 else pl.ds(nat_start, Q_BLOCK, stride=dil)
            for ref, value in zip(natural_order, results):
                ref[p, nat, :] = value

    for block in range(N_BLOCKS):
        prepare_dense(block)

    for n in range(N_BLOCKS):
        first = 2 * N_PATTERNS if n == 0 else 0
        attend(0, 1, n * Q_BLOCK, True, n * Q_BLOCK, (first, first + 1))
        prepare_first_level(n)

    dil4 = DIL_PATTERNS[1][1]
    run = SEQ // dil4
    for n in range(run // Q_BLOCK):
        for r in range(dil4):
            attend(1, dil4, r * run + n * Q_BLOCK, n > 0, n * (Q_BLOCK * dil4) + r, (2, 3))
            prepare_second_level(n * dil4 + r)

    dil16 = DIL_PATTERNS[2][1]
    for r in range(dil16):
        attend(2, dil16, r * Q_BLOCK, False, r, (4, 5))

    rows_per_step = 256

    def blend(i, carry):
        sl = pl.ds(pl.multiple_of(i * rows_per_step, rows_per_step), rows_per_step)
        per_class = rows_per_step // step
        for r4 in range(step):
            src = pl.ds(pl.multiple_of(r4 * run4 + i * per_class, per_class), per_class)
            dst = pl.ds(i * rows_per_step + r4, per_class, stride=step)
            for j, ref in enumerate(natural_order):
                ref[N_PATTERNS - 1, dst, :] = staging[j, src, :]
        add = lambda a, b: a + b

        def weighted(m_ref, x_ref):
            ms = [m_ref[p, sl, :] for p in range(N_PATTERNS)]
            mx = functools.reduce(jnp.maximum, ms)
            return functools.reduce(add, [jnp.exp2(ms[p] - mx) * x_ref[p, sl, :] for p in range(N_PATTERNS)])

        num = weighted(m_acc, o_acc)
        den = pltpu.roll(weighted(m_swapped, l_acc), HEAD_DIM, axis=1)
        o_ref[sl, :] = (num * (1.0 / den)).astype(BF16)
        return carry

    lax.fori_loop(0, SEQ // rows_per_step, blend, 0)


def _dilated_attention(slopes, dil3):
    batch = dil3.shape[0]
    n_pairs = DIL_WIDTH // LANES
    q_off, k_off, v_off = 0, n_pairs, 2 * n_pairs
    return pl.pallas_call(
        _dil_attn_kernel,
        grid=(batch, n_pairs),
        in_specs=[
            pl.BlockSpec(memory_space=pltpu.SMEM),
            pl.BlockSpec((None, SEQ, LANES), lambda b, h: (b, 0, q_off + h)),
            pl.BlockSpec((None, SEQ, LANES), lambda b, h: (b, 0, k_off + h)),
            pl.BlockSpec((None, SEQ, LANES), lambda b, h: (b, 0, v_off + h)),
        ],
        out_specs=pl.BlockSpec((None, SEQ, LANES), lambda b, h: (b, 0, h)),
        out_shape=jax.ShapeDtypeStruct((batch, SEQ, DIL_WIDTH), BF16),
        scratch_shapes=[
            pltpu.VMEM((SEQ, LANES), F32),
            pltpu.VMEM((SEQ, LANES), F32),
            pltpu.VMEM((SEQ, LANES), F32),
            pltpu.VMEM((N_PATTERNS, SEQ, LANES), BF16),
            pltpu.VMEM((N_PATTERNS, LANES, KPAD + SEQ), BF16),
            pltpu.VMEM((N_PATTERNS, KPAD + SEQ, LANES), BF16),
            pltpu.VMEM((2 * N_PATTERNS + 2, Q_BLOCK, 2 * Q_BLOCK), F32),
            pltpu.VMEM((N_PATTERNS, SEQ, LANES), F32),
            pltpu.VMEM((N_PATTERNS, SEQ, LANES), F32),
            pltpu.VMEM((N_PATTERNS, SEQ, LANES), F32),
            pltpu.VMEM((N_PATTERNS, SEQ, LANES), F32),
            pltpu.VMEM((4, SEQ, LANES), F32),
        ],
        compiler_params=pltpu.CompilerParams(
            dimension_semantics=("arbitrary", "arbitrary"), vmem_limit_bytes=VMEM_LIMIT_ATTN),
        name="dilated_attention",
    )(slopes, dil3, dil3, dil3)


def _alibi_slopes():
    all_s = 2.0 ** (-8.0 * np.arange(1, N_ALIBI_HEADS + 1, dtype=np.float32) / N_ALIBI_HEADS)
    diff_idx = np.arange(0, N_ALIBI_HEADS, N_ALIBI_HEADS // N_DIFF_HEADS)
    dil_idx = np.setdiff1d(np.arange(N_ALIBI_HEADS), diff_idx)
    return tuple(float(s) for s in all_s[diff_idx]), jnp.asarray(all_s[dil_idx], F32)


def kernel(x, ffn1_pre_g, ffn1_w_gate, ffn1_w_up, ffn1_w_down, ffn1_post_g, mix_pre_g, w_in, lambda_q1, lambda_k1, lambda_q2, lambda_k2, diff_subln_g, w_out, mix_post_g, ffn2_pre_g, ffn2_w_gate, ffn2_w_up, ffn2_w_down, ffn2_post_g):
    batch, seq, d_model = x.shape
    assert (seq, d_model) == (SEQ, D_MODEL) and ffn1_pre_g.shape[0] == 1
    slopes_d, slopes_l = _alibi_slopes()

    scale = HEAD_DIM ** -0.5 * LOG2E
    col_scale = np.ones((IN_PROJ_WIDTH,), np.float32)
    col_scale[:DIFF_QK_WIDTH] = scale
    q_l0 = 2 * DIFF_QK_WIDTH + DIFF_V_WIDTH
    col_scale[q_l0:q_l0 + DIL_WIDTH] = scale

    l = 0
    row = lambda g: g[l].reshape(1, -1)
    x2d = x.reshape(batch * seq, d_model)
    x1, qk, vt, dil = _ffn1_inproj(
        x2d, row(ffn1_pre_g), ffn1_w_gate[l].astype(BF16), ffn1_w_up[l].astype(BF16),
        ffn1_w_down[l].astype(BF16), row(ffn1_post_g), row(mix_pre_g),
        (w_in[l] * col_scale).astype(BF16), batch)
    o_d = _diff_attention(slopes_d, qk.reshape(batch, seq, 2 * DIFF_QK_WIDTH), vt, row(lambda_q1),
                          row(lambda_k1), row(lambda_q2), row(lambda_k2),
                          diff_subln_g[l].reshape(DIFF_V_DIM, 1))
    o_l = _dilated_attention(slopes_l, dil.reshape(batch, seq, 3 * DIL_WIDTH))
    w_o = w_out[l].astype(BF16)
    out = _outproj_ffn2(
        o_d.reshape(batch * seq, DIFF_V_WIDTH), o_l.reshape(batch * seq, DIL_WIDTH), x1,
        w_o[:DIFF_V_WIDTH], w_o[DIFF_V_WIDTH:], row(mix_post_g), row(ffn2_pre_g),
        ffn2_w_gate[l].astype(BF16), ffn2_w_up[l].astype(BF16), ffn2_w_down[l].astype(BF16),
        row(ffn2_post_g))
    return out.reshape(batch, seq, d_model)
```

```python
import functools
import math

import numpy as np
import jax
import jax.numpy as jnp
from jax import lax
from jax.experimental import pallas as pl
from jax.experimental.pallas import tpu as pltpu

F32 = jnp.float32
BF16 = jnp.bfloat16

D_MODEL = 1024
SEQ = 2048
HEAD_DIM = 64
N_DIFF_HEADS = 4
DIFF_V_DIM = 2 * HEAD_DIM
N_DIL_HEADS = 8
DIL_PATTERNS = ((128, 1), (512, 4), (2048, 16))
Q_BLOCK = 128
D_FF = 2816
RMS_EPS = 1e-6

DIFF_QK_WIDTH = N_DIFF_HEADS * 2 * HEAD_DIM
DIFF_V_WIDTH = N_DIFF_HEADS * DIFF_V_DIM
DIL_WIDTH = N_DIL_HEADS * HEAD_DIM
IN_PROJ_WIDTH = 2 * DIFF_QK_WIDTH + DIFF_V_WIDTH + 3 * DIL_WIDTH
DIL_OFFSET = 2 * DIFF_QK_WIDTH + DIFF_V_WIDTH
N_ALIBI_HEADS = N_DIFF_HEADS + N_DIL_HEADS
LAM_INIT = 0.8 - 0.6 * math.exp(-0.3 * 0)

LANES = 128
NEG_BIG = -1e30
VMEM_LIMIT_FFN = 56 * 1024 * 1024
VMEM_LIMIT_ATTN = 52 * 1024 * 1024

TM_IN = 512
TM_OUT = 1024
TM_SUB = 256
TQ = 256
TK = 256


def _rms(x, g):
    ms = jnp.mean(x * x, axis=-1, keepdims=True)
    return x * lax.rsqrt(ms + RMS_EPS) * g


def _row_groups(tm):
    return [slice(r * TM_SUB, (r + 1) * TM_SUB) for r in range(tm // TM_SUB)]


def _swiglu_half_step(xs, pre_g, wg_ref, wu_ref, wd_ref, post_g):
    xb = [_rms(x, pre_g).astype(BF16) for x in xs]
    gu = [(jnp.dot(b, wg_ref[...], preferred_element_type=F32),
           jnp.dot(b, wu_ref[...], preferred_element_type=F32)) for b in xb]
    ys = []
    for g, u in gu:
        h = (g * (1.0 / (1.0 + jnp.exp(-g))) * u).astype(BF16)
        ys.append(jnp.dot(h, wd_ref[...], preferred_element_type=F32))
    return [x + 0.5 * _rms(y, post_g) for x, y in zip(xs, ys)]


def _ffn1_inproj_kernel(x_ref, pre_g, wg, wu, wd, post_g, mix_g, w_in, x1_ref, qk_ref, vt_ref, dil_ref):
    groups = _row_groups(TM_IN)
    x1s = _swiglu_half_step([x_ref[rows, :] for rows in groups], pre_g[...], wg, wu, wd, post_g[...])
    for rows, x1 in zip(groups, x1s):
        x1_ref[rows, :] = x1
        hb = _rms(x1, mix_g[...]).astype(BF16)
        proj = jnp.dot(hb, w_in[...], preferred_element_type=F32)
        qk_ref[rows, :] = proj[:, :2 * DIFF_QK_WIDTH].astype(BF16)
        v_d = proj[:, 2 * DIFF_QK_WIDTH:DIL_OFFSET]
        vt_ref[:, rows] = v_d.T.astype(BF16)
        dil_ref[rows, :] = proj[:, DIL_OFFSET:]


def _const_spec(shape):
    nd = len(shape)
    return pl.BlockSpec(shape, lambda *_: (0,) * nd, pipeline_mode=pl.Buffered(1))


def _ffn1_inproj(x2d, pre_g, wg, wu, wd, post_g, mix_g, w_in, batch):
    t = x2d.shape[0]
    tm = TM_IN
    tiles_per_seq = SEQ // tm
    row = lambda i: (i, 0)
    return pl.pallas_call(
        _ffn1_inproj_kernel,
        grid=(t // tm,),
        in_specs=[
            pl.BlockSpec((tm, D_MODEL), row),
            _const_spec((1, D_MODEL)),
            _const_spec((D_MODEL, D_FF)),
            _const_spec((D_MODEL, D_FF)),
            _const_spec((D_FF, D_MODEL)),
            _const_spec((1, D_MODEL)),
            _const_spec((1, D_MODEL)),
            _const_spec((D_MODEL, IN_PROJ_WIDTH)),
        ],
        out_specs=[
            pl.BlockSpec((tm, D_MODEL), row),
            pl.BlockSpec((tm, 2 * DIFF_QK_WIDTH), row),
            pl.BlockSpec((None, DIFF_V_WIDTH, tm), lambda i: (i // tiles_per_seq, 0, i % tiles_per_seq)),
            pl.BlockSpec((tm, 3 * DIL_WIDTH), row),
        ],
        out_shape=[
            jax.ShapeDtypeStruct((t, D_MODEL), F32),
            jax.ShapeDtypeStruct((t, 2 * DIFF_QK_WIDTH), BF16),
            jax.ShapeDtypeStruct((batch, DIFF_V_WIDTH, SEQ), BF16),
            jax.ShapeDtypeStruct((t, 3 * DIL_WIDTH), F32),
        ],
        compiler_params=pltpu.CompilerParams(
            dimension_semantics=("arbitrary",), vmem_limit_bytes=VMEM_LIMIT_FFN),
        name="ffn1_inproj",
    )(x2d, pre_g, wg, wu, wd, post_g, mix_g, w_in)


def _outproj_ffn2_kernel(od_ref, ol_ref, x1_ref, wo_d, wo_l, mix_post_g, pre_g, wg, wu, wd, post_g, out_ref):
    groups = _row_groups(TM_OUT)
    x2s = []
    for rows in groups:
        h = (jnp.dot(od_ref[rows, :], wo_d[...], preferred_element_type=F32)
             + jnp.dot(ol_ref[rows, :], wo_l[...], preferred_element_type=F32))
        x2s.append(x1_ref[rows, :] + _rms(h, mix_post_g[...]))
    outs = _swiglu_half_step(x2s, pre_g[...], wg, wu, wd, post_g[...])
    for rows, out in zip(groups, outs):
        out_ref[rows, :] = out


def _outproj_ffn2(od, ol, x1, wo_d, wo_l, mix_post_g, pre_g, wg, wu, wd, post_g):
    t = x1.shape[0]
    tm = TM_OUT
    row = lambda i: (i, 0)
    return pl.pallas_call(
        _outproj_ffn2_kernel,
        grid=(t // tm,),
        in_specs=[
            pl.BlockSpec((tm, DIFF_V_WIDTH), row),
            pl.BlockSpec((tm, DIL_WIDTH), row),
            pl.BlockSpec((tm, D_MODEL), row),
            _const_spec((DIFF_V_WIDTH, D_MODEL)),
            _const_spec((DIL_WIDTH, D_MODEL)),
            _const_spec((1, D_MODEL)),
            _const_spec((1, D_MODEL)),
            _const_spec((D_MODEL, D_FF)),
            _const_spec((D_MODEL, D_FF)),
            _const_spec((D_FF, D_MODEL)),
            _const_spec((1, D_MODEL)),
        ],
        out_specs=pl.BlockSpec((tm, D_MODEL), row),
        out_shape=jax.ShapeDtypeStruct((t, D_MODEL), F32),
        compiler_params=pltpu.CompilerParams(
            dimension_semantics=("arbitrary",), vmem_limit_bytes=VMEM_LIMIT_FFN),
        name="outproj_ffn2",
    )(od, ol, x1, wo_d, wo_l, mix_post_g, pre_g, wg, wu, wd, post_g)


_NT = (((1,), (1,)), ((), ()))
N_MAPS = 2 * N_DIFF_HEADS
ONES_ROWS = 16
SLOPE_TERMS = 3
N_QT = SEQ // TQ
N_TILES = N_QT * (N_QT + 1) // 2
TILES_PER_TRIP = 2
LOG2E = math.log2(math.e)


def _bf16_terms(x, n_terms):
    terms, rest = [], np.float32(x)
    for _ in range(n_terms):
        t = np.float32(np.asarray(rest, dtype=BF16))
        terms.append(float(t))
        rest = np.float32(rest - t)
    return terms


def _diff_attn_kernel(q_ref, k_ref, vt_ref, lq1, lk1, lq2, lk2, g_ref, o_ref,
                      kp, qp, vta, masks, s_buf, p_buf, acc, *, slopes):
    lam = (jnp.exp(jnp.sum(lq1[...] * lk1[...], axis=-1, keepdims=True))
           - jnp.exp(jnp.sum(lq2[...] * lk2[...], axis=-1, keepdims=True)) + LAM_INIT)
    lane = lax.broadcasted_iota(jnp.int32, (1, LANES), 1)
    first_map = lane < HEAD_DIM

    def in_lanes(base, values):
        row = jnp.zeros((1, LANES), F32)
        for i, v in enumerate(values):
            row = jnp.where(lane == base + i, v, row)
        return row.astype(BF16)

    slope_rows = [(in_lanes(HEAD_DIM, _bf16_terms(s * LOG2E, SLOPE_TERMS)),
                   in_lanes(0, _bf16_terms(s * LOG2E, SLOPE_TERMS))) for s in slopes]
    key_idx = lax.broadcasted_iota(jnp.int32, (TK, LANES), 0).astype(F32)
    idx_first = jnp.where((lane >= HEAD_DIM) & (lane < HEAD_DIM + SLOPE_TERMS), key_idx, 0.0).astype(BF16)
    idx_second = jnp.where(lane < SLOPE_TERMS, key_idx, 0.0).astype(BF16)

    def augment(i, carry):
        rows = pl.ds(pl.multiple_of(i * TK, TK), TK)
        for h in range(N_DIFF_HEADS):
            k = k_ref[rows, h * LANES:(h + 1) * LANES]
            kp[2 * h, rows, :] = jnp.where(first_map, k, idx_first)
            kp[2 * h + 1, rows, :] = jnp.where(first_map, idx_second, k)
            q = q_ref[rows, h * LANES:(h + 1) * LANES]
            qp[2 * h, rows, :] = jnp.where(first_map, q, slope_rows[h][0])
            qp[2 * h + 1, rows, :] = jnp.where(first_map, slope_rows[h][1], q)
        return carry

    lax.fori_loop(0, SEQ // TK, augment, 0)
    for h in range(N_DIFF_HEADS):
        vta[h, 0:DIFF_V_DIM, :] = vt_ref[h * DIFF_V_DIM:(h + 1) * DIFF_V_DIM, :]
        vta[h, DIFF_V_DIM:, :] = jnp.ones((ONES_ROWS, SEQ), BF16)
    causal = (lax.broadcasted_iota(jnp.int32, (TK, TQ), 1)
              >= lax.broadcasted_iota(jnp.int32, (TK, TQ), 0))
    masks[0] = jnp.zeros((TK, TQ), F32)
    masks[1] = jnp.where(causal, 0.0, NEG_BIG)

    def clear(qi, carry):
        acc[qi] = jnp.zeros((N_MAPS, DIFF_V_DIM + ONES_ROWS, TQ), F32)
        return carry

    lax.fori_loop(0, N_QT, clear, 0)

    def score_matmuls(qi, t, slot):
        qs = pl.multiple_of(qi * TQ, TQ)
        ks = pl.multiple_of((qi - t) * TK, TK)
        mask = masks[(t == 0).astype(jnp.int32)]
        tile_max = []
        for c in range(N_MAPS):
            s = lax.dot_general(kp[c, pl.ds(ks, TK), :], qp[c, pl.ds(qs, TQ), :], _NT,
                                preferred_element_type=F32) + mask
            s_buf[slot, c] = s
            tile_max.append(jnp.max(s, axis=0, keepdims=True))
        return tuple(tile_max)

    def softmax_step(t, slot, tile_max, maxima):
        on_diagonal = t == 0
        dist = jnp.asarray(t * TK, F32)
        new_m, alphas = [], []
        for c in range(N_MAPS):
            coff = (slopes[c // 2] * LOG2E) * dist
            m = jnp.where(on_diagonal, NEG_BIG, maxima[c])
            n = jnp.maximum(m, tile_max[c] - coff)
            new_m.append(n)
            alphas.append(jnp.exp2(m - n))
            p_buf[slot, c] = jnp.exp2(s_buf[slot, c] - (n + coff)).astype(BF16)
        return tuple(new_m), tuple(alphas)

    def value_matmuls(qi, t, slot, alphas):
        ks = pl.multiple_of((qi - t) * TK, TK)
        for c in range(N_MAPS):
            pv = jnp.dot(vta[c // 2, :, pl.ds(ks, TK)], p_buf[slot, c], preferred_element_type=F32)
            acc[qi, c] = alphas[c] * acc[qi, c] + pv

    def next_tile(qi, t):
        last = t == qi
        nq = jnp.where(last, qi + 1, qi)
        nt = jnp.where(last, 0, t + 1)
        return jnp.minimum(nq, N_QT - 1), nt

    zero = jnp.int32(0)
    first_max = score_matmuls(zero, zero, 0)
    p_buf[1] = jnp.zeros((N_MAPS, TK, TQ), BF16)
    row = lambda v: tuple(jnp.full((1, TQ), v, F32) for _ in range(N_MAPS))

    def several_tiles(i, carry):
        qi, t, qi_prev, t_prev, tile_max, maxima, alphas = carry
        for u in range(TILES_PER_TRIP):
            slot = u % 2
            q_next, t_next = next_tile(qi, t)
            next_max = score_matmuls(q_next, t_next, 1 - slot)
            maxima, new_alphas = softmax_step(t, slot, tile_max, maxima)
            value_matmuls(qi_prev, t_prev, 1 - slot, alphas)
            qi_prev, t_prev, alphas = qi, t, new_alphas
            qi, t, tile_max = q_next, t_next, next_max
        return qi, t, qi_prev, t_prev, tile_max, maxima, alphas

    assert TILES_PER_TRIP % 2 == 0 and N_TILES % TILES_PER_TRIP == 0
    carry = lax.fori_loop(0, N_TILES // TILES_PER_TRIP, several_tiles,
                          (zero, zero, zero, zero, first_max, row(0.0), row(1.0)))
    _, _, qi_prev, t_prev, _, _, alphas = carry
    value_matmuls(qi_prev, t_prev, 1, alphas)

    def finish(qi, carry):
        qs = pl.multiple_of(qi * TQ, TQ)
        for h in range(N_DIFF_HEADS):
            a1, a2 = acc[qi, 2 * h], acc[qi, 2 * h + 1]
            l1, l2 = a1[DIFF_V_DIM:DIFF_V_DIM + 1, :], a2[DIFF_V_DIM:DIFF_V_DIM + 1, :]
            o = a1[:DIFF_V_DIM, :] * (1.0 / l1) - lam * (a2[:DIFF_V_DIM, :] * (1.0 / l2))
            ms = jnp.mean(o * o, axis=0, keepdims=True)
            y = o * lax.rsqrt(ms + RMS_EPS) * g_ref[...] * (1.0 - LAM_INIT)
            o_ref[pl.ds(qs, TQ), h * LANES:(h + 1) * LANES] = y.T.astype(BF16)
        return carry

    lax.fori_loop(0, N_QT, finish, 0)


def _diff_attention(slopes, qk3, vt, lq1, lk1, lq2, lk2, g_col):
    batch = qk3.shape[0]
    vec = pl.BlockSpec((1, HEAD_DIM), lambda b: (0, 0))
    return pl.pallas_call(
        functools.partial(_diff_attn_kernel, slopes=slopes),
        grid=(batch,),
        in_specs=[
            pl.BlockSpec((None, SEQ, DIFF_QK_WIDTH), lambda b: (b, 0, 0)),
            pl.BlockSpec((None, SEQ, DIFF_QK_WIDTH), lambda b: (b, 0, 1)),
            pl.BlockSpec((None, DIFF_V_WIDTH, SEQ), lambda b: (b, 0, 0)),
            vec, vec, vec, vec,
            pl.BlockSpec((DIFF_V_DIM, 1), lambda b: (0, 0)),
        ],
        out_specs=pl.BlockSpec((None, SEQ, DIFF_V_WIDTH), lambda b: (b, 0, 0)),
        out_shape=jax.ShapeDtypeStruct((batch, SEQ, DIFF_V_WIDTH), BF16),
        scratch_shapes=[
            pltpu.VMEM((N_MAPS, SEQ, LANES), BF16),
            pltpu.VMEM((N_MAPS, SEQ, LANES), BF16),
            pltpu.VMEM((N_DIFF_HEADS, DIFF_V_DIM + ONES_ROWS, SEQ), BF16),
            pltpu.VMEM((2, TK, TQ), F32),
            pltpu.VMEM((2, N_MAPS, TK, TQ), F32),
            pltpu.VMEM((2, N_MAPS, TK, TQ), BF16),
            pltpu.VMEM((N_QT, N_MAPS, DIFF_V_DIM + ONES_ROWS, TQ), F32),
        ],
        compiler_params=pltpu.CompilerParams(
            dimension_semantics=("arbitrary",), vmem_limit_bytes=VMEM_LIMIT_ATTN),
        name="diff_attention",
    )(qk3, qk3, vt, lq1, lk1, lq2, lk2, g_col)


N_PATTERNS = len(DIL_PATTERNS)
N_WIN = Q_BLOCK
N_BLOCKS = SEQ // Q_BLOCK
KPAD = Q_BLOCK


def _dil_attn_kernel(slopes_ref, q_ref, k_ref, v_ref, o_ref,
                     qq, kt, vv, bias, o_acc, l_acc, m_acc, m_swapped, staging):
    hp = pl.program_id(1)
    lane = lax.broadcasted_iota(jnp.int32, (1, LANES), 1)
    first_head = lane < HEAD_DIM

    kt[0, :, 0:KPAD] = jnp.zeros((LANES, KPAD), BF16)
    vv[0, 0:KPAD, :] = jnp.zeros((KPAD, LANES), BF16)

    col = lax.broadcasted_iota(jnp.int32, (Q_BLOCK, 2 * Q_BLOCK), 1)
    dsub = Q_BLOCK + lax.broadcasted_iota(jnp.int32, (Q_BLOCK, 2 * Q_BLOCK), 0) - col
    valid = (dsub >= 0) & (dsub <= N_WIN)
    dsub_f = dsub.astype(F32)
    for p, (_, dil) in enumerate(DIL_PATTERNS):
        for hh in range(2):
            slope = slopes_ref[2 * hp + hh]
            b = jnp.where(valid, (-slope * (LOG2E * dil)) * dsub_f, NEG_BIG)
            bias[2 * p + hh] = b
            if p == 0:
                bias[2 * N_PATTERNS + hh] = jnp.where(col >= Q_BLOCK, b, NEG_BIG)

    def emit(p, block, q, k, v):
        dst = block * Q_BLOCK
        qq[p, pl.ds(dst, Q_BLOCK), :] = q.astype(BF16)
        kt[p, :, pl.ds(KPAD + dst, Q_BLOCK)] = k.T.astype(BF16)
        vv[p, pl.ds(KPAD + dst, Q_BLOCK), :] = v.astype(BF16)

    step = DIL_PATTERNS[1][1]
    assert DIL_PATTERNS[2][1] == step * step and N_BLOCKS == step * step
    run4 = SEQ // step

    def prepare_dense(block):
        rows = pl.ds(block * Q_BLOCK, Q_BLOCK)
        emit(0, block, q_ref[rows, :], k_ref[rows, :], v_ref[rows, :])

    def prepare_dilated(block):
        r4, n = divmod(block, step)
        src = pl.ds(n * (Q_BLOCK * step) + r4, Q_BLOCK, stride=step)
        emit(1, block, q_ref[src, :], k_ref[src, :], v_ref[src, :])

    def prepare_sparse(r16):
        src = pl.ds(r16, Q_BLOCK, stride=step * step)
        emit(2, r16, q_ref[src, :], k_ref[src, :], v_ref[src, :])

    natural_order = (o_acc, l_acc, m_acc, m_swapped)

    def attend(p, dil, q_row, with_prev, nat_start, bias_idx):
        n_keys = 2 * Q_BLOCK if with_prev else Q_BLOCK
        key_row = q_row + (KPAD - Q_BLOCK if with_prev else KPAD)
        if not isinstance(q_row, int):
            q_row = pl.multiple_of(q_row, Q_BLOCK)
            key_row = pl.multiple_of(key_row, Q_BLOCK)
        k_t = kt[p, :, pl.ds(key_row, n_keys)]
        q = qq[p, pl.ds(q_row, Q_BLOCK), :]
        v = vv[p, pl.ds(key_row, n_keys), :]
        zero, one = jnp.zeros((), BF16), jnp.ones((), BF16)
        outs = []
        for hh in range(2):
            qh = jnp.where(first_head, q, zero) if hh == 0 else jnp.where(first_head, zero, q)
            s = jnp.dot(qh, k_t, preferred_element_type=F32)
            if with_prev:
                s = s + bias[bias_idx[hh]]
            else:
                s = s + bias[bias_idx[hh], :, Q_BLOCK:]
            m = jnp.max(s, axis=1, keepdims=True)
            e = jnp.exp2(s - m).astype(BF16)
            vh = jnp.where(first_head, v, one) if hh == 0 else jnp.where(first_head, one, v)
            ov = jnp.dot(e, vh, preferred_element_type=F32)
            outs.append((ov, m))
        results = (jnp.where(first_head, outs[0][0], outs[1][0]), jnp.where(first_head, outs[1][0], outs[0][0]),
                   jnp.where(first_head, outs[0][1], outs[1][1]), jnp.where(first_head, outs[1][1], outs[0][1]))
        if dil == step * step:
            a, r4 = divmod(nat_start, step)
            staged = pl.ds(r4 * run4 + a, Q_BLOCK, stride=step)
            for j, value in enumerate(results):
                staging[j, staged, :] = value
        else:
            nat = pl.ds(nat_start, Q_BLOCK) if dil == 1 else pl.ds(nat_start, Q_BLOCK, stride=dil)
            for ref, value in zip(natural_order, results):
                ref[p, nat, :] = value

    for block in range(N_BLOCKS):
        prepare_dense(block)

    for n in range(N_BLOCKS):
        first = 2 * N_PATTERNS if n == 0 else 0
        attend(0, 1, n * Q_BLOCK, True, n * Q_BLOCK, (first, first + 1))
        prepare_dilated(n)

    dil4 = DIL_PATTERNS[1][1]
    run = SEQ // dil4
    for n in range(run // Q_BLOCK):
        for r in range(dil4):
            attend(1, dil4, r * run + n * Q_BLOCK, n > 0, n * (Q_BLOCK * dil4) + r, (2, 3))
            prepare_sparse(n * dil4 + r)

    dil16 = DIL_PATTERNS[2][1]
    for r in range(dil16):
        attend(2, dil16, r * Q_BLOCK, False, r, (4, 5))

    rows_per_step = 256

    def blend(i, carry):
        sl = pl.ds(pl.multiple_of(i * rows_per_step, rows_per_step), rows_per_step)
        per_class = rows_per_step // step
        for r4 in range(step):
            src = pl.ds(pl.multiple_of(r4 * run4 + i * per_class, per_class), per_class)
            dst = pl.ds(i * rows_per_step + r4, per_class, stride=step)
            for j, ref in enumerate(natural_order):
                ref[N_PATTERNS - 1, dst, :] = staging[j, src, :]
        add = lambda a, b: a + b

        def weighted(m_ref, x_ref):
            ms = [m_ref[p, sl, :] for p in range(N_PATTERNS)]
            mx = functools.reduce(jnp.maximum, ms)
            return functools.reduce(add, [jnp.exp2(ms[p] - mx) * x_ref[p, sl, :] for p in range(N_PATTERNS)])

        num = weighted(m_acc, o_acc)
        den = pltpu.roll(weighted(m_swapped, l_acc), HEAD_DIM, axis=1)
        o_ref[sl, :] = (num * (1.0 / den)).astype(BF16)
        return carry

    lax.fori_loop(0, SEQ // rows_per_step, blend, 0)


def _dilated_attention(slopes, dil3):
    batch = dil3.shape[0]
    n_pairs = DIL_WIDTH // LANES
    q_off, k_off, v_off = 0, n_pairs, 2 * n_pairs
    return pl.pallas_call(
        _dil_attn_kernel,
        grid=(batch, n_pairs),
        in_specs=[
            pl.BlockSpec(memory_space=pltpu.SMEM),
            pl.BlockSpec((None, SEQ, LANES), lambda b, h: (b, 0, q_off + h)),
            pl.BlockSpec((None, SEQ, LANES), lambda b, h: (b, 0, k_off + h)),
            pl.BlockSpec((None, SEQ, LANES), lambda b, h: (b, 0, v_off + h)),
        ],
        out_specs=pl.BlockSpec((None, SEQ, LANES), lambda b, h: (b, 0, h)),
        out_shape=jax.ShapeDtypeStruct((batch, SEQ, DIL_WIDTH), BF16),
        scratch_shapes=[
            pltpu.VMEM((N_PATTERNS, SEQ, LANES), BF16),
            pltpu.VMEM((N_PATTERNS, LANES, KPAD + SEQ), BF16),
            pltpu.VMEM((N_PATTERNS, KPAD + SEQ, LANES), BF16),
            pltpu.VMEM((2 * N_PATTERNS + 2, Q_BLOCK, 2 * Q_BLOCK), F32),
            pltpu.VMEM((N_PATTERNS, SEQ, LANES), F32),
            pltpu.VMEM((N_PATTERNS, SEQ, LANES), F32),
            pltpu.VMEM((N_PATTERNS, SEQ, LANES), F32),
            pltpu.VMEM((N_PATTERNS, SEQ, LANES), F32),
            pltpu.VMEM((4, SEQ, LANES), F32),
        ],
        compiler_params=pltpu.CompilerParams(
            dimension_semantics=("arbitrary", "arbitrary"), vmem_limit_bytes=VMEM_LIMIT_ATTN),
        name="dilated_attention",
    )(slopes, dil3, dil3, dil3)


def _alibi_slopes():
    all_s = 2.0 ** (-8.0 * np.arange(1, N_ALIBI_HEADS + 1, dtype=np.float32) / N_ALIBI_HEADS)
    diff_idx = np.arange(0, N_ALIBI_HEADS, N_ALIBI_HEADS // N_DIFF_HEADS)
    dil_idx = np.setdiff1d(np.arange(N_ALIBI_HEADS), diff_idx)
    return tuple(float(s) for s in all_s[diff_idx]), jnp.asarray(all_s[dil_idx], F32)


def kernel(x, ffn1_pre_g, ffn1_w_gate, ffn1_w_up, ffn1_w_down, ffn1_post_g, mix_pre_g, w_in, lambda_q1, lambda_k1, lambda_q2, lambda_k2, diff_subln_g, w_out, mix_post_g, ffn2_pre_g, ffn2_w_gate, ffn2_w_up, ffn2_w_down, ffn2_post_g):
    batch, seq, d_model = x.shape
    assert (seq, d_model) == (SEQ, D_MODEL) and ffn1_pre_g.shape[0] == 1
    slopes_d, slopes_l = _alibi_slopes()

    scale = HEAD_DIM ** -0.5 * LOG2E
    col_scale = np.ones((IN_PROJ_WIDTH,), np.float32)
    col_scale[:DIFF_QK_WIDTH] = scale
    q_l0 = 2 * DIFF_QK_WIDTH + DIFF_V_WIDTH
    col_scale[q_l0:q_l0 + DIL_WIDTH] = scale

    l = 0
    row = lambda g: g[l].reshape(1, -1)
    x2d = x.reshape(batch * seq, d_model)
    x1, qk, vt, dil = _ffn1_inproj(
        x2d, row(ffn1_pre_g), ffn1_w_gate[l].astype(BF16), ffn1_w_up[l].astype(BF16),
        ffn1_w_down[l].astype(BF16), row(ffn1_post_g), row(mix_pre_g),
        (w_in[l] * col_scale).astype(BF16), batch)
    o_d = _diff_attention(slopes_d, qk.reshape(batch, seq, 2 * DIFF_QK_WIDTH), vt, row(lambda_q1),
                          row(lambda_k1), row(lambda_q2), row(lambda_k2),
                          diff_subln_g[l].reshape(DIFF_V_DIM, 1))
    o_l = _dilated_attention(slopes_l, dil.reshape(batch, seq, 3 * DIL_WIDTH))
    w_o = w_out[l].astype(BF16)
    out = _outproj_ffn2(
        o_d.reshape(batch * seq, DIFF_V_WIDTH), o_l.reshape(batch * seq, DIL_WIDTH), x1,
        w_o[:DIFF_V_WIDTH], w_o[DIFF_V_WIDTH:], row(mix_post_g), row(ffn2_pre_g),
        ffn2_w_gate[l].astype(BF16), ffn2_w_up[l].astype(BF16), ffn2_w_down[l].astype(BF16),
        row(ffn2_post_g))
    return out.reshape(batch, seq, d_model)
```

```python
import functools
import math

import numpy as np
import jax
import jax.numpy as jnp
from jax import lax
from jax.experimental import pallas as pl
from jax.experimental.pallas import tpu as pltpu

F32 = jnp.float32
BF16 = jnp.bfloat16

D_MODEL = 1024
SEQ = 2048
HEAD_DIM = 64
N_DIFF_HEADS = 4
DIFF_V_DIM = 2 * HEAD_DIM
N_DIL_HEADS = 8
DIL_PATTERNS = ((128, 1), (512, 4), (2048, 16))
Q_BLOCK = 128
D_FF = 2816
RMS_EPS = 1e-6

DIFF_QK_WIDTH = N_DIFF_HEADS * 2 * HEAD_DIM
DIFF_V_WIDTH = N_DIFF_HEADS * DIFF_V_DIM
DIL_WIDTH = N_DIL_HEADS * HEAD_DIM
IN_PROJ_WIDTH = 2 * DIFF_QK_WIDTH + DIFF_V_WIDTH + 3 * DIL_WIDTH
DIL_OFFSET = 2 * DIFF_QK_WIDTH + DIFF_V_WIDTH
N_ALIBI_HEADS = N_DIFF_HEADS + N_DIL_HEADS
LAM_INIT = 0.8 - 0.6 * math.exp(-0.3 * 0)

LANES = 128
NEG_BIG = -1e30
VMEM_LIMIT_FFN = 56 * 1024 * 1024
VMEM_LIMIT_ATTN = 52 * 1024 * 1024

TM_IN = 512
TM_OUT = 1024
TM_SUB = 256
TQ = 256
TK = 256


def _rms(x, g):
    ms = jnp.mean(x * x, axis=-1, keepdims=True)
    return x * lax.rsqrt(ms + RMS_EPS) * g


def _row_groups(tm):
    return [slice(r * TM_SUB, (r + 1) * TM_SUB) for r in range(tm // TM_SUB)]


def _swiglu_half_step(xs, pre_g, wg_ref, wu_ref, wd_ref, post_g):
    xb = [_rms(x, pre_g).astype(BF16) for x in xs]
    gu = [(jnp.dot(b, wg_ref[...], preferred_element_type=F32),
           jnp.dot(b, wu_ref[...], preferred_element_type=F32)) for b in xb]
    ys = []
    for g, u in gu:
        h = (g * (1.0 / (1.0 + jnp.exp(-g))) * u).astype(BF16)
        ys.append(jnp.dot(h, wd_ref[...], preferred_element_type=F32))
    return [x + 0.5 * _rms(y, post_g) for x, y in zip(xs, ys)]


def _ffn1_inproj_kernel(x_ref, pre_g, wg, wu, wd, post_g, mix_g, w_in, x1_ref, qk_ref, vt_ref, dil_ref):
    groups = _row_groups(TM_IN)
    x1s = _swiglu_half_step([x_ref[rows, :] for rows in groups], pre_g[...], wg, wu, wd, post_g[...])
    for rows, x1 in zip(groups, x1s):
        x1_ref[rows, :] = x1
        hb = _rms(x1, mix_g[...]).astype(BF16)
        proj = jnp.dot(hb, w_in[...], preferred_element_type=F32)
        qk_ref[rows, :] = proj[:, :2 * DIFF_QK_WIDTH].astype(BF16)
        v_d = proj[:, 2 * DIFF_QK_WIDTH:DIL_OFFSET]
        vt_ref[:, rows] = v_d.T.astype(BF16)
        dil_ref[rows, :] = proj[:, DIL_OFFSET:]


def _const_spec(shape):
    nd = len(shape)
    return pl.BlockSpec(shape, lambda *_: (0,) * nd, pipeline_mode=pl.Buffered(1))


def _ffn1_inproj(x2d, pre_g, wg, wu, wd, post_g, mix_g, w_in, batch):
    t = x2d.shape[0]
    tm = TM_IN
    tiles_per_seq = SEQ // tm
    row = lambda i: (i, 0)
    return pl.pallas_call(
        _ffn1_inproj_kernel,
        grid=(t // tm,),
        in_specs=[
            pl.BlockSpec((tm, D_MODEL), row),
            _const_spec((1, D_MODEL)),
            _const_spec((D_MODEL, D_FF)),
            _const_spec((D_MODEL, D_FF)),
            _const_spec((D_FF, D_MODEL)),
            _const_spec((1, D_MODEL)),
            _const_spec((1, D_MODEL)),
            _const_spec((D_MODEL, IN_PROJ_WIDTH)),
        ],
        out_specs=[
            pl.BlockSpec((tm, D_MODEL), row),
            pl.BlockSpec((tm, 2 * DIFF_QK_WIDTH), row),
            pl.BlockSpec((None, DIFF_V_WIDTH, tm), lambda i: (i // tiles_per_seq, 0, i % tiles_per_seq)),
            pl.BlockSpec((tm, 3 * DIL_WIDTH), row),
        ],
        out_shape=[
            jax.ShapeDtypeStruct((t, D_MODEL), F32),
            jax.ShapeDtypeStruct((t, 2 * DIFF_QK_WIDTH), BF16),
            jax.ShapeDtypeStruct((batch, DIFF_V_WIDTH, SEQ), BF16),
            jax.ShapeDtypeStruct((t, 3 * DIL_WIDTH), F32),
        ],
        compiler_params=pltpu.CompilerParams(
            dimension_semantics=("arbitrary",), vmem_limit_bytes=VMEM_LIMIT_FFN),
        name="ffn1_inproj",
    )(x2d, pre_g, wg, wu, wd, post_g, mix_g, w_in)


def _outproj_ffn2_kernel(od_ref, ol_ref, x1_ref, wo_d, wo_l, mix_post_g, pre_g, wg, wu, wd, post_g, out_ref):
    groups = _row_groups(TM_OUT)
    x2s = []
    for rows in groups:
        h = (jnp.dot(od_ref[rows, :], wo_d[...], preferred_element_type=F32)
             + jnp.dot(ol_ref[rows, :], wo_l[...], preferred_element_type=F32))
        x2s.append(x1_ref[rows, :] + _rms(h, mix_post_g[...]))
    outs = _swiglu_half_step(x2s, pre_g[...], wg, wu, wd, post_g[...])
    for rows, out in zip(groups, outs):
        out_ref[rows, :] = out


def _outproj_ffn2(od, ol, x1, wo_d, wo_l, mix_post_g, pre_g, wg, wu, wd, post_g):
    t = x1.shape[0]
    tm = TM_OUT
    row = lambda i: (i, 0)
    return pl.pallas_call(
        _outproj_ffn2_kernel,
        grid=(t // tm,),
        in_specs=[
            pl.BlockSpec((tm, DIFF_V_WIDTH), row),
            pl.BlockSpec((tm, DIL_WIDTH), row),
            pl.BlockSpec((tm, D_MODEL), row),
            _const_spec((DIFF_V_WIDTH, D_MODEL)),
            _const_spec((DIL_WIDTH, D_MODEL)),
            _const_spec((1, D_MODEL)),
            _const_spec((1, D_MODEL)),
            _const_spec((D_MODEL, D_FF)),
            _const_spec((D_MODEL, D_FF)),
            _const_spec((D_FF, D_MODEL)),
            _const_spec((1, D_MODEL)),
        ],
        out_specs=pl.BlockSpec((tm, D_MODEL), row),
        out_shape=jax.ShapeDtypeStruct((t, D_MODEL), F32),
        compiler_params=pltpu.CompilerParams(
            dimension_semantics=("arbitrary",), vmem_limit_bytes=VMEM_LIMIT_FFN),
        name="outproj_ffn2",
    )(od, ol, x1, wo_d, wo_l, mix_post_g, pre_g, wg, wu, wd, post_g)


_NT = (((1,), (1,)), ((), ()))
N_MAPS = 2 * N_DIFF_HEADS
ONES_ROWS = 16
SLOPE_TERMS = 3
N_QT = SEQ // TQ
N_TILES = N_QT * (N_QT + 1) // 2
TILES_PER_TRIP = 2
LOG2E = math.log2(math.e)


def _bf16_terms(x, n_terms):
    terms, rest = [], np.float32(x)
    for _ in range(n_terms):
        t = np.float32(np.asarray(rest, dtype=BF16))
        terms.append(float(t))
        rest = np.float32(rest - t)
    return terms


def _diff_attn_kernel(q_ref, k_ref, vt_ref, lq1, lk1, lq2, lk2, g_ref, o_ref,
                      kp, qp, vta, masks, s_buf, acc, *, slopes):
    lam = (jnp.exp(jnp.sum(lq1[...] * lk1[...], axis=-1, keepdims=True))
           - jnp.exp(jnp.sum(lq2[...] * lk2[...], axis=-1, keepdims=True)) + LAM_INIT)
    lane = lax.broadcasted_iota(jnp.int32, (1, LANES), 1)
    first_map = lane < HEAD_DIM

    def in_lanes(base, values):
        row = jnp.zeros((1, LANES), F32)
        for i, v in enumerate(values):
            row = jnp.where(lane == base + i, v, row)
        return row.astype(BF16)

    slope_rows = [(in_lanes(HEAD_DIM, _bf16_terms(s * LOG2E, SLOPE_TERMS)),
                   in_lanes(0, _bf16_terms(s * LOG2E, SLOPE_TERMS))) for s in slopes]
    key_idx = lax.broadcasted_iota(jnp.int32, (TK, LANES), 0).astype(F32)
    idx_first = jnp.where((lane >= HEAD_DIM) & (lane < HEAD_DIM + SLOPE_TERMS), key_idx, 0.0).astype(BF16)
    idx_second = jnp.where(lane < SLOPE_TERMS, key_idx, 0.0).astype(BF16)

    def augment(i, carry):
        rows = pl.ds(pl.multiple_of(i * TK, TK), TK)
        for h in range(N_DIFF_HEADS):
            k = k_ref[rows, h * LANES:(h + 1) * LANES]
            kp[2 * h, rows, :] = jnp.where(first_map, k, idx_first)
            kp[2 * h + 1, rows, :] = jnp.where(first_map, idx_second, k)
            q = q_ref[rows, h * LANES:(h + 1) * LANES]
            qp[2 * h, rows, :] = jnp.where(first_map, q, slope_rows[h][0])
            qp[2 * h + 1, rows, :] = jnp.where(first_map, slope_rows[h][1], q)
        return carry

    lax.fori_loop(0, SEQ // TK, augment, 0)
    for h in range(N_DIFF_HEADS):
        vta[h, 0:DIFF_V_DIM, :] = vt_ref[h * DIFF_V_DIM:(h + 1) * DIFF_V_DIM, :]
        vta[h, DIFF_V_DIM:, :] = jnp.ones((ONES_ROWS, SEQ), BF16)
    causal = (lax.broadcasted_iota(jnp.int32, (TK, TQ), 1)
              >= lax.broadcasted_iota(jnp.int32, (TK, TQ), 0))
    masks[0] = jnp.zeros((TK, TQ), F32)
    masks[1] = jnp.where(causal, 0.0, NEG_BIG)

    def clear(qi, carry):
        acc[qi] = jnp.zeros((N_MAPS, DIFF_V_DIM + ONES_ROWS, TQ), F32)
        return carry

    lax.fori_loop(0, N_QT, clear, 0)

    def score_matmuls(qi, t, slot):
        qs = pl.multiple_of(qi * TQ, TQ)
        ks = pl.multiple_of((qi - t) * TK, TK)
        mask = masks[(t == 0).astype(jnp.int32)]
        tile_max = []
        for c in range(N_MAPS):
            s = lax.dot_general(kp[c, pl.ds(ks, TK), :], qp[c, pl.ds(qs, TQ), :], _NT,
                                preferred_element_type=F32) + mask
            s_buf[slot, c] = s
            tile_max.append(jnp.max(s, axis=0, keepdims=True))
        return tuple(tile_max)

    def tile_step(qi, t, slot, tile_max, maxima, q_next, t_next):
        on_diagonal = t == 0
        dist = jnp.asarray(t * TK, F32)
        ks = pl.multiple_of((qi - t) * TK, TK)
        qs_next = pl.multiple_of(q_next * TQ, TQ)
        ks_next = pl.multiple_of((q_next - t_next) * TK, TK)
        mask_next = masks[(t_next == 0).astype(jnp.int32)]
        next_max, new_m = [], []

        def next_scores(c):
            s = lax.dot_general(kp[c, pl.ds(ks_next, TK), :], qp[c, pl.ds(qs_next, TQ), :], _NT,
                                preferred_element_type=F32) + mask_next
            s_buf[1 - slot, c] = s
            next_max.append(jnp.max(s, axis=0, keepdims=True))

        next_scores(0)
        next_scores(1)
        for c in range(N_MAPS):
            coff = (slopes[c // 2] * LOG2E) * dist
            m = jnp.where(on_diagonal, NEG_BIG, maxima[c])
            n = jnp.maximum(m, tile_max[c] - coff)
            new_m.append(n)
            p = jnp.exp2(s_buf[slot, c] - (n + coff)).astype(BF16)
            pv = jnp.dot(vta[c // 2, :, pl.ds(ks, TK)], p, preferred_element_type=F32)
            acc[qi, c] = jnp.exp2(m - n) * acc[qi, c] + pv
            if c + 2 < N_MAPS:
                next_scores(c + 2)
        return tuple(next_max), tuple(new_m)

    def next_tile(qi, t):
        last = t == qi
        nq = jnp.where(last, qi + 1, qi)
        nt = jnp.where(last, 0, t + 1)
        return jnp.minimum(nq, N_QT - 1), nt

    zero = jnp.int32(0)
    first_max = score_matmuls(zero, zero, 0)
    row = lambda v: tuple(jnp.full((1, TQ), v, F32) for _ in range(N_MAPS))

    def several_tiles(i, carry):
        qi, t, tile_max, maxima = carry
        for u in range(TILES_PER_TRIP):
            q_next, t_next = next_tile(qi, t)
            tile_max, maxima = tile_step(qi, t, u % 2, tile_max, maxima, q_next, t_next)
            qi, t = q_next, t_next
        return qi, t, tile_max, maxima

    assert TILES_PER_TRIP % 2 == 0 and N_TILES % TILES_PER_TRIP == 0
    lax.fori_loop(0, N_TILES // TILES_PER_TRIP, several_tiles, (zero, zero, first_max, row(0.0)))

    def finish(qi, carry):
        qs = pl.multiple_of(qi * TQ, TQ)
        for h in range(N_DIFF_HEADS):
            a1, a2 = acc[qi, 2 * h], acc[qi, 2 * h + 1]
            l1, l2 = a1[DIFF_V_DIM:DIFF_V_DIM + 1, :], a2[DIFF_V_DIM:DIFF_V_DIM + 1, :]
            o = a1[:DIFF_V_DIM, :] * (1.0 / l1) - lam * (a2[:DIFF_V_DIM, :] * (1.0 / l2))
            ms = jnp.mean(o * o, axis=0, keepdims=True)
            y = o * lax.rsqrt(ms + RMS_EPS) * g_ref[...] * (1.0 - LAM_INIT)
            o_ref[pl.ds(qs, TQ), h * LANES:(h + 1) * LANES] = y.T.astype(BF16)
        return carry

    lax.fori_loop(0, N_QT, finish, 0)


def _diff_attention(slopes, qk3, vt, lq1, lk1, lq2, lk2, g_col):
    batch = qk3.shape[0]
    vec = pl.BlockSpec((1, HEAD_DIM), lambda b: (0, 0))
    return pl.pallas_call(
        functools.partial(_diff_attn_kernel, slopes=slopes),
        grid=(batch,),
        in_specs=[
            pl.BlockSpec((None, SEQ, DIFF_QK_WIDTH), lambda b: (b, 0, 0)),
            pl.BlockSpec((None, SEQ, DIFF_QK_WIDTH), lambda b: (b, 0, 1)),
            pl.BlockSpec((None, DIFF_V_WIDTH, SEQ), lambda b: (b, 0, 0)),
            vec, vec, vec, vec,
            pl.BlockSpec((DIFF_V_DIM, 1), lambda b: (0, 0)),
        ],
        out_specs=pl.BlockSpec((None, SEQ, DIFF_V_WIDTH), lambda b: (b, 0, 0)),
        out_shape=jax.ShapeDtypeStruct((batch, SEQ, DIFF_V_WIDTH), BF16),
        scratch_shapes=[
            pltpu.VMEM((N_MAPS, SEQ, LANES), BF16),
            pltpu.VMEM((N_MAPS, SEQ, LANES), BF16),
            pltpu.VMEM((N_DIFF_HEADS, DIFF_V_DIM + ONES_ROWS, SEQ), BF16),
            pltpu.VMEM((2, TK, TQ), F32),
            pltpu.VMEM((2, N_MAPS, TK, TQ), F32),
            pltpu.VMEM((N_QT, N_MAPS, DIFF_V_DIM + ONES_ROWS, TQ), F32),
        ],
        compiler_params=pltpu.CompilerParams(
            dimension_semantics=("arbitrary",), vmem_limit_bytes=VMEM_LIMIT_ATTN),
        name="diff_attention",
    )(qk3, qk3, vt, lq1, lk1, lq2, lk2, g_col)


N_PATTERNS = len(DIL_PATTERNS)
N_WIN = Q_BLOCK
N_BLOCKS = SEQ // Q_BLOCK
KPAD = Q_BLOCK


def _dil_attn_kernel(slopes_ref, q_ref, k_ref, v_ref, o_ref,
                     q4, k4, v4, qq, kt, vv, bias, o_acc, l_acc, m_acc, m_swapped, staging):
    hp = pl.program_id(1)
    lane = lax.broadcasted_iota(jnp.int32, (1, LANES), 1)
    first_head = lane < HEAD_DIM

    kt[0, :, 0:KPAD] = jnp.zeros((LANES, KPAD), BF16)
    vv[0, 0:KPAD, :] = jnp.zeros((KPAD, LANES), BF16)

    col = lax.broadcasted_iota(jnp.int32, (Q_BLOCK, 2 * Q_BLOCK), 1)
    dsub = Q_BLOCK + lax.broadcasted_iota(jnp.int32, (Q_BLOCK, 2 * Q_BLOCK), 0) - col
    valid = (dsub >= 0) & (dsub <= N_WIN)
    dsub_f = dsub.astype(F32)
    for p, (_, dil) in enumerate(DIL_PATTERNS):
        for hh in range(2):
            slope = slopes_ref[2 * hp + hh]
            b = jnp.where(valid, (-slope * (LOG2E * dil)) * dsub_f, NEG_BIG)
            bias[2 * p + hh] = b
            if p == 0:
                bias[2 * N_PATTERNS + hh] = jnp.where(col >= Q_BLOCK, b, NEG_BIG)

    def emit(p, block, q, k, v):
        dst = block * Q_BLOCK
        qq[p, pl.ds(dst, Q_BLOCK), :] = q.astype(BF16)
        kt[p, :, pl.ds(KPAD + dst, Q_BLOCK)] = k.T.astype(BF16)
        vv[p, pl.ds(KPAD + dst, Q_BLOCK), :] = v.astype(BF16)

    step = DIL_PATTERNS[1][1]
    assert DIL_PATTERNS[2][1] == step * step and N_BLOCKS == step * step
    run4 = SEQ // step

    def prepare_dense(block):
        rows = pl.ds(block * Q_BLOCK, Q_BLOCK)
        emit(0, block, q_ref[rows, :], k_ref[rows, :], v_ref[rows, :])

    def prepare_first_level(block):
        r4, n = divmod(block, step)
        rows = pl.ds(block * Q_BLOCK, Q_BLOCK)
        src = pl.ds(n * (Q_BLOCK * step) + r4, Q_BLOCK, stride=step)
        q, k, v = q_ref[src, :], k_ref[src, :], v_ref[src, :]
        q4[rows, :] = q
        k4[rows, :] = k
        v4[rows, :] = v
        emit(1, block, q, k, v)

    def prepare_second_level(r16):
        a, r4 = divmod(r16, step)
        src = pl.ds(r4 * run4 + a, Q_BLOCK, stride=step)
        emit(2, r16, q4[src, :], k4[src, :], v4[src, :])

    natural_order = (o_acc, l_acc, m_acc, m_swapped)

    def attend(p, dil, q_row, with_prev, nat_start, bias_idx):
        n_keys = 2 * Q_BLOCK if with_prev else Q_BLOCK
        key_row = q_row + (KPAD - Q_BLOCK if with_prev else KPAD)
        if not isinstance(q_row, int):
            q_row = pl.multiple_of(q_row, Q_BLOCK)
            key_row = pl.multiple_of(key_row, Q_BLOCK)
        k_t = kt[p, :, pl.ds(key_row, n_keys)]
        q = qq[p, pl.ds(q_row, Q_BLOCK), :]
        v = vv[p, pl.ds(key_row, n_keys), :]
        zero, one = jnp.zeros((), BF16), jnp.ones((), BF16)
        outs = []
        for hh in range(2):
            qh = jnp.where(first_head, q, zero) if hh == 0 else jnp.where(first_head, zero, q)
            s = jnp.dot(qh, k_t, preferred_element_type=F32)
            if with_prev:
                s = s + bias[bias_idx[hh]]
            else:
                s = s + bias[bias_idx[hh], :, Q_BLOCK:]
            m = jnp.max(s, axis=1, keepdims=True)
            e = jnp.exp2(s - m).astype(BF16)
            vh = jnp.where(first_head, v, one) if hh == 0 else jnp.where(first_head, one, v)
            ov = jnp.dot(e, vh, preferred_element_type=F32)
            outs.append((ov, m))
        results = (jnp.where(first_head, outs[0][0], outs[1][0]), jnp.where(first_head, outs[1][0], outs[0][0]),
                   jnp.where(first_head, outs[0][1], outs[1][1]), jnp.where(first_head, outs[1][1], outs[0][1]))
        if dil == step * step:
            a, r4 = divmod(nat_start, step)
            staged = pl.ds(r4 * run4 + a, Q_BLOCK, stride=step)
            for j, value in enumerate(results):
                staging[j, staged, :] = value
        else:
            nat = pl.ds(nat_start, Q_BLOCK) if dil == 1 else pl.ds(nat_start, Q_BLOCK, stride=dil)
            for ref, value in zip(natural_order, results):
                ref[p, nat, :] = value

    for block in range(N_BLOCKS):
        prepare_dense(block)

    for n in range(N_BLOCKS):
        first = 2 * N_PATTERNS if n == 0 else 0
        attend(0, 1, n * Q_BLOCK, True, n * Q_BLOCK, (first, first + 1))
        prepare_first_level(n)

    dil4 = DIL_PATTERNS[1][1]
    run = SEQ // dil4
    for n in range(run // Q_BLOCK):
        for r in range(dil4):
            attend(1, dil4, r * run + n * Q_BLOCK, n > 0, n * (Q_BLOCK * dil4) + r, (2, 3))
            prepare_second_level(n * dil4 + r)

    dil16 = DIL_PATTERNS[2][1]
    for r in range(dil16):
        attend(2, dil16, r * Q_BLOCK, False, r, (4, 5))

    rows_per_step = 256

    def blend(i, carry):
        sl = pl.ds(pl.multiple_of(i * rows_per_step, rows_per_step), rows_per_step)
        per_class = rows_per_step // step
        for r4 in range(step):
            src = pl.ds(pl.multiple_of(r4 * run4 + i * per_class, per_class), per_class)
            dst = pl.ds(i * rows_per_step + r4, per_class, stride=step)
            for j, ref in enumerate(natural_order):
                ref[N_PATTERNS - 1, dst, :] = staging[j, src, :]
        add = lambda a, b: a + b

        def weighted(m_ref, x_ref):
            ms = [m_ref[p, sl, :] for p in range(N_PATTERNS)]
            mx = functools.reduce(jnp.maximum, ms)
            return functools.reduce(add, [jnp.exp2(ms[p] - mx) * x_ref[p, sl, :] for p in range(N_PATTERNS)])

        num = weighted(m_acc, o_acc)
        den = pltpu.roll(weighted(m_swapped, l_acc), HEAD_DIM, axis=1)
        o_ref[sl, :] = (num * (1.0 / den)).astype(BF16)
        return carry

    lax.fori_loop(0, SEQ // rows_per_step, blend, 0)


def _dilated_attention(slopes, dil3):
    batch = dil3.shape[0]
    n_pairs = DIL_WIDTH // LANES
    q_off, k_off, v_off = 0, n_pairs, 2 * n_pairs
    return pl.pallas_call(
        _dil_attn_kernel,
        grid=(batch, n_pairs),
        in_specs=[
            pl.BlockSpec(memory_space=pltpu.SMEM),
            pl.BlockSpec((None, SEQ, LANES), lambda b, h: (b, 0, q_off + h)),
            pl.BlockSpec((None, SEQ, LANES), lambda b, h: (b, 0, k_off + h)),
            pl.BlockSpec((None, SEQ, LANES), lambda b, h: (b, 0, v_off + h)),
        ],
        out_specs=pl.BlockSpec((None, SEQ, LANES), lambda b, h: (b, 0, h)),
        out_shape=jax.ShapeDtypeStruct((batch, SEQ, DIL_WIDTH), BF16),
        scratch_shapes=[
            pltpu.VMEM((SEQ, LANES), F32),
            pltpu.VMEM((SEQ, LANES), F32),
            pltpu.VMEM((SEQ, LANES), F32),
            pltpu.VMEM((N_PATTERNS, SEQ, LANES), BF16),
            pltpu.VMEM((N_PATTERNS, LANES, KPAD + SEQ), BF16),
            pltpu.VMEM((N_PATTERNS, KPAD + SEQ, LANES), BF16),
            pltpu.VMEM((2 * N_PATTERNS + 2, Q_BLOCK, 2 * Q_BLOCK), F32),
            pltpu.VMEM((N_PATTERNS, SEQ, LANES), F32),
            pltpu.VMEM((N_PATTERNS, SEQ, LANES), F32),
            pltpu.VMEM((N_PATTERNS, SEQ, LANES), F32),
            pltpu.VMEM((N_PATTERNS, SEQ, LANES), F32),
            pltpu.VMEM((4, SEQ, LANES), F32),
        ],
        compiler_params=pltpu.CompilerParams(
            dimension_semantics=("arbitrary", "arbitrary"), vmem_limit_bytes=VMEM_LIMIT_ATTN),
        name="dilated_attention",
    )(slopes, dil3, dil3, dil3)


def _alibi_slopes():
    all_s = 2.0 ** (-8.0 * np.arange(1, N_ALIBI_HEADS + 1, dtype=np.float32) / N_ALIBI_HEADS)
    diff_idx = np.arange(0, N_ALIBI_HEADS, N_ALIBI_HEADS // N_DIFF_HEADS)
    dil_idx = np.setdiff1d(np.arange(N_ALIBI_HEADS), diff_idx)
    return tuple(float(s) for s in all_s[diff_idx]), jnp.asarray(all_s[dil_idx], F32)


def kernel(x, ffn1_pre_g, ffn1_w_gate, ffn1_w_up, ffn1_w_down, ffn1_post_g, mix_pre_g, w_in, lambda_q1, lambda_k1, lambda_q2, lambda_k2, diff_subln_g, w_out, mix_post_g, ffn2_pre_g, ffn2_w_gate, ffn2_w_up, ffn2_w_down, ffn2_post_g):
    batch, seq, d_model = x.shape
    assert (seq, d_model) == (SEQ, D_MODEL) and ffn1_pre_g.shape[0] == 1
    slopes_d, slopes_l = _alibi_slopes()

    scale = HEAD_DIM ** -0.5 * LOG2E
    col_scale = np.ones((IN_PROJ_WIDTH,), np.float32)
    col_scale[:DIFF_QK_WIDTH] = scale
    q_l0 = 2 * DIFF_QK_WIDTH + DIFF_V_WIDTH
    col_scale[q_l0:q_l0 + DIL_WIDTH] = scale

    l = 0
    row = lambda g: g[l].reshape(1, -1)
    x2d = x.reshape(batch * seq, d_model)
    x1, qk, vt, dil = _ffn1_inproj(
        x2d, row(ffn1_pre_g), ffn1_w_gate[l].astype(BF16), ffn1_w_up[l].astype(BF16),
        ffn1_w_down[l].astype(BF16), row(ffn1_post_g), row(mix_pre_g),
        (w_in[l] * col_scale).astype(BF16), batch)
    o_d = _diff_attention(slopes_d, qk.reshape(batch, seq, 2 * DIFF_QK_WIDTH), vt, row(lambda_q1),
                          row(lambda_k1), row(lambda_q2), row(lambda_k2),
                          diff_subln_g[l].reshape(DIFF_V_DIM, 1))
    o_l = _dilated_attention(slopes_l, dil.reshape(batch, seq, 3 * DIL_WIDTH))
    w_o = w_out[l].astype(BF16)
    out = _outproj_ffn2(
        o_d.reshape(batch * seq, DIFF_V_WIDTH), o_l.reshape(batch * seq, DIL_WIDTH), x1,
        w_o[:DIFF_V_WIDTH], w_o[DIFF_V_WIDTH:], row(mix_post_g), row(ffn2_pre_g),
        ffn2_w_gate[l].astype(BF16), ffn2_w_up[l].astype(BF16), ffn2_w_down[l].astype(BF16),
        row(ffn2_post_g))
    return out.reshape(batch, seq, d_model)
```

```python
import functools
import math

import numpy as np
import jax
import jax.numpy as jnp
from jax import lax
from jax.experimental import pallas as pl
from jax.experimental.pallas import tpu as pltpu

F32 = jnp.float32
BF16 = jnp.bfloat16

D_MODEL = 1024
SEQ = 2048
HEAD_DIM = 64
N_DIFF_HEADS = 4
DIFF_V_DIM = 2 * HEAD_DIM
N_DIL_HEADS = 8
DIL_PATTERNS = ((128, 1), (512, 4), (2048, 16))
Q_BLOCK = 128
D_FF = 2816
RMS_EPS = 1e-6

DIFF_QK_WIDTH = N_DIFF_HEADS * 2 * HEAD_DIM
DIFF_V_WIDTH = N_DIFF_HEADS * DIFF_V_DIM
DIL_WIDTH = N_DIL_HEADS * HEAD_DIM
IN_PROJ_WIDTH = 2 * DIFF_QK_WIDTH + DIFF_V_WIDTH + 3 * DIL_WIDTH
DIL_OFFSET = 2 * DIFF_QK_WIDTH + DIFF_V_WIDTH
N_ALIBI_HEADS = N_DIFF_HEADS + N_DIL_HEADS
LAM_INIT = 0.8 - 0.6 * math.exp(-0.3 * 0)

LANES = 128
NEG_BIG = -1e30
VMEM_LIMIT_FFN = 56 * 1024 * 1024
VMEM_LIMIT_ATTN = 52 * 1024 * 1024

TM_IN = 512
TM_OUT = 1024
TM_SUB = 256
TQ = 256
TK = 256


def _rms(x, g):
    ms = jnp.mean(x * x, axis=-1, keepdims=True)
    return x * lax.rsqrt(ms + RMS_EPS) * g


def _row_groups(tm):
    return [slice(r * TM_SUB, (r + 1) * TM_SUB) for r in range(tm // TM_SUB)]


def _swiglu_half_step(xs, pre_g, wg_ref, wu_ref, wd_ref, post_g):
    xb = [_rms(x, pre_g).astype(BF16) for x in xs]
    gu = [(jnp.dot(b, wg_ref[...], preferred_element_type=F32),
           jnp.dot(b, wu_ref[...], preferred_element_type=F32)) for b in xb]
    ys = []
    for g, u in gu:
        h = (g * (1.0 / (1.0 + jnp.exp(-g))) * u).astype(BF16)
        ys.append(jnp.dot(h, wd_ref[...], preferred_element_type=F32))
    return [x + 0.5 * _rms(y, post_g) for x, y in zip(xs, ys)]


def _ffn1_inproj_kernel(x_ref, pre_g, wg, wu, wd, post_g, mix_g, w_in, x1_ref, qk_ref, vt_ref, dil_ref):
    groups = _row_groups(TM_IN)
    x1s = _swiglu_half_step([x_ref[rows, :] for rows in groups], pre_g[...], wg, wu, wd, post_g[...])
    for rows, x1 in zip(groups, x1s):
        x1_ref[rows, :] = x1
        hb = _rms(x1, mix_g[...]).astype(BF16)
        proj = jnp.dot(hb, w_in[...], preferred_element_type=F32)
        qk_ref[rows, :] = proj[:, :2 * DIFF_QK_WIDTH].astype(BF16)
        v_d = proj[:, 2 * DIFF_QK_WIDTH:DIL_OFFSET]
        vt_ref[:, rows] = v_d.T.astype(BF16)
        dil_ref[rows, :] = proj[:, DIL_OFFSET:]


def _const_spec(shape):
    nd = len(shape)
    return pl.BlockSpec(shape, lambda *_: (0,) * nd, pipeline_mode=pl.Buffered(1))


def _ffn1_inproj(x2d, pre_g, wg, wu, wd, post_g, mix_g, w_in, batch):
    t = x2d.shape[0]
    tm = TM_IN
    tiles_per_seq = SEQ // tm
    row = lambda i: (i, 0)
    return pl.pallas_call(
        _ffn1_inproj_kernel,
        grid=(t // tm,),
        in_specs=[
            pl.BlockSpec((tm, D_MODEL), row),
            _const_spec((1, D_MODEL)),
            _const_spec((D_MODEL, D_FF)),
            _const_spec((D_MODEL, D_FF)),
            _const_spec((D_FF, D_MODEL)),
            _const_spec((1, D_MODEL)),
            _const_spec((1, D_MODEL)),
            _const_spec((D_MODEL, IN_PROJ_WIDTH)),
        ],
        out_specs=[
            pl.BlockSpec((tm, D_MODEL), row),
            pl.BlockSpec((tm, 2 * DIFF_QK_WIDTH), row),
            pl.BlockSpec((None, DIFF_V_WIDTH, tm), lambda i: (i // tiles_per_seq, 0, i % tiles_per_seq)),
            pl.BlockSpec((tm, 3 * DIL_WIDTH), row),
        ],
        out_shape=[
            jax.ShapeDtypeStruct((t, D_MODEL), F32),
            jax.ShapeDtypeStruct((t, 2 * DIFF_QK_WIDTH), BF16),
            jax.ShapeDtypeStruct((batch, DIFF_V_WIDTH, SEQ), BF16),
            jax.ShapeDtypeStruct((t, 3 * DIL_WIDTH), F32),
        ],
        compiler_params=pltpu.CompilerParams(
            dimension_semantics=("arbitrary",), vmem_limit_bytes=VMEM_LIMIT_FFN),
        name="ffn1_inproj",
    )(x2d, pre_g, wg, wu, wd, post_g, mix_g, w_in)


def _outproj_ffn2_kernel(od_ref, ol_ref, x1_ref, wo_d, wo_l, mix_post_g, pre_g, wg, wu, wd, post_g, out_ref):
    groups = _row_groups(TM_OUT)
    x2s = []
    for rows in groups:
        h = (jnp.dot(od_ref[rows, :], wo_d[...], preferred_element_type=F32)
             + jnp.dot(ol_ref[rows, :], wo_l[...], preferred_element_type=F32))
        x2s.append(x1_ref[rows, :] + _rms(h, mix_post_g[...]))
    outs = _swiglu_half_step(x2s, pre_g[...], wg, wu, wd, post_g[...])
    for rows, out in zip(groups, outs):
        out_ref[rows, :] = out


def _outproj_ffn2(od, ol, x1, wo_d, wo_l, mix_post_g, pre_g, wg, wu, wd, post_g):
    t = x1.shape[0]
    tm = TM_OUT
    row = lambda i: (i, 0)
    return pl.pallas_call(
        _outproj_ffn2_kernel,
        grid=(t // tm,),
        in_specs=[
            pl.BlockSpec((tm, DIFF_V_WIDTH), row),
            pl.BlockSpec((tm, DIL_WIDTH), row),
            pl.BlockSpec((tm, D_MODEL), row),
            _const_spec((DIFF_V_WIDTH, D_MODEL)),
            _const_spec((DIL_WIDTH, D_MODEL)),
            _const_spec((1, D_MODEL)),
            _const_spec((1, D_MODEL)),
            _const_spec((D_MODEL, D_FF)),
            _const_spec((D_MODEL, D_FF)),
            _const_spec((D_FF, D_MODEL)),
            _const_spec((1, D_MODEL)),
        ],
        out_specs=pl.BlockSpec((tm, D_MODEL), row),
        out_shape=jax.ShapeDtypeStruct((t, D_MODEL), F32),
        compiler_params=pltpu.CompilerParams(
            dimension_semantics=("arbitrary",), vmem_limit_bytes=VMEM_LIMIT_FFN),
        name="outproj_ffn2",
    )(od, ol, x1, wo_d, wo_l, mix_post_g, pre_g, wg, wu, wd, post_g)


_NT = (((1,), (1,)), ((), ()))
N_MAPS = 2 * N_DIFF_HEADS
ONES_ROWS = 16
SLOPE_TERMS = 3
N_QT = SEQ // TQ
N_TILES = N_QT * (N_QT + 1) // 2
TILES_PER_TRIP = 4
LOG2E = math.log2(math.e)


def _bf16_terms(x, n_terms):
    terms, rest = [], np.float32(x)
    for _ in range(n_terms):
        t = np.float32(np.asarray(rest, dtype=BF16))
        terms.append(float(t))
        rest = np.float32(rest - t)
    return terms


def _diff_attn_kernel(q_ref, k_ref, vt_ref, lq1, lk1, lq2, lk2, g_ref, o_ref,
                      kp, qp, vta, masks, s_buf, acc, *, slopes):
    lam = (jnp.exp(jnp.sum(lq1[...] * lk1[...], axis=-1, keepdims=True))
           - jnp.exp(jnp.sum(lq2[...] * lk2[...], axis=-1, keepdims=True)) + LAM_INIT)
    lane = lax.broadcasted_iota(jnp.int32, (1, LANES), 1)
    first_map = lane < HEAD_DIM

    def in_lanes(base, values):
        row = jnp.zeros((1, LANES), F32)
        for i, v in enumerate(values):
            row = jnp.where(lane == base + i, v, row)
        return row.astype(BF16)

    slope_rows = [(in_lanes(HEAD_DIM, _bf16_terms(s * LOG2E, SLOPE_TERMS)),
                   in_lanes(0, _bf16_terms(s * LOG2E, SLOPE_TERMS))) for s in slopes]
    key_idx = lax.broadcasted_iota(jnp.int32, (TK, LANES), 0).astype(F32)
    idx_first = jnp.where((lane >= HEAD_DIM) & (lane < HEAD_DIM + SLOPE_TERMS), key_idx, 0.0).astype(BF16)
    idx_second = jnp.where(lane < SLOPE_TERMS, key_idx, 0.0).astype(BF16)

    def augment(i, carry):
        rows = pl.ds(pl.multiple_of(i * TK, TK), TK)
        for h in range(N_DIFF_HEADS):
            k = k_ref[rows, h * LANES:(h + 1) * LANES]
            kp[2 * h, rows, :] = jnp.where(first_map, k, idx_first)
            kp[2 * h + 1, rows, :] = jnp.where(first_map, idx_second, k)
            q = q_ref[rows, h * LANES:(h + 1) * LANES]
            qp[2 * h, rows, :] = jnp.where(first_map, q, slope_rows[h][0])
            qp[2 * h + 1, rows, :] = jnp.where(first_map, slope_rows[h][1], q)
        return carry

    lax.fori_loop(0, SEQ // TK, augment, 0)
    for h in range(N_DIFF_HEADS):
        vta[h, 0:DIFF_V_DIM, :] = vt_ref[h * DIFF_V_DIM:(h + 1) * DIFF_V_DIM, :]
        vta[h, DIFF_V_DIM:, :] = jnp.ones((ONES_ROWS, SEQ), BF16)
    causal = (lax.broadcasted_iota(jnp.int32, (TK, TQ), 1)
              >= lax.broadcasted_iota(jnp.int32, (TK, TQ), 0))
    masks[0] = jnp.zeros((TK, TQ), F32)
    masks[1] = jnp.where(causal, 0.0, NEG_BIG)

    def clear(qi, carry):
        acc[qi] = jnp.zeros((N_MAPS, DIFF_V_DIM + ONES_ROWS, TQ), F32)
        return carry

    @pl.when(pl.program_id(0) == 0)
    def _():
        lax.fori_loop(0, N_QT, clear, 0)

    def score_matmuls(qi, t, slot):
        qs = pl.multiple_of(qi * TQ, TQ)
        ks = pl.multiple_of((qi - t) * TK, TK)
        mask = masks[(t == 0).astype(jnp.int32)]
        tile_max = []
        for c in range(N_MAPS):
            s = lax.dot_general(kp[c, pl.ds(ks, TK), :], qp[c, pl.ds(qs, TQ), :], _NT,
                                preferred_element_type=F32) + mask
            s_buf[slot, c] = s
            tile_max.append(jnp.max(s, axis=0, keepdims=True))
        return tuple(tile_max)

    def tile_step(qi, t, slot, tile_max, maxima, q_next, t_next):
        on_diagonal = t == 0
        dist = jnp.asarray(t * TK, F32)
        ks = pl.multiple_of((qi - t) * TK, TK)
        qs_next = pl.multiple_of(q_next * TQ, TQ)
        ks_next = pl.multiple_of((q_next - t_next) * TK, TK)
        mask_next = masks[(t_next == 0).astype(jnp.int32)]
        next_max, new_m = [], []

        def next_scores(c):
            s = lax.dot_general(kp[c, pl.ds(ks_next, TK), :], qp[c, pl.ds(qs_next, TQ), :], _NT,
                                preferred_element_type=F32) + mask_next
            s_buf[1 - slot, c] = s
            next_max.append(jnp.max(s, axis=0, keepdims=True))

        next_scores(0)
        next_scores(1)
        for c in range(N_MAPS):
            coff = (slopes[c // 2] * LOG2E) * dist
            m = jnp.where(on_diagonal, NEG_BIG, maxima[c])
            n = jnp.maximum(m, tile_max[c] - coff)
            new_m.append(n)
            p = jnp.exp2(s_buf[slot, c] - (n + coff)).astype(BF16)
            pv = jnp.dot(vta[c // 2, :, pl.ds(ks, TK)], p, preferred_element_type=F32)
            acc[qi, c] = jnp.exp2(m - n) * acc[qi, c] + pv
            if c + 2 < N_MAPS:
                next_scores(c + 2)
        return tuple(next_max), tuple(new_m)

    def next_tile(qi, t):
        last = t == qi
        nq = jnp.where(last, qi + 1, qi)
        nt = jnp.where(last, 0, t + 1)
        return jnp.minimum(nq, N_QT - 1), nt

    zero = jnp.int32(0)
    first_max = score_matmuls(zero, zero, 0)
    row = lambda v: tuple(jnp.full((1, TQ), v, F32) for _ in range(N_MAPS))

    def several_tiles(i, carry):
        qi, t, tile_max, maxima = carry
        for u in range(TILES_PER_TRIP):
            q_next, t_next = next_tile(qi, t)
            tile_max, maxima = tile_step(qi, t, u % 2, tile_max, maxima, q_next, t_next)
            qi, t = q_next, t_next
        return qi, t, tile_max, maxima

    assert TILES_PER_TRIP % 2 == 0 and N_TILES % TILES_PER_TRIP == 0
    lax.fori_loop(0, N_TILES // TILES_PER_TRIP, several_tiles, (zero, zero, first_max, row(0.0)))

    def finish(qi, carry):
        qs = pl.multiple_of(qi * TQ, TQ)
        for h in range(N_DIFF_HEADS):
            a1, a2 = acc[qi, 2 * h], acc[qi, 2 * h + 1]
            l1, l2 = a1[DIFF_V_DIM:DIFF_V_DIM + 1, :], a2[DIFF_V_DIM:DIFF_V_DIM + 1, :]
            o = a1[:DIFF_V_DIM, :] * (1.0 / l1) - lam * (a2[:DIFF_V_DIM, :] * (1.0 / l2))
            ms = jnp.mean(o * o, axis=0, keepdims=True)
            y = o * lax.rsqrt(ms + RMS_EPS) * g_ref[...] * (1.0 - LAM_INIT)
            o_ref[pl.ds(qs, TQ), h * LANES:(h + 1) * LANES] = y.T.astype(BF16)
        return carry

    lax.fori_loop(0, N_QT, finish, 0)


def _diff_attention(slopes, qk3, vt, lq1, lk1, lq2, lk2, g_col):
    batch = qk3.shape[0]
    vec = pl.BlockSpec((1, HEAD_DIM), lambda b: (0, 0))
    return pl.pallas_call(
        functools.partial(_diff_attn_kernel, slopes=slopes),
        grid=(batch,),
        in_specs=[
            pl.BlockSpec((None, SEQ, DIFF_QK_WIDTH), lambda b: (b, 0, 0)),
            pl.BlockSpec((None, SEQ, DIFF_QK_WIDTH), lambda b: (b, 0, 1)),
            pl.BlockSpec((None, DIFF_V_WIDTH, SEQ), lambda b: (b, 0, 0)),
            vec, vec, vec, vec,
            pl.BlockSpec((DIFF_V_DIM, 1), lambda b: (0, 0)),
        ],
        out_specs=pl.BlockSpec((None, SEQ, DIFF_V_WIDTH), lambda b: (b, 0, 0)),
        out_shape=jax.ShapeDtypeStruct((batch, SEQ, DIFF_V_WIDTH), BF16),
        scratch_shapes=[
            pltpu.VMEM((N_MAPS, SEQ, LANES), BF16),
            pltpu.VMEM((N_MAPS, SEQ, LANES), BF16),
            pltpu.VMEM((N_DIFF_HEADS, DIFF_V_DIM + ONES_ROWS, SEQ), BF16),
            pltpu.VMEM((2, TK, TQ), F32),
            pltpu.VMEM((2, N_MAPS, TK, TQ), F32),
            pltpu.VMEM((N_QT, N_MAPS, DIFF_V_DIM + ONES_ROWS, TQ), F32),
        ],
        compiler_params=pltpu.CompilerParams(
            dimension_semantics=("arbitrary",), vmem_limit_bytes=VMEM_LIMIT_ATTN),
        name="diff_attention",
    )(qk3, qk3, vt, lq1, lk1, lq2, lk2, g_col)


N_PATTERNS = len(DIL_PATTERNS)
N_WIN = Q_BLOCK
N_BLOCKS = SEQ // Q_BLOCK
KPAD = Q_BLOCK


def _dil_attn_kernel(slopes_ref, q_ref, k_ref, v_ref, o_ref,
                     q4, k4, v4, qq, kt, vv, bias, o_acc, l_acc, m_acc, m_swapped, staging):
    hp = pl.program_id(1)
    lane = lax.broadcasted_iota(jnp.int32, (1, LANES), 1)
    first_head = lane < HEAD_DIM

    kt[0, :, 0:KPAD] = jnp.zeros((LANES, KPAD), BF16)
    vv[0, 0:KPAD, :] = jnp.zeros((KPAD, LANES), BF16)

    col = lax.broadcasted_iota(jnp.int32, (Q_BLOCK, 2 * Q_BLOCK), 1)
    dsub = Q_BLOCK + lax.broadcasted_iota(jnp.int32, (Q_BLOCK, 2 * Q_BLOCK), 0) - col
    valid = (dsub >= 0) & (dsub <= N_WIN)
    dsub_f = dsub.astype(F32)
    for p, (_, dil) in enumerate(DIL_PATTERNS):
        for hh in range(2):
            slope = slopes_ref[2 * hp + hh]
            b = jnp.where(valid, (-slope * (LOG2E * dil)) * dsub_f, NEG_BIG)
            bias[2 * p + hh] = b
            if p == 0:
                bias[2 * N_PATTERNS + hh] = jnp.where(col >= Q_BLOCK, b, NEG_BIG)

    def emit(p, block, q, k, v):
        dst = block * Q_BLOCK
        qq[p, pl.ds(dst, Q_BLOCK), :] = q.astype(BF16)
        kt[p, :, pl.ds(KPAD + dst, Q_BLOCK)] = k.T.astype(BF16)
        vv[p, pl.ds(KPAD + dst, Q_BLOCK), :] = v.astype(BF16)

    step = DIL_PATTERNS[1][1]
    assert DIL_PATTERNS[2][1] == step * step and N_BLOCKS == step * step
    run4 = SEQ // step

    def prepare_dense(block):
        rows = pl.ds(block * Q_BLOCK, Q_BLOCK)
        emit(0, block, q_ref[rows, :], k_ref[rows, :], v_ref[rows, :])

    def prepare_first_level(block):
        r4, n = divmod(block, step)
        rows = pl.ds(block * Q_BLOCK, Q_BLOCK)
        src = pl.ds(n * (Q_BLOCK * step) + r4, Q_BLOCK, stride=step)
        q, k, v = q_ref[src, :], k_ref[src, :], v_ref[src, :]
        q4[rows, :] = q
        k4[rows, :] = k
        v4[rows, :] = v
        emit(1, block, q, k, v)

    def prepare_second_level(r16):
        a, r4 = divmod(r16, step)
        src = pl.ds(r4 * run4 + a, Q_BLOCK, stride=step)
        emit(2, r16, q4[src, :], k4[src, :], v4[src, :])

    natural_order = (o_acc, l_acc, m_acc, m_swapped)

    def attend(p, dil, q_row, with_prev, nat_start, bias_idx):
        n_keys = 2 * Q_BLOCK if with_prev else Q_BLOCK
        key_row = q_row + (KPAD - Q_BLOCK if with_prev else KPAD)
        if not isinstance(q_row, int):
            q_row = pl.multiple_of(q_row, Q_BLOCK)
            key_row = pl.multiple_of(key_row, Q_BLOCK)
        k_t = kt[p, :, pl.ds(key_row, n_keys)]
        q = qq[p, pl.ds(q_row, Q_BLOCK), :]
        v = vv[p, pl.ds(key_row, n_keys), :]
        zero, one = jnp.zeros((), BF16), jnp.ones((), BF16)
        outs = []
        for hh in range(2):
            qh = jnp.where(first_head, q, zero) if hh == 0 else jnp.where(first_head, zero, q)
            s = jnp.dot(qh, k_t, preferred_element_type=F32)
            if with_prev:
                s = s + bias[bias_idx[hh]]
            else:
                s = s + bias[bias_idx[hh], :, Q_BLOCK:]
            m = jnp.max(s, axis=1, keepdims=True)
            e = jnp.exp2(s - m).astype(BF16)
            vh = jnp.where(first_head, v, one) if hh == 0 else jnp.where(first_head, one, v)
            ov = jnp.dot(e, vh, preferred_element_type=F32)
            outs.append((ov, m))
        results = (jnp.where(first_head, outs[0][0], outs[1][0]), jnp.where(first_head, outs[1][0], outs[0][0]),
                   jnp.where(first_head, outs[0][1], outs[1][1]), jnp.where(first_head, outs[1][1], outs[0][1]))
        if dil == step * step:
            a, r4 = divmod(nat_start, step)
            staged = pl.ds(r4 * run4 + a, Q_BLOCK, stride=step)
            for j, value in enumerate(results):
                staging[j, staged, :] = value
        else:
            nat = pl.ds(nat_start, Q_BLOCK) if dil == 1 else pl.ds(nat_start, Q_BLOCK, stride=dil)
            for ref, value in zip(natural_order, results):
                ref[p, nat, :] = value

    for block in range(N_BLOCKS):
        prepare_dense(block)

    for n in range(N_BLOCKS):
        first = 2 * N_PATTERNS if n == 0 else 0
        attend(0, 1, n * Q_BLOCK, True, n * Q_BLOCK, (first, first + 1))
        prepare_first_level(n)

    dil4 = DIL_PATTERNS[1][1]
    run = SEQ // dil4
    for n in range(run // Q_BLOCK):
        for r in range(dil4):
            attend(1, dil4, r * run + n * Q_BLOCK, n > 0, n * (Q_BLOCK * dil4) + r, (2, 3))
            prepare_second_level(n * dil4 + r)

    dil16 = DIL_PATTERNS[2][1]
    for r in range(dil16):
        attend(2, dil16, r * Q_BLOCK, False, r, (4, 5))

    rows_per_step = 256

    def blend(i, carry):
        sl = pl.ds(pl.multiple_of(i * rows_per_step, rows_per_step), rows_per_step)
        per_class = rows_per_step // step
        for r4 in range(step):
            src = pl.ds(pl.multiple_of(r4 * run4 + i * per_class, per_class), per_class)
            dst = pl.ds(i * rows_per_step + r4, per_class, stride=step)
            for j, ref in enumerate(natural_order):
                ref[N_PATTERNS - 1, dst, :] = staging[j, src, :]
        add = lambda a, b: a + b

        def weighted(m_ref, x_ref):
            ms = [m_ref[p, sl, :] for p in range(N_PATTERNS)]
            mx = functools.reduce(jnp.maximum, ms)
            return functools.reduce(add, [jnp.exp2(ms[p] - mx) * x_ref[p, sl, :] for p in range(N_PATTERNS)])

        num = weighted(m_acc, o_acc)
        den = pltpu.roll(weighted(m_swapped, l_acc), HEAD_DIM, axis=1)
        o_ref[sl, :] = (num * (1.0 / den)).astype(BF16)
        return carry

    lax.fori_loop(0, SEQ // rows_per_step, blend, 0)


def _dilated_attention(slopes, dil3):
    batch = dil3.shape[0]
    n_pairs = DIL_WIDTH // LANES
    q_off, k_off, v_off = 0, n_pairs, 2 * n_pairs
    return pl.pallas_call(
        _dil_attn_kernel,
        grid=(batch, n_pairs),
        in_specs=[
            pl.BlockSpec(memory_space=pltpu.SMEM),
            pl.BlockSpec((None, SEQ, LANES), lambda b, h: (b, 0, q_off + h)),
            pl.BlockSpec((None, SEQ, LANES), lambda b, h: (b, 0, k_off + h)),
            pl.BlockSpec((None, SEQ, LANES), lambda b, h: (b, 0, v_off + h)),
        ],
        out_specs=pl.BlockSpec((None, SEQ, LANES), lambda b, h: (b, 0, h)),
        out_shape=jax.ShapeDtypeStruct((batch, SEQ, DIL_WIDTH), BF16),
        scratch_shapes=[
            pltpu.VMEM((SEQ, LANES), F32),
            pltpu.VMEM((SEQ, LANES), F32),
            pltpu.VMEM((SEQ, LANES), F32),
            pltpu.VMEM((N_PATTERNS, SEQ, LANES), BF16),
            pltpu.VMEM((N_PATTERNS, LANES, KPAD + SEQ), BF16),
            pltpu.VMEM((N_PATTERNS, KPAD + SEQ, LANES), BF16),
            pltpu.VMEM((2 * N_PATTERNS + 2, Q_BLOCK, 2 * Q_BLOCK), F32),
            pltpu.VMEM((N_PATTERNS, SEQ, LANES), F32),
            pltpu.VMEM((N_PATTERNS, SEQ, LANES), F32),
            pltpu.VMEM((N_PATTERNS, SEQ, LANES), F32),
            pltpu.VMEM((N_PATTERNS, SEQ, LANES), F32),
            pltpu.VMEM((4, SEQ, LANES), F32),
        ],
        compiler_params=pltpu.CompilerParams(
            dimension_semantics=("arbitrary", "arbitrary"), vmem_limit_bytes=VMEM_LIMIT_ATTN),
        name="dilated_attention",
    )(slopes, dil3, dil3, dil3)


def _alibi_slopes():
    all_s = 2.0 ** (-8.0 * np.arange(1, N_ALIBI_HEADS + 1, dtype=np.float32) / N_ALIBI_HEADS)
    diff_idx = np.arange(0, N_ALIBI_HEADS, N_ALIBI_HEADS // N_DIFF_HEADS)
    dil_idx = np.setdiff1d(np.arange(N_ALIBI_HEADS), diff_idx)
    return tuple(float(s) for s in all_s[diff_idx]), jnp.asarray(all_s[dil_idx], F32)


def kernel(x, ffn1_pre_g, ffn1_w_gate, ffn1_w_up, ffn1_w_down, ffn1_post_g, mix_pre_g, w_in, lambda_q1, lambda_k1, lambda_q2, lambda_k2, diff_subln_g, w_out, mix_post_g, ffn2_pre_g, ffn2_w_gate, ffn2_w_up, ffn2_w_down, ffn2_post_g):
    batch, seq, d_model = x.shape
    assert (seq, d_model) == (SEQ, D_MODEL) and ffn1_pre_g.shape[0] == 1
    slopes_d, slopes_l = _alibi_slopes()

    scale = HEAD_DIM ** -0.5 * LOG2E
    col_scale = np.ones((IN_PROJ_WIDTH,), np.float32)
    col_scale[:DIFF_QK_WIDTH] = scale
    q_l0 = 2 * DIFF_QK_WIDTH + DIFF_V_WIDTH
    col_scale[q_l0:q_l0 + DIL_WIDTH] = scale

    l = 0
    row = lambda g: g[l].reshape(1, -1)
    x2d = x.reshape(batch * seq, d_model)
    x1, qk, vt, dil = _ffn1_inproj(
        x2d, row(ffn1_pre_g), ffn1_w_gate[l].astype(BF16), ffn1_w_up[l].astype(BF16),
        ffn1_w_down[l].astype(BF16), row(ffn1_post_g), row(mix_pre_g),
        (w_in[l] * col_scale).astype(BF16), batch)
    o_d = _diff_attention(slopes_d, qk.reshape(batch, seq, 2 * DIFF_QK_WIDTH), vt, row(lambda_q1),
                          row(lambda_k1), row(lambda_q2), row(lambda_k2),
                          diff_subln_g[l].reshape(DIFF_V_DIM, 1))
    o_l = _dilated_attention(slopes_l, dil.reshape(batch, seq, 3 * DIL_WIDTH))
    w_o = w_out[l].astype(BF16)
    out = _outproj_ffn2(
        o_d.reshape(batch * seq, DIFF_V_WIDTH), o_l.reshape(batch * seq, DIL_WIDTH), x1,
        w_o[:DIFF_V_WIDTH], w_o[DIFF_V_WIDTH:], row(mix_post_g), row(ffn2_pre_g),
        ffn2_w_gate[l].astype(BF16), ffn2_w_up[l].astype(BF16), ffn2_w_down[l].astype(BF16),
        row(ffn2_post_g))
    return out.reshape(batch, seq, d_model)
```

```python
import functools
import math

import numpy as np
import jax
import jax.numpy as jnp
from jax import lax
from jax.experimental import pallas as pl
from jax.experimental.pallas import tpu as pltpu

F32 = jnp.float32
BF16 = jnp.bfloat16

D_MODEL = 1024
SEQ = 2048
HEAD_DIM = 64
N_DIFF_HEADS = 4
DIFF_V_DIM = 2 * HEAD_DIM
N_DIL_HEADS = 8
DIL_PATTERNS = ((128, 1), (512, 4), (2048, 16))
Q_BLOCK = 128
D_FF = 2816
RMS_EPS = 1e-6

DIFF_QK_WIDTH = N_DIFF_HEADS * 2 * HEAD_DIM
DIFF_V_WIDTH = N_DIFF_HEADS * DIFF_V_DIM
DIL_WIDTH = N_DIL_HEADS * HEAD_DIM
IN_PROJ_WIDTH = 2 * DIFF_QK_WIDTH + DIFF_V_WIDTH + 3 * DIL_WIDTH
DIL_OFFSET = 2 * DIFF_QK_WIDTH + DIFF_V_WIDTH
N_ALIBI_HEADS = N_DIFF_HEADS + N_DIL_HEADS
LAM_INIT = 0.8 - 0.6 * math.exp(-0.3 * 0)

LANES = 128
NEG_BIG = -1e30
VMEM_LIMIT_FFN = 56 * 1024 * 1024
VMEM_LIMIT_ATTN = 52 * 1024 * 1024

TM_IN = 512
TM_OUT = 1024
TM_SUB = 256
TQ = 256
TK = 256


def _rms(x, g):
    ms = jnp.mean(x * x, axis=-1, keepdims=True)
    return x * lax.rsqrt(ms + RMS_EPS) * g


def _row_groups(tm):
    return [slice(r * TM_SUB, (r + 1) * TM_SUB) for r in range(tm // TM_SUB)]


def _swiglu_half_step(xs, pre_g, wg_ref, wu_ref, wd_ref, post_g):
    xb = [_rms(x, pre_g).astype(BF16) for x in xs]
    gu = [(jnp.dot(b, wg_ref[...], preferred_element_type=F32),
           jnp.dot(b, wu_ref[...], preferred_element_type=F32)) for b in xb]
    ys = []
    for g, u in gu:
        h = (g * (1.0 / (1.0 + jnp.exp(-g))) * u).astype(BF16)
        ys.append(jnp.dot(h, wd_ref[...], preferred_element_type=F32))
    return [x + 0.5 * _rms(y, post_g) for x, y in zip(xs, ys)]


def _ffn1_inproj_kernel(x_ref, pre_g, wg, wu, wd, post_g, mix_g, w_in, x1_ref, qk_ref, vt_ref, dil_ref):
    groups = _row_groups(TM_IN)
    x1s = _swiglu_half_step([x_ref[rows, :] for rows in groups], pre_g[...], wg, wu, wd, post_g[...])
    for rows, x1 in zip(groups, x1s):
        x1_ref[rows, :] = x1
        hb = _rms(x1, mix_g[...]).astype(BF16)
        proj = jnp.dot(hb, w_in[...], preferred_element_type=F32)
        qk_ref[rows, :] = proj[:, :2 * DIFF_QK_WIDTH].astype(BF16)
        v_d = proj[:, 2 * DIFF_QK_WIDTH:DIL_OFFSET]
        vt_ref[:, rows] = v_d.T.astype(BF16)
        dil_ref[rows, :] = proj[:, DIL_OFFSET:]


def _const_spec(shape):
    nd = len(shape)
    return pl.BlockSpec(shape, lambda *_: (0,) * nd, pipeline_mode=pl.Buffered(1))


def _ffn1_inproj(x2d, pre_g, wg, wu, wd, post_g, mix_g, w_in, batch):
    t = x2d.shape[0]
    tm = TM_IN
    tiles_per_seq = SEQ // tm
    row = lambda i: (i, 0)
    return pl.pallas_call(
        _ffn1_inproj_kernel,
        grid=(t // tm,),
        in_specs=[
            pl.BlockSpec((tm, D_MODEL), row),
            _const_spec((1, D_MODEL)),
            _const_spec((D_MODEL, D_FF)),
            _const_spec((D_MODEL, D_FF)),
            _const_spec((D_FF, D_MODEL)),
            _const_spec((1, D_MODEL)),
            _const_spec((1, D_MODEL)),
            _const_spec((D_MODEL, IN_PROJ_WIDTH)),
        ],
        out_specs=[
            pl.BlockSpec((tm, D_MODEL), row),
            pl.BlockSpec((tm, 2 * DIFF_QK_WIDTH), row),
            pl.BlockSpec((None, DIFF_V_WIDTH, tm), lambda i: (i // tiles_per_seq, 0, i % tiles_per_seq)),
            pl.BlockSpec((tm, 3 * DIL_WIDTH), row),
        ],
        out_shape=[
            jax.ShapeDtypeStruct((t, D_MODEL), F32),
            jax.ShapeDtypeStruct((t, 2 * DIFF_QK_WIDTH), BF16),
            jax.ShapeDtypeStruct((batch, DIFF_V_WIDTH, SEQ), BF16),
            jax.ShapeDtypeStruct((t, 3 * DIL_WIDTH), F32),
        ],
        compiler_params=pltpu.CompilerParams(
            dimension_semantics=("arbitrary",), vmem_limit_bytes=VMEM_LIMIT_FFN),
        name="ffn1_inproj",
    )(x2d, pre_g, wg, wu, wd, post_g, mix_g, w_in)


def _outproj_ffn2_kernel(od_ref, ol_ref, x1_ref, wo_d, wo_l, mix_post_g, pre_g, wg, wu, wd, post_g, out_ref):
    groups = _row_groups(TM_OUT)
    x2s = []
    for rows in groups:
        h = (jnp.dot(od_ref[rows, :], wo_d[...], preferred_element_type=F32)
             + jnp.dot(ol_ref[rows, :], wo_l[...], preferred_element_type=F32))
        x2s.append(x1_ref[rows, :] + _rms(h, mix_post_g[...]))
    outs = _swiglu_half_step(x2s, pre_g[...], wg, wu, wd, post_g[...])
    for rows, out in zip(groups, outs):
        out_ref[rows, :] = out


def _outproj_ffn2(od, ol, x1, wo_d, wo_l, mix_post_g, pre_g, wg, wu, wd, post_g):
    t = x1.shape[0]
    tm = TM_OUT
    row = lambda i: (i, 0)
    return pl.pallas_call(
        _outproj_ffn2_kernel,
        grid=(t // tm,),
        in_specs=[
            pl.BlockSpec((tm, DIFF_V_WIDTH), row),
            pl.BlockSpec((tm, DIL_WIDTH), row),
            pl.BlockSpec((tm, D_MODEL), row),
            _const_spec((DIFF_V_WIDTH, D_MODEL)),
            _const_spec((DIL_WIDTH, D_MODEL)),
            _const_spec((1, D_MODEL)),
            _const_spec((1, D_MODEL)),
            _const_spec((D_MODEL, D_FF)),
            _const_spec((D_MODEL, D_FF)),
            _const_spec((D_FF, D_MODEL)),
            _const_spec((1, D_MODEL)),
        ],
        out_specs=pl.BlockSpec((tm, D_MODEL), row),
        out_shape=jax.ShapeDtypeStruct((t, D_MODEL), F32),
        compiler_params=pltpu.CompilerParams(
            dimension_semantics=("arbitrary",), vmem_limit_bytes=VMEM_LIMIT_FFN),
        name="outproj_ffn2",
    )(od, ol, x1, wo_d, wo_l, mix_post_g, pre_g, wg, wu, wd, post_g)


_NT = (((1,), (1,)), ((), ()))
N_MAPS = 2 * N_DIFF_HEADS
ONES_ROWS = 16
SLOPE_TERMS = 3
N_QT = SEQ // TQ
N_TILES = N_QT * (N_QT + 1) // 2
TILES_PER_TRIP = 6
LOG2E = math.log2(math.e)


def _bf16_terms(x, n_terms):
    terms, rest = [], np.float32(x)
    for _ in range(n_terms):
        t = np.float32(np.asarray(rest, dtype=BF16))
        terms.append(float(t))
        rest = np.float32(rest - t)
    return terms


def _diff_attn_kernel(q_ref, k_ref, vt_ref, lq1, lk1, lq2, lk2, g_ref, o_ref,
                      kp, qp, vta, masks, s_buf, acc, *, slopes):
    lam = (jnp.exp(jnp.sum(lq1[...] * lk1[...], axis=-1, keepdims=True))
           - jnp.exp(jnp.sum(lq2[...] * lk2[...], axis=-1, keepdims=True)) + LAM_INIT)
    lane = lax.broadcasted_iota(jnp.int32, (1, LANES), 1)
    first_map = lane < HEAD_DIM

    def in_lanes(base, values):
        row = jnp.zeros((1, LANES), F32)
        for i, v in enumerate(values):
            row = jnp.where(lane == base + i, v, row)
        return row.astype(BF16)

    slope_rows = [(in_lanes(HEAD_DIM, _bf16_terms(s * LOG2E, SLOPE_TERMS)),
                   in_lanes(0, _bf16_terms(s * LOG2E, SLOPE_TERMS))) for s in slopes]
    key_idx = lax.broadcasted_iota(jnp.int32, (TK, LANES), 0).astype(F32)
    idx_first = jnp.where((lane >= HEAD_DIM) & (lane < HEAD_DIM + SLOPE_TERMS), key_idx, 0.0).astype(BF16)
    idx_second = jnp.where(lane < SLOPE_TERMS, key_idx, 0.0).astype(BF16)

    def augment(i, carry):
        rows = pl.ds(pl.multiple_of(i * TK, TK), TK)
        for h in range(N_DIFF_HEADS):
            k = k_ref[rows, h * LANES:(h + 1) * LANES]
            kp[2 * h, rows, :] = jnp.where(first_map, k, idx_first)
            kp[2 * h + 1, rows, :] = jnp.where(first_map, idx_second, k)
            q = q_ref[rows, h * LANES:(h + 1) * LANES]
            qp[2 * h, rows, :] = jnp.where(first_map, q, slope_rows[h][0])
            qp[2 * h + 1, rows, :] = jnp.where(first_map, slope_rows[h][1], q)
        return carry

    lax.fori_loop(0, SEQ // TK, augment, 0)
    for h in range(N_DIFF_HEADS):
        vta[h, 0:DIFF_V_DIM, :] = vt_ref[h * DIFF_V_DIM:(h + 1) * DIFF_V_DIM, :]
        vta[h, DIFF_V_DIM:, :] = jnp.ones((ONES_ROWS, SEQ), BF16)
    causal = (lax.broadcasted_iota(jnp.int32, (TK, TQ), 1)
              >= lax.broadcasted_iota(jnp.int32, (TK, TQ), 0))
    masks[0] = jnp.zeros((TK, TQ), F32)
    masks[1] = jnp.where(causal, 0.0, NEG_BIG)

    def clear(qi, carry):
        acc[qi] = jnp.zeros((N_MAPS, DIFF_V_DIM + ONES_ROWS, TQ), F32)
        return carry

    @pl.when(pl.program_id(0) == 0)
    def _():
        lax.fori_loop(0, N_QT, clear, 0)

    def score_matmuls(qi, t, slot):
        qs = pl.multiple_of(qi * TQ, TQ)
        ks = pl.multiple_of((qi - t) * TK, TK)
        mask = masks[(t == 0).astype(jnp.int32)]
        tile_max = []
        for c in range(N_MAPS):
            s = lax.dot_general(kp[c, pl.ds(ks, TK), :], qp[c, pl.ds(qs, TQ), :], _NT,
                                preferred_element_type=F32) + mask
            s_buf[slot, c] = s
            tile_max.append(jnp.max(s, axis=0, keepdims=True))
        return tuple(tile_max)

    def tile_step(qi, t, slot, tile_max, maxima, q_next, t_next):
        on_diagonal = t == 0
        dist = jnp.asarray(t * TK, F32)
        ks = pl.multiple_of((qi - t) * TK, TK)
        qs_next = pl.multiple_of(q_next * TQ, TQ)
        ks_next = pl.multiple_of((q_next - t_next) * TK, TK)
        mask_next = masks[(t_next == 0).astype(jnp.int32)]
        next_max, new_m = [], []

        def next_scores(c):
            s = lax.dot_general(kp[c, pl.ds(ks_next, TK), :], qp[c, pl.ds(qs_next, TQ), :], _NT,
                                preferred_element_type=F32) + mask_next
            s_buf[1 - slot, c] = s
            next_max.append(jnp.max(s, axis=0, keepdims=True))

        next_scores(0)
        next_scores(1)
        for c in range(N_MAPS):
            coff = (slopes[c // 2] * LOG2E) * dist
            m = jnp.where(on_diagonal, NEG_BIG, maxima[c])
            n = jnp.maximum(m, tile_max[c] - coff)
            new_m.append(n)
            p = jnp.exp2(s_buf[slot, c] - (n + coff)).astype(BF16)
            pv = jnp.dot(vta[c // 2, :, pl.ds(ks, TK)], p, preferred_element_type=F32)
            acc[qi, c] = jnp.exp2(m - n) * acc[qi, c] + pv
            if c + 2 < N_MAPS:
                next_scores(c + 2)
        return tuple(next_max), tuple(new_m)

    def next_tile(qi, t):
        last = t == qi
        nq = jnp.where(last, qi + 1, qi)
        nt = jnp.where(last, 0, t + 1)
        return jnp.minimum(nq, N_QT - 1), nt

    zero = jnp.int32(0)
    first_max = score_matmuls(zero, zero, 0)
    row = lambda v: tuple(jnp.full((1, TQ), v, F32) for _ in range(N_MAPS))

    def several_tiles(i, carry):
        qi, t, tile_max, maxima = carry
        for u in range(TILES_PER_TRIP):
            q_next, t_next = next_tile(qi, t)
            tile_max, maxima = tile_step(qi, t, u % 2, tile_max, maxima, q_next, t_next)
            qi, t = q_next, t_next
        return qi, t, tile_max, maxima

    assert TILES_PER_TRIP % 2 == 0 and N_TILES % TILES_PER_TRIP == 0
    lax.fori_loop(0, N_TILES // TILES_PER_TRIP, several_tiles, (zero, zero, first_max, row(0.0)))

    def finish(qi, carry):
        qs = pl.multiple_of(qi * TQ, TQ)
        for h in range(N_DIFF_HEADS):
            a1, a2 = acc[qi, 2 * h], acc[qi, 2 * h + 1]
            l1, l2 = a1[DIFF_V_DIM:DIFF_V_DIM + 1, :], a2[DIFF_V_DIM:DIFF_V_DIM + 1, :]
            o = a1[:DIFF_V_DIM, :] * (1.0 / l1) - lam * (a2[:DIFF_V_DIM, :] * (1.0 / l2))
            ms = jnp.mean(o * o, axis=0, keepdims=True)
            y = o * lax.rsqrt(ms + RMS_EPS) * g_ref[...] * (1.0 - LAM_INIT)
            o_ref[pl.ds(qs, TQ), h * LANES:(h + 1) * LANES] = y.T.astype(BF16)
        return carry

    lax.fori_loop(0, N_QT, finish, 0)


def _diff_attention(slopes, qk3, vt, lq1, lk1, lq2, lk2, g_col):
    batch = qk3.shape[0]
    vec = pl.BlockSpec((1, HEAD_DIM), lambda b: (0, 0))
    return pl.pallas_call(
        functools.partial(_diff_attn_kernel, slopes=slopes),
        grid=(batch,),
        in_specs=[
            pl.BlockSpec((None, SEQ, DIFF_QK_WIDTH), lambda b: (b, 0, 0)),
            pl.BlockSpec((None, SEQ, DIFF_QK_WIDTH), lambda b: (b, 0, 1)),
            pl.BlockSpec((None, DIFF_V_WIDTH, SEQ), lambda b: (b, 0, 0)),
            vec, vec, vec, vec,
            pl.BlockSpec((DIFF_V_DIM, 1), lambda b: (0, 0)),
        ],
        out_specs=pl.BlockSpec((None, SEQ, DIFF_V_WIDTH), lambda b: (b, 0, 0)),
        out_shape=jax.ShapeDtypeStruct((batch, SEQ, DIFF_V_WIDTH), BF16),
        scratch_shapes=[
            pltpu.VMEM((N_MAPS, SEQ, LANES), BF16),
            pltpu.VMEM((N_MAPS, SEQ, LANES), BF16),
            pltpu.VMEM((N_DIFF_HEADS, DIFF_V_DIM + ONES_ROWS, SEQ), BF16),
            pltpu.VMEM((2, TK, TQ), F32),
            pltpu.VMEM((2, N_MAPS, TK, TQ), F32),
            pltpu.VMEM((N_QT, N_MAPS, DIFF_V_DIM + ONES_ROWS, TQ), F32),
        ],
        compiler_params=pltpu.CompilerParams(
            dimension_semantics=("arbitrary",), vmem_limit_bytes=VMEM_LIMIT_ATTN),
        name="diff_attention",
    )(qk3, qk3, vt, lq1, lk1, lq2, lk2, g_col)


N_PATTERNS = len(DIL_PATTERNS)
N_WIN = Q_BLOCK
N_BLOCKS = SEQ // Q_BLOCK
KPAD = Q_BLOCK


def _dil_attn_kernel(slopes_ref, q_ref, k_ref, v_ref, o_ref,
                     q4, k4, v4, qq, kt, vv, bias, o_acc, l_acc, m_acc, m_swapped, staging):
    hp = pl.program_id(1)
    lane = lax.broadcasted_iota(jnp.int32, (1, LANES), 1)
    first_head = lane < HEAD_DIM

    kt[0, :, 0:KPAD] = jnp.zeros((LANES, KPAD), BF16)
    vv[0, 0:KPAD, :] = jnp.zeros((KPAD, LANES), BF16)

    col = lax.broadcasted_iota(jnp.int32, (Q_BLOCK, 2 * Q_BLOCK), 1)
    dsub = Q_BLOCK + lax.broadcasted_iota(jnp.int32, (Q_BLOCK, 2 * Q_BLOCK), 0) - col
    valid = (dsub >= 0) & (dsub <= N_WIN)
    dsub_f = dsub.astype(F32)
    for p, (_, dil) in enumerate(DIL_PATTERNS):
        for hh in range(2):
            slope = slopes_ref[2 * hp + hh]
            b = jnp.where(valid, (-slope * (LOG2E * dil)) * dsub_f, NEG_BIG)
            bias[2 * p + hh] = b
            if p == 0:
                bias[2 * N_PATTERNS + hh] = jnp.where(col >= Q_BLOCK, b, NEG_BIG)

    def emit(p, block, q, k, v):
        dst = block * Q_BLOCK
        qq[p, pl.ds(dst, Q_BLOCK), :] = q.astype(BF16)
        kt[p, :, pl.ds(KPAD + dst, Q_BLOCK)] = k.T.astype(BF16)
        vv[p, pl.ds(KPAD + dst, Q_BLOCK), :] = v.astype(BF16)

    step = DIL_PATTERNS[1][1]
    assert DIL_PATTERNS[2][1] == step * step and N_BLOCKS == step * step
    run4 = SEQ // step

    def prepare_dense(block):
        rows = pl.ds(block * Q_BLOCK, Q_BLOCK)
        emit(0, block, q_ref[rows, :], k_ref[rows, :], v_ref[rows, :])

    def prepare_first_level(block):
        r4, n = divmod(block, step)
        rows = pl.ds(block * Q_BLOCK, Q_BLOCK)
        src = pl.ds(n * (Q_BLOCK * step) + r4, Q_BLOCK, stride=step)
        q, k, v = q_ref[src, :], k_ref[src, :], v_ref[src, :]
        q4[rows, :] = q
        k4[rows, :] = k
        v4[rows, :] = v
        emit(1, block, q, k, v)

    def prepare_second_level(r16):
        a, r4 = divmod(r16, step)
        src = pl.ds(r4 * run4 + a, Q_BLOCK, stride=step)
        emit(2, r16, q4[src, :], k4[src, :], v4[src, :])

    natural_order = (o_acc, l_acc, m_acc, m_swapped)

    def attend(p, dil, q_row, with_prev, nat_start, bias_idx):
        n_keys = 2 * Q_BLOCK if with_prev else Q_BLOCK
        key_row = q_row + (KPAD - Q_BLOCK if with_prev else KPAD)
        if not isinstance(q_row, int):
            q_row = pl.multiple_of(q_row, Q_BLOCK)
            key_row = pl.multiple_of(key_row, Q_BLOCK)
        k_t = kt[p, :, pl.ds(key_row, n_keys)]
        q = qq[p, pl.ds(q_row, Q_BLOCK), :]
        v = vv[p, pl.ds(key_row, n_keys), :]
        zero, one = jnp.zeros((), BF16), jnp.ones((), BF16)
        outs = []
        for hh in range(2):
            qh = jnp.where(first_head, q, zero) if hh == 0 else jnp.where(first_head, zero, q)
            s = jnp.dot(qh, k_t, preferred_element_type=F32)
            if with_prev:
                s = s + bias[bias_idx[hh]]
            else:
                s = s + bias[bias_idx[hh], :, Q_BLOCK:]
            m = jnp.max(s, axis=1, keepdims=True)
            e = jnp.exp2(s - m).astype(BF16)
            vh = jnp.where(first_head, v, one) if hh == 0 else jnp.where(first_head, one, v)
            ov = jnp.dot(e, vh, preferred_element_type=F32)
            outs.append((ov, m))
        results = (jnp.where(first_head, outs[0][0], outs[1][0]), jnp.where(first_head, outs[1][0], outs[0][0]),
                   jnp.where(first_head, outs[0][1], outs[1][1]), jnp.where(first_head, outs[1][1], outs[0][1]))
        if dil == step * step:
            a, r4 = divmod(nat_start, step)
            staged = pl.ds(r4 * run4 + a, Q_BLOCK, stride=step)
            for j, value in enumerate(results):
                staging[j, staged, :] = value
        else:
            nat = pl.ds(nat_start, Q_BLOCK) if dil == 1 else pl.ds(nat_start, Q_BLOCK, stride=dil)
            for ref, value in zip(natural_order, results):
                ref[p, nat, :] = value

    for block in range(N_BLOCKS):
        prepare_dense(block)

    for n in range(N_BLOCKS):
        first = 2 * N_PATTERNS if n == 0 else 0
        attend(0, 1, n * Q_BLOCK, True, n * Q_BLOCK, (first, first + 1))
        prepare_first_level(n)

    dil4 = DIL_PATTERNS[1][1]
    run = SEQ // dil4
    for n in range(run // Q_BLOCK):
        for r in range(dil4):
            attend(1, dil4, r * run + n * Q_BLOCK, n > 0, n * (Q_BLOCK * dil4) + r, (2, 3))
            prepare_second_level(n * dil4 + r)

    dil16 = DIL_PATTERNS[2][1]
    for r in range(dil16):
        attend(2, dil16, r * Q_BLOCK, False, r, (4, 5))

    rows_per_step = 256

    def blend(i, carry):
        sl = pl.ds(pl.multiple_of(i * rows_per_step, rows_per_step), rows_per_step)
        per_class = rows_per_step // step
        for r4 in range(step):
            src = pl.ds(pl.multiple_of(r4 * run4 + i * per_class, per_class), per_class)
            dst = pl.ds(i * rows_per_step + r4, per_class, stride=step)
            for j, ref in enumerate(natural_order):
                ref[N_PATTERNS - 1, dst, :] = staging[j, src, :]
        add = lambda a, b: a + b

        def weighted(m_ref, x_ref):
            ms = [m_ref[p, sl, :] for p in range(N_PATTERNS)]
            mx = functools.reduce(jnp.maximum, ms)
            return functools.reduce(add, [jnp.exp2(ms[p] - mx) * x_ref[p, sl, :] for p in range(N_PATTERNS)])

        num = weighted(m_acc, o_acc)
        den = pltpu.roll(weighted(m_swapped, l_acc), HEAD_DIM, axis=1)
        o_ref[sl, :] = (num * (1.0 / den)).astype(BF16)
        return carry

    lax.fori_loop(0, SEQ // rows_per_step, blend, 0)


def _dilated_attention(slopes, dil3):
    batch = dil3.shape[0]
    n_pairs = DIL_WIDTH // LANES
    q_off, k_off, v_off = 0, n_pairs, 2 * n_pairs
    return pl.pallas_call(
        _dil_attn_kernel,
        grid=(batch, n_pairs),
        in_specs=[
            pl.BlockSpec(memory_space=pltpu.SMEM),
            pl.BlockSpec((None, SEQ, LANES), lambda b, h: (b, 0, q_off + h)),
            pl.BlockSpec((None, SEQ, LANES), lambda b, h: (b, 0, k_off + h)),
            pl.BlockSpec((None, SEQ, LANES), lambda b, h: (b, 0, v_off + h)),
        ],
        out_specs=pl.BlockSpec((None, SEQ, LANES), lambda b, h: (b, 0, h)),
        out_shape=jax.ShapeDtypeStruct((batch, SEQ, DIL_WIDTH), BF16),
        scratch_shapes=[
            pltpu.VMEM((SEQ, LANES), F32),
            pltpu.VMEM((SEQ, LANES), F32),
            pltpu.VMEM((SEQ, LANES), F32),
            pltpu.VMEM((N_PATTERNS, SEQ, LANES), BF16),
            pltpu.VMEM((N_PATTERNS, LANES, KPAD + SEQ), BF16),
            pltpu.VMEM((N_PATTERNS, KPAD + SEQ, LANES), BF16),
            pltpu.VMEM((2 * N_PATTERNS + 2, Q_BLOCK, 2 * Q_BLOCK), F32),
            pltpu.VMEM((N_PATTERNS, SEQ, LANES), F32),
            pltpu.VMEM((N_PATTERNS, SEQ, LANES), F32),
            pltpu.VMEM((N_PATTERNS, SEQ, LANES), F32),
            pltpu.VMEM((N_PATTERNS, SEQ, LANES), F32),
            pltpu.VMEM((4, SEQ, LANES), F32),
        ],
        compiler_params=pltpu.CompilerParams(
            dimension_semantics=("arbitrary", "arbitrary"), vmem_limit_bytes=VMEM_LIMIT_ATTN),
        name="dilated_attention",
    )(slopes, dil3, dil3, dil3)


def _alibi_slopes():
    all_s = 2.0 ** (-8.0 * np.arange(1, N_ALIBI_HEADS + 1, dtype=np.float32) / N_ALIBI_HEADS)
    diff_idx = np.arange(0, N_ALIBI_HEADS, N_ALIBI_HEADS // N_DIFF_HEADS)
    dil_idx = np.setdiff1d(np.arange(N_ALIBI_HEADS), diff_idx)
    return tuple(float(s) for s in all_s[diff_idx]), jnp.asarray(all_s[dil_idx], F32)


def kernel(x, ffn1_pre_g, ffn1_w_gate, ffn1_w_up, ffn1_w_down, ffn1_post_g, mix_pre_g, w_in, lambda_q1, lambda_k1, lambda_q2, lambda_k2, diff_subln_g, w_out, mix_post_g, ffn2_pre_g, ffn2_w_gate, ffn2_w_up, ffn2_w_down, ffn2_post_g):
    batch, seq, d_model = x.shape
    assert (seq, d_model) == (SEQ, D_MODEL) and ffn1_pre_g.shape[0] == 1
    slopes_d, slopes_l = _alibi_slopes()

    scale = HEAD_DIM ** -0.5 * LOG2E
    col_scale = np.ones((IN_PROJ_WIDTH,), np.float32)
    col_scale[:DIFF_QK_WIDTH] = scale
    q_l0 = 2 * DIFF_QK_WIDTH + DIFF_V_WIDTH
    col_scale[q_l0:q_l0 + DIL_WIDTH] = scale

    l = 0
    row = lambda g: g[l].reshape(1, -1)
    x2d = x.reshape(batch * seq, d_model)
    x1, qk, vt, dil = _ffn1_inproj(
        x2d, row(ffn1_pre_g), ffn1_w_gate[l].astype(BF16), ffn1_w_up[l].astype(BF16),
        ffn1_w_down[l].astype(BF16), row(ffn1_post_g), row(mix_pre_g),
        (w_in[l] * col_scale).astype(BF16), batch)
    o_d = _diff_attention(slopes_d, qk.reshape(batch, seq, 2 * DIFF_QK_WIDTH), vt, row(lambda_q1),
                          row(lambda_k1), row(lambda_q2), row(lambda_k2),
                          diff_subln_g[l].reshape(DIFF_V_DIM, 1))
    o_l = _dilated_attention(slopes_l, dil.reshape(batch, seq, 3 * DIL_WIDTH))
    w_o = w_out[l].astype(BF16)
    out = _outproj_ffn2(
        o_d.reshape(batch * seq, DIFF_V_WIDTH), o_l.reshape(batch * seq, DIL_WIDTH), x1,
        w_o[:DIFF_V_WIDTH], w_o[DIFF_V_WIDTH:], row(mix_post_g), row(ffn2_pre_g),
        ffn2_w_gate[l].astype(BF16), ffn2_w_up[l].astype(BF16), ffn2_w_down[l].astype(BF16),
        row(ffn2_post_g))
    return out.reshape(batch, seq, d_model)
```

```python
import functools
import math

import numpy as np
import jax
import jax.numpy as jnp
from jax import lax
from jax.experimental import pallas as pl
from jax.experimental.pallas import tpu as pltpu

F32 = jnp.float32
BF16 = jnp.bfloat16

D_MODEL = 1024
SEQ = 2048
HEAD_DIM = 64
N_DIFF_HEADS = 4
DIFF_V_DIM = 2 * HEAD_DIM
N_DIL_HEADS = 8
DIL_PATTERNS = ((128, 1), (512, 4), (2048, 16))
Q_BLOCK = 128
D_FF = 2816
RMS_EPS = 1e-6

DIFF_QK_WIDTH = N_DIFF_HEADS * 2 * HEAD_DIM
DIFF_V_WIDTH = N_DIFF_HEADS * DIFF_V_DIM
DIL_WIDTH = N_DIL_HEADS * HEAD_DIM
IN_PROJ_WIDTH = 2 * DIFF_QK_WIDTH + DIFF_V_WIDTH + 3 * DIL_WIDTH
DIL_OFFSET = 2 * DIFF_QK_WIDTH + DIFF_V_WIDTH
N_ALIBI_HEADS = N_DIFF_HEADS + N_DIL_HEADS
LAM_INIT = 0.8 - 0.6 * math.exp(-0.3 * 0)

LANES = 128
NEG_BIG = -1e30
VMEM_LIMIT_FFN = 56 * 1024 * 1024
VMEM_LIMIT_ATTN = 52 * 1024 * 1024

TM_IN = 512
TM_OUT = 1024
TM_SUB = 256
TQ = 256
TK = 256


def _rms(x, g):
    ms = jnp.mean(x * x, axis=-1, keepdims=True)
    return x * lax.rsqrt(ms + RMS_EPS) * g


def _row_groups(tm):
    return [slice(r * TM_SUB, (r + 1) * TM_SUB) for r in range(tm // TM_SUB)]


def _swiglu_half_step(xs, pre_g, wg_ref, wu_ref, wd_ref, post_g):
    xb = [_rms(x, pre_g).astype(BF16) for x in xs]
    gu = [(jnp.dot(b, wg_ref[...], preferred_element_type=F32),
           jnp.dot(b, wu_ref[...], preferred_element_type=F32)) for b in xb]
    ys = []
    for g, u in gu:
        h = (g * (1.0 / (1.0 + jnp.exp(-g))) * u).astype(BF16)
        ys.append(jnp.dot(h, wd_ref[...], preferred_element_type=F32))
    return [x + 0.5 * _rms(y, post_g) for x, y in zip(xs, ys)]


def _ffn1_inproj_kernel(x_ref, pre_g, wg, wu, wd, post_g, mix_g, w_in, x1_ref, qk_ref, vt_ref, dil_ref):
    groups = _row_groups(TM_IN)
    x1s = _swiglu_half_step([x_ref[rows, :] for rows in groups], pre_g[...], wg, wu, wd, post_g[...])
    for rows, x1 in zip(groups, x1s):
        x1_ref[rows, :] = x1
        hb = _rms(x1, mix_g[...]).astype(BF16)
        proj = jnp.dot(hb, w_in[...], preferred_element_type=F32)
        qk_ref[rows, :] = proj[:, :2 * DIFF_QK_WIDTH].astype(BF16)
        v_d = proj[:, 2 * DIFF_QK_WIDTH:DIL_OFFSET]
        vt_ref[:, rows] = v_d.T.astype(BF16)
        dil_ref[rows, :] = proj[:, DIL_OFFSET:]


def _const_spec(shape):
    nd = len(shape)
    return pl.BlockSpec(shape, lambda *_: (0,) * nd, pipeline_mode=pl.Buffered(1))


def _ffn1_inproj(x2d, pre_g, wg, wu, wd, post_g, mix_g, w_in, batch):
    t = x2d.shape[0]
    tm = TM_IN
    tiles_per_seq = SEQ // tm
    row = lambda i: (i, 0)
    return pl.pallas_call(
        _ffn1_inproj_kernel,
        grid=(t // tm,),
        in_specs=[
            pl.BlockSpec((tm, D_MODEL), row),
            _const_spec((1, D_MODEL)),
            _const_spec((D_MODEL, D_FF)),
            _const_spec((D_MODEL, D_FF)),
            _const_spec((D_FF, D_MODEL)),
            _const_spec((1, D_MODEL)),
            _const_spec((1, D_MODEL)),
            _const_spec((D_MODEL, IN_PROJ_WIDTH)),
        ],
        out_specs=[
            pl.BlockSpec((tm, D_MODEL), row),
            pl.BlockSpec((tm, 2 * DIFF_QK_WIDTH), row),
            pl.BlockSpec((None, DIFF_V_WIDTH, tm), lambda i: (i // tiles_per_seq, 0, i % tiles_per_seq)),
            pl.BlockSpec((tm, 3 * DIL_WIDTH), row),
        ],
        out_shape=[
            jax.ShapeDtypeStruct((t, D_MODEL), F32),
            jax.ShapeDtypeStruct((t, 2 * DIFF_QK_WIDTH), BF16),
            jax.ShapeDtypeStruct((batch, DIFF_V_WIDTH, SEQ), BF16),
            jax.ShapeDtypeStruct((t, 3 * DIL_WIDTH), F32),
        ],
        compiler_params=pltpu.CompilerParams(
            dimension_semantics=("arbitrary",), vmem_limit_bytes=VMEM_LIMIT_FFN),
        name="ffn1_inproj",
    )(x2d, pre_g, wg, wu, wd, post_g, mix_g, w_in)


def _outproj_ffn2_kernel(od_ref, ol_ref, x1_ref, wo_d, wo_l, mix_post_g, pre_g, wg, wu, wd, post_g, out_ref):
    groups = _row_groups(TM_OUT)
    x2s = []
    for rows in groups:
        h = (jnp.dot(od_ref[rows, :], wo_d[...], preferred_element_type=F32)
             + jnp.dot(ol_ref[rows, :], wo_l[...], preferred_element_type=F32))
        x2s.append(x1_ref[rows, :] + _rms(h, mix_post_g[...]))
    outs = _swiglu_half_step(x2s, pre_g[...], wg, wu, wd, post_g[...])
    for rows, out in zip(groups, outs):
        out_ref[rows, :] = out


def _outproj_ffn2(od, ol, x1, wo_d, wo_l, mix_post_g, pre_g, wg, wu, wd, post_g):
    t = x1.shape[0]
    tm = TM_OUT
    row = lambda i: (i, 0)
    return pl.pallas_call(
        _outproj_ffn2_kernel,
        grid=(t // tm,),
        in_specs=[
            pl.BlockSpec((tm, DIFF_V_WIDTH), row),
            pl.BlockSpec((tm, DIL_WIDTH), row),
            pl.BlockSpec((tm, D_MODEL), row),
            _const_spec((DIFF_V_WIDTH, D_MODEL)),
            _const_spec((DIL_WIDTH, D_MODEL)),
            _const_spec((1, D_MODEL)),
            _const_spec((1, D_MODEL)),
            _const_spec((D_MODEL, D_FF)),
            _const_spec((D_MODEL, D_FF)),
            _const_spec((D_FF, D_MODEL)),
            _const_spec((1, D_MODEL)),
        ],
        out_specs=pl.BlockSpec((tm, D_MODEL), row),
        out_shape=jax.ShapeDtypeStruct((t, D_MODEL), F32),
        compiler_params=pltpu.CompilerParams(
            dimension_semantics=("arbitrary",), vmem_limit_bytes=VMEM_LIMIT_FFN),
        name="outproj_ffn2",
    )(od, ol, x1, wo_d, wo_l, mix_post_g, pre_g, wg, wu, wd, post_g)


_NT = (((1,), (1,)), ((), ()))
N_MAPS = 2 * N_DIFF_HEADS
ONES_ROWS = 16
SLOPE_TERMS = 3
N_QT = SEQ // TQ
N_TILES = N_QT * (N_QT + 1) // 2
TILES_PER_TRIP = 6
LOG2E = math.log2(math.e)


def _bf16_terms(x, n_terms):
    terms, rest = [], np.float32(x)
    for _ in range(n_terms):
        t = np.float32(np.asarray(rest, dtype=BF16))
        terms.append(float(t))
        rest = np.float32(rest - t)
    return terms


def _diff_attn_kernel(q_ref, k_ref, vt_ref, lq1, lk1, lq2, lk2, g_ref, o_ref,
                      kp, vta, masks, s_buf, acc, *, slopes):
    lam = (jnp.exp(jnp.sum(lq1[...] * lk1[...], axis=-1, keepdims=True))
           - jnp.exp(jnp.sum(lq2[...] * lk2[...], axis=-1, keepdims=True)) + LAM_INIT)
    lane = lax.broadcasted_iota(jnp.int32, (1, LANES), 1)
    first_map = lane < HEAD_DIM

    def in_lanes(base, values):
        row = jnp.zeros((1, LANES), F32)
        for i, v in enumerate(values):
            row = jnp.where(lane == base + i, v, row)
        return row.astype(BF16)

    slope_rows = [(in_lanes(HEAD_DIM, _bf16_terms(s * LOG2E, SLOPE_TERMS)),
                   in_lanes(0, _bf16_terms(s * LOG2E, SLOPE_TERMS))) for s in slopes]
    key_idx = lax.broadcasted_iota(jnp.int32, (TK, LANES), 0).astype(F32)
    idx_first = jnp.where((lane >= HEAD_DIM) & (lane < HEAD_DIM + SLOPE_TERMS), key_idx, 0.0).astype(BF16)
    idx_second = jnp.where(lane < SLOPE_TERMS, key_idx, 0.0).astype(BF16)

    def augment(i, carry):
        rows = pl.ds(pl.multiple_of(i * TK, TK), TK)
        for h in range(N_DIFF_HEADS):
            k = k_ref[rows, h * LANES:(h + 1) * LANES]
            kp[2 * h, rows, :] = jnp.where(first_map, k, idx_first)
            kp[2 * h + 1, rows, :] = jnp.where(first_map, idx_second, k)
        return carry

    lax.fori_loop(0, SEQ // TK, augment, 0)

    def queries(c, qs):
        h = c // 2
        q = q_ref[pl.ds(qs, TQ), h * LANES:(h + 1) * LANES]
        if c % 2 == 0:
            return jnp.where(first_map, q, slope_rows[h][0])
        return jnp.where(first_map, slope_rows[h][1], q)

    for h in range(N_DIFF_HEADS):
        vta[h, 0:DIFF_V_DIM, :] = vt_ref[h * DIFF_V_DIM:(h + 1) * DIFF_V_DIM, :]
        vta[h, DIFF_V_DIM:, :] = jnp.ones((ONES_ROWS, SEQ), BF16)
    causal = (lax.broadcasted_iota(jnp.int32, (TK, TQ), 1)
              >= lax.broadcasted_iota(jnp.int32, (TK, TQ), 0))
    masks[0] = jnp.zeros((TK, TQ), F32)
    masks[1] = jnp.where(causal, 0.0, NEG_BIG)

    def clear(qi, carry):
        acc[qi] = jnp.zeros((N_MAPS, DIFF_V_DIM + ONES_ROWS, TQ), F32)
        return carry

    @pl.when(pl.program_id(0) == 0)
    def _():
        lax.fori_loop(0, N_QT, clear, 0)

    def score_matmuls(qi, t, slot):
        qs = pl.multiple_of(qi * TQ, TQ)
        ks = pl.multiple_of((qi - t) * TK, TK)
        mask = masks[(t == 0).astype(jnp.int32)]
        tile_max = []
        for c in range(N_MAPS):
            s = lax.dot_general(kp[c, pl.ds(ks, TK), :], queries(c, qs), _NT,
                                preferred_element_type=F32) + mask
            s_buf[slot, c] = s
            tile_max.append(jnp.max(s, axis=0, keepdims=True))
        return tuple(tile_max)

    def tile_step(qi, t, slot, tile_max, maxima, q_next, t_next):
        on_diagonal = t == 0
        dist = jnp.asarray(t * TK, F32)
        ks = pl.multiple_of((qi - t) * TK, TK)
        qs_next = pl.multiple_of(q_next * TQ, TQ)
        ks_next = pl.multiple_of((q_next - t_next) * TK, TK)
        mask_next = masks[(t_next == 0).astype(jnp.int32)]
        next_max, new_m = [], []

        def next_scores(c):
            s = lax.dot_general(kp[c, pl.ds(ks_next, TK), :], queries(c, qs_next), _NT,
                                preferred_element_type=F32) + mask_next
            s_buf[1 - slot, c] = s
            next_max.append(jnp.max(s, axis=0, keepdims=True))

        next_scores(0)
        next_scores(1)
        for c in range(N_MAPS):
            coff = (slopes[c // 2] * LOG2E) * dist
            m = jnp.where(on_diagonal, NEG_BIG, maxima[c])
            n = jnp.maximum(m, tile_max[c] - coff)
            new_m.append(n)
            p = jnp.exp2(s_buf[slot, c] - (n + coff)).astype(BF16)
            pv = jnp.dot(vta[c // 2, :, pl.ds(ks, TK)], p, preferred_element_type=F32)
            acc[qi, c] = jnp.exp2(m - n) * acc[qi, c] + pv
            if c + 2 < N_MAPS:
                next_scores(c + 2)
        return tuple(next_max), tuple(new_m)

    def next_tile(qi, t):
        last = t == qi
        nq = jnp.where(last, qi + 1, qi)
        nt = jnp.where(last, 0, t + 1)
        return jnp.minimum(nq, N_QT - 1), nt

    zero = jnp.int32(0)
    first_max = score_matmuls(zero, zero, 0)
    row = lambda v: tuple(jnp.full((1, TQ), v, F32) for _ in range(N_MAPS))

    def several_tiles(i, carry):
        qi, t, tile_max, maxima = carry
        for u in range(TILES_PER_TRIP):
            q_next, t_next = next_tile(qi, t)
            tile_max, maxima = tile_step(qi, t, u % 2, tile_max, maxima, q_next, t_next)
            qi, t = q_next, t_next
        return qi, t, tile_max, maxima

    assert TILES_PER_TRIP % 2 == 0 and N_TILES % TILES_PER_TRIP == 0
    lax.fori_loop(0, N_TILES // TILES_PER_TRIP, several_tiles, (zero, zero, first_max, row(0.0)))

    def finish(qi, carry):
        qs = pl.multiple_of(qi * TQ, TQ)
        for h in range(N_DIFF_HEADS):
            a1, a2 = acc[qi, 2 * h], acc[qi, 2 * h + 1]
            l1, l2 = a1[DIFF_V_DIM:DIFF_V_DIM + 1, :], a2[DIFF_V_DIM:DIFF_V_DIM + 1, :]
            o = a1[:DIFF_V_DIM, :] * (1.0 / l1) - lam * (a2[:DIFF_V_DIM, :] * (1.0 / l2))
            ms = jnp.mean(o * o, axis=0, keepdims=True)
            y = o * lax.rsqrt(ms + RMS_EPS) * g_ref[...] * (1.0 - LAM_INIT)
            o_ref[pl.ds(qs, TQ), h * LANES:(h + 1) * LANES] = y.T.astype(BF16)
        return carry

    lax.fori_loop(0, N_QT, finish, 0)


def _diff_attention(slopes, qk3, vt, lq1, lk1, lq2, lk2, g_col):
    batch = qk3.shape[0]
    vec = pl.BlockSpec((1, HEAD_DIM), lambda b: (0, 0))
    return pl.pallas_call(
        functools.partial(_diff_attn_kernel, slopes=slopes),
        grid=(batch,),
        in_specs=[
            pl.BlockSpec((None, SEQ, DIFF_QK_WIDTH), lambda b: (b, 0, 0)),
            pl.BlockSpec((None, SEQ, DIFF_QK_WIDTH), lambda b: (b, 0, 1)),
            pl.BlockSpec((None, DIFF_V_WIDTH, SEQ), lambda b: (b, 0, 0)),
            vec, vec, vec, vec,
            pl.BlockSpec((DIFF_V_DIM, 1), lambda b: (0, 0)),
        ],
        out_specs=pl.BlockSpec((None, SEQ, DIFF_V_WIDTH), lambda b: (b, 0, 0)),
        out_shape=jax.ShapeDtypeStruct((batch, SEQ, DIFF_V_WIDTH), BF16),
        scratch_shapes=[
            pltpu.VMEM((N_MAPS, SEQ, LANES), BF16),
            pltpu.VMEM((N_DIFF_HEADS, DIFF_V_DIM + ONES_ROWS, SEQ), BF16),
            pltpu.VMEM((2, TK, TQ), F32),
            pltpu.VMEM((2, N_MAPS, TK, TQ), F32),
            pltpu.VMEM((N_QT, N_MAPS, DIFF_V_DIM + ONES_ROWS, TQ), F32),
        ],
        compiler_params=pltpu.CompilerParams(
            dimension_semantics=("arbitrary",), vmem_limit_bytes=VMEM_LIMIT_ATTN),
        name="diff_attention",
    )(qk3, qk3, vt, lq1, lk1, lq2, lk2, g_col)


N_PATTERNS = len(DIL_PATTERNS)
N_WIN = Q_BLOCK
N_BLOCKS = SEQ // Q_BLOCK
KPAD = Q_BLOCK


def _dil_attn_kernel(slopes_ref, q_ref, k_ref, v_ref, o_ref,
                     q4, k4, v4, qq, kt, vv, bias, o_acc, l_acc, m_acc, m_swapped, staging):
    hp = pl.program_id(1)
    lane = lax.broadcasted_iota(jnp.int32, (1, LANES), 1)
    first_head = lane < HEAD_DIM

    kt[0, :, 0:KPAD] = jnp.zeros((LANES, KPAD), BF16)
    vv[0, 0:KPAD, :] = jnp.zeros((KPAD, LANES), BF16)

    col = lax.broadcasted_iota(jnp.int32, (Q_BLOCK, 2 * Q_BLOCK), 1)
    dsub = Q_BLOCK + lax.broadcasted_iota(jnp.int32, (Q_BLOCK, 2 * Q_BLOCK), 0) - col
    valid = (dsub >= 0) & (dsub <= N_WIN)
    dsub_f = dsub.astype(F32)
    for p, (_, dil) in enumerate(DIL_PATTERNS):
        for hh in range(2):
            slope = slopes_ref[2 * hp + hh]
            b = jnp.where(valid, (-slope * (LOG2E * dil)) * dsub_f, NEG_BIG)
            bias[2 * p + hh] = b
            if p == 0:
                bias[2 * N_PATTERNS + hh] = jnp.where(col >= Q_BLOCK, b, NEG_BIG)

    def emit(p, block, q, k, v):
        dst = block * Q_BLOCK
        qq[p, pl.ds(dst, Q_BLOCK), :] = q.astype(BF16)
        kt[p, :, pl.ds(KPAD + dst, Q_BLOCK)] = k.T.astype(BF16)
        vv[p, pl.ds(KPAD + dst, Q_BLOCK), :] = v.astype(BF16)

    step = DIL_PATTERNS[1][1]
    assert DIL_PATTERNS[2][1] == step * step and N_BLOCKS == step * step
    run4 = SEQ // step

    def prepare_dense(block):
        rows = pl.ds(block * Q_BLOCK, Q_BLOCK)
        emit(0, block, q_ref[rows, :], k_ref[rows, :], v_ref[rows, :])

    def prepare_first_level(block):
        r4, n = divmod(block, step)
        rows = pl.ds(block * Q_BLOCK, Q_BLOCK)
        src = pl.ds(n * (Q_BLOCK * step) + r4, Q_BLOCK, stride=step)
        q, k, v = q_ref[src, :], k_ref[src, :], v_ref[src, :]
        q4[rows, :] = q
        k4[rows, :] = k
        v4[rows, :] = v
        emit(1, block, q, k, v)

    def prepare_second_level(r16):
        a, r4 = divmod(r16, step)
        src = pl.ds(r4 * run4 + a, Q_BLOCK, stride=step)
        emit(2, r16, q4[src, :], k4[src, :], v4[src, :])

    natural_order = (o_acc, l_acc, m_acc, m_swapped)

    def attend(p, dil, q_row, with_prev, nat_start, bias_idx):
        n_keys = 2 * Q_BLOCK if with_prev else Q_BLOCK
        key_row = q_row + (KPAD - Q_BLOCK if with_prev else KPAD)
        if not isinstance(q_row, int):
            q_row = pl.multiple_of(q_row, Q_BLOCK)
            key_row = pl.multiple_of(key_row, Q_BLOCK)
        k_t = kt[p, :, pl.ds(key_row, n_keys)]
        q = qq[p, pl.ds(q_row, Q_BLOCK), :]
        v = vv[p, pl.ds(key_row, n_keys), :]
        zero, one = jnp.zeros((), BF16), jnp.ones((), BF16)
        outs = []
        for hh in range(2):
            qh = jnp.where(first_head, q, zero) if hh == 0 else jnp.where(first_head, zero, q)
            s = jnp.dot(qh, k_t, preferred_element_type=F32)
            if with_prev:
                s = s + bias[bias_idx[hh]]
            else:
                s = s + bias[bias_idx[hh], :, Q_BLOCK:]
            m = jnp.max(s, axis=1, keepdims=True)
            e = jnp.exp2(s - m).astype(BF16)
            vh = jnp.where(first_head, v, one) if hh == 0 else jnp.where(first_head, one, v)
            ov = jnp.dot(e, vh, preferred_element_type=F32)
            outs.append((ov, m))
        results = (jnp.where(first_head, outs[0][0], outs[1][0]), jnp.where(first_head, outs[1][0], outs[0][0]),
                   jnp.where(first_head, outs[0][1], outs[1][1]), jnp.where(first_head, outs[1][1], outs[0][1]))
        if dil == step * step:
            a, r4 = divmod(nat_start, step)
            staged = pl.ds(r4 * run4 + a, Q_BLOCK, stride=step)
            for j, value in enumerate(results):
                staging[j, staged, :] = value
        else:
            nat = pl.ds(nat_start, Q_BLOCK) if dil == 1 else pl.ds(nat_start, Q_BLOCK, stride=dil)
            for ref, value in zip(natural_order, results):
                ref[p, nat, :] = value

    for block in range(N_BLOCKS):
        prepare_dense(block)

    for n in range(N_BLOCKS):
        first = 2 * N_PATTERNS if n == 0 else 0
        attend(0, 1, n * Q_BLOCK, True, n * Q_BLOCK, (first, first + 1))
        prepare_first_level(n)

    dil4 = DIL_PATTERNS[1][1]
    run = SEQ // dil4
    for n in range(run // Q_BLOCK):
        for r in range(dil4):
            attend(1, dil4, r * run + n * Q_BLOCK, n > 0, n * (Q_BLOCK * dil4) + r, (2, 3))
            prepare_second_level(n * dil4 + r)

    dil16 = DIL_PATTERNS[2][1]
    for r in range(dil16):
        attend(2, dil16, r * Q_BLOCK, False, r, (4, 5))

    rows_per_step = 256

    def blend(i, carry):
        sl = pl.ds(pl.multiple_of(i * rows_per_step, rows_per_step), rows_per_step)
        per_class = rows_per_step // step
        for r4 in range(step):
            src = pl.ds(pl.multiple_of(r4 * run4 + i * per_class, per_class), per_class)
            dst = pl.ds(i * rows_per_step + r4, per_class, stride=step)
            for j, ref in enumerate(natural_order):
                ref[N_PATTERNS - 1, dst, :] = staging[j, src, :]
        add = lambda a, b: a + b

        def weighted(m_ref, x_ref):
            ms = [m_ref[p, sl, :] for p in range(N_PATTERNS)]
            mx = functools.reduce(jnp.maximum, ms)
            return functools.reduce(add, [jnp.exp2(ms[p] - mx) * x_ref[p, sl, :] for p in range(N_PATTERNS)])

        num = weighted(m_acc, o_acc)
        den = pltpu.roll(weighted(m_swapped, l_acc), HEAD_DIM, axis=1)
        o_ref[sl, :] = (num * (1.0 / den)).astype(BF16)
        return carry

    lax.fori_loop(0, SEQ // rows_per_step, blend, 0)


def _dilated_attention(slopes, dil3):
    batch = dil3.shape[0]
    n_pairs = DIL_WIDTH // LANES
    q_off, k_off, v_off = 0, n_pairs, 2 * n_pairs
    return pl.pallas_call(
        _dil_attn_kernel,
        grid=(batch, n_pairs),
        in_specs=[
            pl.BlockSpec(memory_space=pltpu.SMEM),
            pl.BlockSpec((None, SEQ, LANES), lambda b, h: (b, 0, q_off + h)),
            pl.BlockSpec((None, SEQ, LANES), lambda b, h: (b, 0, k_off + h)),
            pl.BlockSpec((None, SEQ, LANES), lambda b, h: (b, 0, v_off + h)),
        ],
        out_specs=pl.BlockSpec((None, SEQ, LANES), lambda b, h: (b, 0, h)),
        out_shape=jax.ShapeDtypeStruct((batch, SEQ, DIL_WIDTH), BF16),
        scratch_shapes=[
            pltpu.VMEM((SEQ, LANES), F32),
            pltpu.VMEM((SEQ, LANES), F32),
            pltpu.VMEM((SEQ, LANES), F32),
            pltpu.VMEM((N_PATTERNS, SEQ, LANES), BF16),
            pltpu.VMEM((N_PATTERNS, LANES, KPAD + SEQ), BF16),
            pltpu.VMEM((N_PATTERNS, KPAD + SEQ, LANES), BF16),
            pltpu.VMEM((2 * N_PATTERNS + 2, Q_BLOCK, 2 * Q_BLOCK), F32),
            pltpu.VMEM((N_PATTERNS, SEQ, LANES), F32),
            pltpu.VMEM((N_PATTERNS, SEQ, LANES), F32),
            pltpu.VMEM((N_PATTERNS, SEQ, LANES), F32),
            pltpu.VMEM((N_PATTERNS, SEQ, LANES), F32),
            pltpu.VMEM((4, SEQ, LANES), F32),
        ],
        compiler_params=pltpu.CompilerParams(
            dimension_semantics=("arbitrary", "arbitrary"), vmem_limit_bytes=VMEM_LIMIT_ATTN),
        name="dilated_attention",
    )(slopes, dil3, dil3, dil3)


def _alibi_slopes():
    all_s = 2.0 ** (-8.0 * np.arange(1, N_ALIBI_HEADS + 1, dtype=np.float32) / N_ALIBI_HEADS)
    diff_idx = np.arange(0, N_ALIBI_HEADS, N_ALIBI_HEADS // N_DIFF_HEADS)
    dil_idx = np.setdiff1d(np.arange(N_ALIBI_HEADS), diff_idx)
    return tuple(float(s) for s in all_s[diff_idx]), jnp.asarray(all_s[dil_idx], F32)


def kernel(x, ffn1_pre_g, ffn1_w_gate, ffn1_w_up, ffn1_w_down, ffn1_post_g, mix_pre_g, w_in, lambda_q1, lambda_k1, lambda_q2, lambda_k2, diff_subln_g, w_out, mix_post_g, ffn2_pre_g, ffn2_w_gate, ffn2_w_up, ffn2_w_down, ffn2_post_g):
    batch, seq, d_model = x.shape
    assert (seq, d_model) == (SEQ, D_MODEL) and ffn1_pre_g.shape[0] == 1
    slopes_d, slopes_l = _alibi_slopes()

    scale = HEAD_DIM ** -0.5 * LOG2E
    col_scale = np.ones((IN_PROJ_WIDTH,), np.float32)
    col_scale[:DIFF_QK_WIDTH] = scale
    q_l0 = 2 * DIFF_QK_WIDTH + DIFF_V_WIDTH
    col_scale[q_l0:q_l0 + DIL_WIDTH] = scale

    l = 0
    row = lambda g: g[l].reshape(1, -1)
    x2d = x.reshape(batch * seq, d_model)
    x1, qk, vt, dil = _ffn1_inproj(
        x2d, row(ffn1_pre_g), ffn1_w_gate[l].astype(BF16), ffn1_w_up[l].astype(BF16),
        ffn1_w_down[l].astype(BF16), row(ffn1_post_g), row(mix_pre_g),
        (w_in[l] * col_scale).astype(BF16), batch)
    o_d = _diff_attention(slopes_d, qk.reshape(batch, seq, 2 * DIFF_QK_WIDTH), vt, row(lambda_q1),
                          row(lambda_k1), row(lambda_q2), row(lambda_k2),
                          diff_subln_g[l].reshape(DIFF_V_DIM, 1))
    o_l = _dilated_attention(slopes_l, dil.reshape(batch, seq, 3 * DIL_WIDTH))
    w_o = w_out[l].astype(BF16)
    out = _outproj_ffn2(
        o_d.reshape(batch * seq, DIFF_V_WIDTH), o_l.reshape(batch * seq, DIL_WIDTH), x1,
        w_o[:DIFF_V_WIDTH], w_o[DIFF_V_WIDTH:], row(mix_post_g), row(ffn2_pre_g),
        ffn2_w_gate[l].astype(BF16), ffn2_w_up[l].astype(BF16), ffn2_w_down[l].astype(BF16),
        row(ffn2_post_g))
    return out.reshape(batch, seq, d_model)
```
